```python
import math
import jax, jax.numpy as jnp
from jax import lax
import numpy as np

D_MODEL = 1024
BATCH = 8
SEQ = 2048
DEPTH = 2

CHUNK = 64
N_META = 16
D_FF = 2816
N_BRANCH = 3
BRANCH_W = 512
RW_HEADS = 8
RW_HEAD = 64
RW_LORA_W = 64
RW_LORA_A = 64
RW_LORA_G = 128
RW_COLS = 3 * BRANCH_W + RW_LORA_W + RW_LORA_A + RW_LORA_G
RW_GN_EPS = 64e-5
ML_HEADS = 4
ML_HEAD = 128
ML_CONV = 4
ML_CHUNK = 64
ML_COLS = 4 * BRANCH_W + 2 * ML_HEADS
DSA_HEADS = 8
DSA_HEAD = 64
DSA_LATENT = 128
IDX_HEADS = 8
IDX_HEAD = 64
TOPK_MAX = 256
QBLOCK = 128
DSA_COLS = BRANCH_W + DSA_LATENT + IDX_HEADS * IDX_HEAD + IDX_HEAD + IDX_HEADS
GATE_COLS = N_BRANCH * D_MODEL
N_IN = RW_COLS + ML_COLS + DSA_COLS + GATE_COLS
NORM_EPS = 1e-6

kernel_name = "hybrid_rwkv7_mlstm_dsa_macaron_encoder"


def _split(x, sizes):
    return jnp.split(x, np.cumsum(sizes)[:-1].tolist(), axis=-1)


def _chunk_ids(t_len):
    pos = jnp.arange(t_len)
    return jnp.where(pos < N_META, 0, 1 + (pos - N_META) // CHUNK)


def _rms_norm(x, g):
    xf = x.astype(jnp.float32)
    y = xf * lax.rsqrt(jnp.mean(xf * xf, axis=-1, keepdims=True) + NORM_EPS)
    return (y * g.astype(jnp.float32)).astype(x.dtype)


def _swiglu(x, w_in, w_out):
    gate, up = jnp.split(x @ w_in, 2, axis=-1)
    return (jax.nn.silu(gate) * up) @ w_out


def _causal_dwconv(x, w, b):
    k_w, c = w.shape
    y = lax.conv_general_dilated(x, w[:, None, :].astype(x.dtype), window_strides=(1,),
                                 padding=[(k_w - 1, 0)], dimension_numbers=("NWC", "WIO", "NWC"),
                                 feature_group_count=c)
    return y + b


def _rwkv7_scan(r, w, k, v, kh, a):
    b, _, h, n = r.shape

    def step(s, xs):
        r_t, w_t, k_t, v_t, kh_t, a_t = xs
        sk = jnp.einsum("bhvk,bhk->bhv", s, kh_t)
        s = (s * w_t[:, :, None, :] - sk[..., None] * (a_t * kh_t)[:, :, None, :]
             + v_t[..., None] * k_t[:, :, None, :])
        return s, jnp.einsum("bhvk,bhk->bhv", s, r_t)

    xs = tuple(jnp.moveaxis(t, 1, 0) for t in (r, w, k, v, kh, a))
    _, y = lax.scan(step, jnp.zeros((b, h, n, n), jnp.float32), xs)
    return jnp.moveaxis(y, 0, 1)


def _rwkv7_mixer(p, mu, w0, w_up, a0, a_up, g_up, k_k, k_a, r_k, gn_gain, gn_bias):
    bsz, t_len, _ = p.shape
    f32 = jnp.float32
    prev = jnp.pad(p, ((0, 0), (1, 0), (0, 0)))[:, :-1]
    p = p + (prev - p) * mu
    r, k, v, w_lo, a_lo, g_lo = _split(p, (BRANCH_W, BRANCH_W, BRANCH_W, RW_LORA_W, RW_LORA_A, RW_LORA_G))
    d = (w0 + jnp.tanh(w_lo) @ w_up).astype(f32)
    decay = jnp.exp(-math.exp(-0.5) * jax.nn.sigmoid(d))
    a = jax.nn.sigmoid(a0 + a_lo @ a_up)
    gate = jax.nn.sigmoid(g_lo) @ g_up

    def heads(t):
        return t.astype(f32).reshape(bsz, t_len, RW_HEADS, RW_HEAD)

    kappa = heads(k * k_k)
    kh = kappa / jnp.maximum(jnp.sqrt(jnp.sum(kappa * kappa, axis=-1, keepdims=True)), 1e-12)
    k = k * (1.0 + (a - 1.0) * k_a)
    r_h, k_h, v_h, a_h = heads(r), heads(k), heads(v), heads(a)
    w_h = decay.reshape(bsz, t_len, RW_HEADS, RW_HEAD)
    y = _rwkv7_scan(r_h, w_h, k_h, v_h, kh, a_h)
    mean = jnp.mean(y, axis=-1, keepdims=True)
    var = jnp.mean((y - mean) ** 2, axis=-1, keepdims=True)
    y = (y - mean) * lax.rsqrt(var + RW_GN_EPS)
    y = y * gn_gain.astype(f32).reshape(RW_HEADS, RW_HEAD) + gn_bias.astype(f32).reshape(RW_HEADS, RW_HEAD)
    y = y + jnp.sum(r_h * k_h * r_k.astype(f32), axis=-1, keepdims=True) * v_h
    return (y.reshape(bsz, t_len, BRANCH_W) * gate.astype(f32)).astype(p.dtype)


def _mlstm_mixer(p, conv_w, conv_b, i_bias, f_bias, norm_gain):
    bsz, t_len, _ = p.shape
    f32 = jnp.float32
    qk, v, o, ig, fg = _split(p, (2 * BRANCH_W, BRANCH_W, BRANCH_W, ML_HEADS, ML_HEADS))
    qk = jax.nn.silu(_causal_dwconv(qk, conv_w, conv_b))
    q, k = jnp.split(qk, 2, axis=-1)
    nc = -(-t_len // ML_CHUNK)
    pad = nc * ML_CHUNK - t_len

    def to_chunks(t):
        t = jnp.pad(t.astype(f32), [(0, 0), (0, pad)] + [(0, 0)] * (t.ndim - 2))
        t = t.reshape((bsz, nc, ML_CHUNK) + t.shape[2:])
        return jnp.transpose(t, (1, 0, 3, 2) + tuple(range(4, t.ndim)))

    hd = (bsz, t_len, ML_HEADS, ML_HEAD)
    q_c = to_chunks(q.reshape(hd))
    k_c = to_chunks(k.reshape(hd)) * ML_HEAD ** -0.5
    v_c = to_chunks(v.reshape(hd))
    ig_c = to_chunks(ig + i_bias)
    lf_c = jax.nn.log_sigmoid(to_chunks(fg + f_bias))
    tril = jnp.tril(jnp.ones((ML_CHUNK, ML_CHUNK), bool))

    def step(carry, xs):
        c_st, n_st, m_st = carry
        q_t, k_t, v_t, i_t, lf_t = xs
        b = jnp.cumsum(lf_t, axis=-1)
        g_tot = b[..., -1]
        log_d = jnp.where(tril, b[..., :, None] - b[..., None, :] + i_t[..., None, :], -jnp.inf)
        m_j = jnp.maximum(b + m_st[..., None], jnp.max(log_d, axis=-1))
        s = jnp.einsum("bhjd,bhsd->bhjs", q_t, k_t) * jnp.exp(log_d - m_j[..., None])
        inter = jnp.exp(b + m_st[..., None] - m_j)
        num = jnp.einsum("bhjs,bhsd->bhjd", s, v_t) + inter[..., None] * jnp.einsum("bhvk,bhjk->bhjv", c_st, q_t)
        den = jnp.sum(s, axis=-1) + inter * jnp.einsum("bhk,bhjk->bhj", n_st, q_t)
        h = num / jnp.maximum(jnp.abs(den), jnp.exp(-m_j))[..., None]
        w_log = g_tot[..., None] - b + i_t
        m_new = jnp.maximum(g_tot + m_st, jnp.max(w_log, axis=-1))
        wgt = jnp.exp(w_log - m_new[..., None])
        dec = jnp.exp(g_tot + m_st - m_new)
        c_st = dec[..., None, None] * c_st + jnp.einsum("bhsv,bhsk->bhvk", wgt[..., None] * v_t, k_t)
        n_st = dec[..., None] * n_st + jnp.einsum("bhs,bhsk->bhk", wgt, k_t)
        return (c_st, n_st, m_new), h

    init = (jnp.zeros((bsz, ML_HEADS, ML_HEAD, ML_HEAD), f32),
            jnp.zeros((bsz, ML_HEADS, ML_HEAD), f32),
            jnp.zeros((bsz, ML_HEADS), f32))
    _, h = lax.scan(step, init, (q_c, k_c, v_c, ig_c, lf_c))
    h = jnp.transpose(h, (1, 0, 3, 2, 4)).reshape(bsz, nc * ML_CHUNK, ML_HEADS, ML_HEAD)[:, :t_len]
    h = h * lax.rsqrt(jnp.mean(h * h, axis=-1, keepdims=True) + NORM_EPS)
    h = h.reshape(bsz, t_len, BRANCH_W) * norm_gain.astype(f32)
    return (jax.nn.sigmoid(o.astype(f32)) * h).astype(p.dtype)


def _dsa_mixer(p, kv_norm, w_uk, w_uv, top_k):
    bsz, t_len, _ = p.shape
    f32 = jnp.float32
    q, c_kv, q_idx, k_idx, w_idx = _split(p, (BRANCH_W, DSA_LATENT, IDX_HEADS * IDX_HEAD, IDX_HEAD, IDX_HEADS))
    c_kv = _rms_norm(c_kv, kv_norm)
    q = q.reshape(bsz, t_len, DSA_HEADS, DSA_HEAD)
    q_lat = jnp.einsum("bthd,hdc->bthc", q, w_uk) * DSA_HEAD ** -0.5
    q_idx = q_idx.reshape(bsz, t_len, IDX_HEADS, IDX_HEAD)
    w_idx = w_idx * (IDX_HEADS * IDX_HEAD) ** -0.5
    cid = _chunk_ids(t_len)
    nb = -(-t_len // QBLOCK)
    pad = nb * QBLOCK - t_len

    def blocks(t):
        t = jnp.pad(t, [(0, 0), (0, pad)] + [(0, 0)] * (t.ndim - 2))
        return jnp.moveaxis(t.reshape((bsz, nb, QBLOCK) + t.shape[2:]), 1, 0)

    q_cid = jnp.pad(cid, (0, pad), mode="edge").reshape(nb, QBLOCK)

    def attend(args):
        ql, qi, wi, qc = args
        s = jnp.einsum("bqhd,bkd->bqhk", qi, k_idx)
        score = jnp.einsum("bqhk,bqh->bqk", jax.nn.relu(s), wi).astype(f32)
        allowed = cid[None, :] <= qc[:, None]
        score = jnp.where(allowed[None], score, -jnp.inf)
        top_s, top_i = lax.top_k(score, top_k)
        valid = jnp.isfinite(top_s)
        kv = jax.vmap(lambda c, i: c[i])(c_kv, top_i)
        logits = jnp.einsum("bqhc,bqkc->bqhk", ql, kv).astype(f32)
        logits = jnp.where(valid[:, :, None, :], logits, -jnp.inf)
        prob = jax.nn.softmax(logits, axis=-1).astype(kv.dtype)
        return jnp.einsum("bqhk,bqkc->bqhc", prob, kv)

    o = lax.map(attend, (blocks(q_lat), blocks(q_idx), blocks(w_idx), q_cid))
    o = jnp.moveaxis(o, 0, 1).reshape(bsz, nb * QBLOCK, DSA_HEADS, DSA_LATENT)[:, :t_len]
    y = jnp.einsum("bthc,hcd->bthd", o, w_uv)
    return y.reshape(bsz, t_len, BRANCH_W)


def setup_inputs(seed: int = 0) -> dict:
    key = jax.random.key(seed)
    ks = iter(jax.random.split(key, 32))
    f32 = jnp.float32

    def nrm(shape, scale):
        return jax.random.normal(next(ks), shape, f32) * scale

    return {
        "x": nrm((BATCH, SEQ, D_MODEL), 1.0),
        "meta_tokens": nrm((N_META, D_MODEL), 1.0),
        "norm_gain": 1.0 + nrm((DEPTH, 6, D_MODEL), 0.02),
        "ffn_w_in": nrm((DEPTH, 2, D_MODEL, 2 * D_FF), D_MODEL ** -0.5),
        "ffn_w_out": nrm((DEPTH, 2, D_FF, D_MODEL), D_FF ** -0.5),
        "w_in": nrm((DEPTH, D_MODEL, N_IN), D_MODEL ** -0.5),
        "rw_mu": jax.random.uniform(next(ks), (DEPTH, RW_COLS), f32),
        "rw_w0": jnp.linspace(-6.5, -1.5, BRANCH_W, dtype=f32) + nrm((DEPTH, BRANCH_W), 0.1),
        "rw_w_up": nrm((DEPTH, RW_LORA_W, BRANCH_W), RW_LORA_W ** -0.5),
        "rw_a0": nrm((DEPTH, BRANCH_W), 0.1),
        "rw_a_up": nrm((DEPTH, RW_LORA_A, BRANCH_W), RW_LORA_A ** -0.5),
        "rw_g_up": nrm((DEPTH, RW_LORA_G, BRANCH_W), RW_LORA_G ** -0.5),
        "rw_k_k": 0.85 + nrm((DEPTH, BRANCH_W), 0.02),
        "rw_k_a": 1.0 + nrm((DEPTH, BRANCH_W), 0.02),
        "rw_r_k": nrm((DEPTH, RW_HEADS, RW_HEAD), 0.1),
        "rw_gn_gain": 1.0 + nrm((DEPTH, BRANCH_W), 0.02),
        "rw_gn_bias": nrm((DEPTH, BRANCH_W), 0.02),
        "ml_conv_w": nrm((DEPTH, ML_CONV, 2 * BRANCH_W), ML_CONV ** -0.5),
        "ml_conv_b": nrm((DEPTH, 2 * BRANCH_W), 0.02),
        "ml_i_bias": nrm((DEPTH, ML_HEADS), 0.1),
        "ml_f_bias": jnp.linspace(3.0, 6.0, ML_HEADS, dtype=f32) + nrm((DEPTH, ML_HEADS), 0.1),
        "ml_norm_gain": 1.0 + nrm((DEPTH, BRANCH_W), 0.02),
        "dsa_kv_norm": 1.0 + nrm((DEPTH, DSA_LATENT), 0.02),
        "dsa_w_uk": nrm((DEPTH, DSA_HEADS, DSA_HEAD, DSA_LATENT), DSA_HEAD ** -0.5),
        "dsa_w_uv": nrm((DEPTH, DSA_HEADS, DSA_LATENT, DSA_HEAD), DSA_LATENT ** -0.5),
        "w_branch": nrm((DEPTH, N_BRANCH, BRANCH_W, D_MODEL), BRANCH_W ** -0.5),
        "w_out": nrm((DEPTH, D_MODEL, D_MODEL), D_MODEL ** -0.5),
    }


def reference(x, meta_tokens, norm_gain, ffn_w_in, ffn_w_out, w_in, rw_mu, rw_w0, rw_w_up, rw_a0,
              rw_a_up, rw_g_up, rw_k_k, rw_k_a, rw_r_k, rw_gn_gain, rw_gn_bias, ml_conv_w, ml_conv_b,
              ml_i_bias, ml_f_bias, ml_norm_gain, dsa_kv_norm, dsa_w_uk, dsa_w_uv, w_branch, w_out):
    bsz, seq, d = x.shape
    top_k = min(TOPK_MAX, seq // 4)
    h = jnp.concatenate([jnp.broadcast_to(meta_tokens.astype(x.dtype)[None], (bsz, N_META, d)), x], axis=1)
    t_len = seq + N_META
    for l in range(DEPTH):
        g = norm_gain[l]
        h = h + 0.5 * _rms_norm(_swiglu(_rms_norm(h, g[0]), ffn_w_in[l, 0], ffn_w_out[l, 0]), g[1])
        u = _rms_norm(h, g[2])
        p_rw, p_ml, p_dsa, p_gate = _split(u @ w_in[l], (RW_COLS, ML_COLS, DSA_COLS, GATE_COLS))
        y_rw = _rwkv7_mixer(p_rw, rw_mu[l], rw_w0[l], rw_w_up[l], rw_a0[l], rw_a_up[l], rw_g_up[l],
                            rw_k_k[l], rw_k_a[l], rw_r_k[l], rw_gn_gain[l], rw_gn_bias[l])
        y_ml = _mlstm_mixer(p_ml, ml_conv_w[l], ml_conv_b[l], ml_i_bias[l], ml_f_bias[l], ml_norm_gain[l])
        y_dsa = _dsa_mixer(p_dsa, dsa_kv_norm[l], dsa_w_uk[l], dsa_w_uv[l], top_k)
        branches = jnp.stack([y_rw, y_ml, y_dsa], axis=2)
        proj = jnp.einsum("btgc,gcd->btgd", branches, w_branch[l])
        gates = jax.nn.sigmoid(p_gate.reshape(bsz, t_len, N_BRANCH, d))
        mixed = jnp.sum(gates * proj, axis=2) @ w_out[l]
        h = h + _rms_norm(mixed, g[3])
        h = h + 0.5 * _rms_norm(_swiglu(_rms_norm(h, g[4]), ffn_w_in[l, 1], ffn_w_out[l, 1]), g[5])
    return h[:, N_META:]
```

```python
import functools
import math

import jax
import jax.numpy as jnp
from jax import lax
from jax.experimental import pallas as pl
from jax.experimental.pallas import tpu as pltpu

F32 = jnp.float32
BF16 = jnp.bfloat16
HI = lax.Precision.HIGHEST

D_MODEL = 1024
D_FF = 2816
N_META = 16
STREAM_CHUNK = 64
BRANCH_W = 512
NORM_EPS = 1e-6

RW_HEADS, RW_HEAD = 8, 64
RW_COLS = 1792
RW_GN_EPS = 64e-5
RW_CHUNK = 64

ML_HEADS, ML_HEAD = 4, 128
ML_CHUNK = 64
ML_CONV = 4
ML_COLS_PAD = 2176

DSA_HEADS, DSA_HEAD, DSA_LATENT = 8, 64, 128
IDX_HEADS, IDX_HEAD = 8, 64
TOPK_MAX = 256
DSA_QBLOCK = 128
DSA_COLS_PAD = 1280
DSA_KEY_COL_BLOCK = 256

ROW_TILE = 512
FF_TILE = 1408
SEQ_PAD_MULTIPLE = 128
VMEM_LIMIT = 56 * 1024 * 1024
DSA_BISECT_STEPS = 24
DSA_BISECT_REFINE = 8
DSA_BISECT_MAX_ROUNDS = 40


def _dot(a, b):
    return jnp.dot(a.astype(BF16), b.astype(BF16), preferred_element_type=F32)


def _dot_nt(a, b):
    return lax.dot_general(a.astype(BF16), b.astype(BF16), (((1,), (1,)), ((), ())),
                           preferred_element_type=F32)


def _dot_tn(a, b):
    return lax.dot_general(a.astype(BF16), b.astype(BF16), (((0,), (0,)), ((), ())),
                           preferred_element_type=F32)


def _dot_hi(a, b):
    return jnp.dot(a, b, preferred_element_type=F32, precision=HI)


def _dot_nt_hi(a, b):
    return lax.dot_general(a, b, (((1,), (1,)), ((), ())), preferred_element_type=F32, precision=HI)


def _dot_tn_hi(a, b):
    return lax.dot_general(a, b, (((0,), (0,)), ((), ())), preferred_element_type=F32, precision=HI)


_NN = ((1,), (0,))
_NT = ((1,), (1,))
_TN = ((0,), (0,))


def _mm(a, b, contract, passes):
    dn = (contract, ((), ()))
    if passes == 6:
        return lax.dot_general(a, b, dn, preferred_element_type=F32, precision=HI)
    dot = lambda x, y: lax.dot_general(x, y, dn, preferred_element_type=F32)
    a_hi, b_hi = a.astype(BF16), b.astype(BF16)
    if passes == 1:
        return dot(a_hi, b_hi)
    a_lo = (a - a_hi.astype(F32)).astype(BF16)
    b_lo = (b - b_hi.astype(F32)).astype(BF16)
    return dot(a_hi, b_hi) + (dot(a_hi, b_lo) + dot(a_lo, b_hi))


RW_PASSES = {"gram": 1, "carry": 1, "solve": 1, "out": 1, "state": 1}


def _rms(x, gain):
    return x * lax.rsqrt(jnp.mean(x * x, axis=-1, keepdims=True) + NORM_EPS) * gain


def _sigmoid(x):
    return jax.nn.sigmoid(x)


def _tril(n, strict=False):
    r = lax.broadcasted_iota(jnp.int32, (n, n), 0)
    c = lax.broadcasted_iota(jnp.int32, (n, n), 1)
    return (c < r) if strict else (c <= r)


def _params(*sem):
    return pltpu.CompilerParams(dimension_semantics=sem, vmem_limit_bytes=VMEM_LIMIT)


def _ffn_kernel(h_ref, gpre_ref, gpost_ref, wg_ref, wu_ref, wo_ref, o_ref, xn_ref, acc_ref):
    j = pl.program_id(1)

    @pl.when(j == 0)
    def _():
        xn_ref[...] = _rms(h_ref[...], gpre_ref[...]).astype(BF16)
        acc_ref[...] = jnp.zeros_like(acc_ref)

    xn = xn_ref[...]
    gate = jnp.dot(xn, wg_ref[...], preferred_element_type=F32)
    up = jnp.dot(xn, wu_ref[...], preferred_element_type=F32)
    act = gate * _sigmoid(gate) * up
    acc_ref[...] += jnp.dot(act.astype(BF16), wo_ref[...], preferred_element_type=F32)

    @pl.when(j == pl.num_programs(1) - 1)
    def _():
        o_ref[...] = h_ref[...] + 0.5 * _rms(acc_ref[...], gpost_ref[...])


def _ffn(h, g_pre, g_post, w_in, w_out):
    m, d = h.shape
    nj = D_FF // FF_TILE
    return pl.pallas_call(
        _ffn_kernel,
        out_shape=jax.ShapeDtypeStruct((m, d), F32),
        grid=(m // ROW_TILE, nj),
        in_specs=[
            pl.BlockSpec((ROW_TILE, d), lambda i, j: (i, 0)),
            pl.BlockSpec((1, d), lambda i, j: (0, 0)),
            pl.BlockSpec((1, d), lambda i, j: (0, 0)),
            pl.BlockSpec((d, FF_TILE), lambda i, j: (0, j)),
            pl.BlockSpec((d, FF_TILE), lambda i, j: (0, j + nj)),
            pl.BlockSpec((FF_TILE, d), lambda i, j: (j, 0)),
        ],
        out_specs=pl.BlockSpec((ROW_TILE, d), lambda i, j: (i, 0)),
        scratch_shapes=[pltpu.VMEM((ROW_TILE, d), BF16), pltpu.VMEM((ROW_TILE, d), F32)],
        compiler_params=_params("parallel", "arbitrary"),
        name="ffn",
    )(h, g_pre.reshape(1, d), g_post.reshape(1, d), w_in, w_in, w_out)


def _proj_kernel(h_ref, g_ref, w_ref, o_ref):
    o_ref[...] = jnp.dot(_rms(h_ref[...], g_ref[...]).astype(BF16), w_ref[...],
                         preferred_element_type=F32)


def _proj(h, gain, w):
    m, d = h.shape
    n = w.shape[1]
    return pl.pallas_call(
        _proj_kernel,
        out_shape=jax.ShapeDtypeStruct((m, n), F32),
        grid=(m // ROW_TILE,),
        in_specs=[
            pl.BlockSpec((ROW_TILE, d), lambda i: (i, 0)),
            pl.BlockSpec((1, d), lambda i: (0, 0)),
            pl.BlockSpec((d, n), lambda i: (0, 0)),
        ],
        out_specs=pl.BlockSpec((ROW_TILE, n), lambda i: (i, 0)),
        compiler_params=_params("parallel"),
        name="in_proj",
    )(h, gain.reshape(1, d), w)


def _rwkv_kernel(p_ref, mu_ref, w0_ref, wup_ref, a0_ref, aup_ref, gup_ref, kk_ref, ka_ref, rk_ref,
                 gng_ref, gnb_ref, o_ref, prev_ref, s_ref):
    L = RW_CHUNK

    @pl.when(pl.program_id(1) == 0)
    def _():
        prev_ref[...] = jnp.zeros_like(prev_ref)
        s_ref[...] = jnp.zeros_like(s_ref)

    p = p_ref[0]
    row = lax.broadcasted_iota(jnp.int32, (L, 1), 0)
    prev = jnp.where(row == 0, prev_ref[...], pltpu.roll(p, 1, 0))
    prev_ref[...] = p[L - 1:L, :]
    ps = p + (prev - p) * mu_ref[...]

    r = ps[:, 0:BRANCH_W]
    k = ps[:, BRANCH_W:2 * BRANCH_W]
    v = ps[:, 2 * BRANCH_W:3 * BRANCH_W]
    w_lo = ps[:, 1536:1600]
    a_lo = ps[:, 1600:1664]
    g_lo = ps[:, 1664:1792]

    d = w0_ref[...] + _dot(jnp.tanh(w_lo), wup_ref[...])
    log_w = -math.exp(-0.5) * _sigmoid(d)
    a = _sigmoid(a0_ref[...] + _dot(a_lo, aup_ref[...]))
    gate = _dot(_sigmoid(g_lo), gup_ref[...])
    kappa = k * kk_ref[...]
    k2 = k * (1.0 + (a - 1.0) * ka_ref[...])

    cum = _dot_hi(_tril(L).astype(F32), log_w)
    g_incl = jnp.exp(cum)
    g_excl = jnp.exp(cum - log_w)
    g_inv = jnp.exp(-cum)
    g_last = g_incl[L - 1:L, :]

    lower = _tril(L)
    strict = _tril(L, strict=True)

    steps = int(math.log2(L))
    heads = range(RW_HEADS)
    sls = [slice(h * RW_HEAD, (h + 1) * RW_HEAD) for h in heads]
    ps = RW_PASSES

    lhs, b_rows, k_rows = [], [], []
    for sl in sls:
        kap = kappa[:, sl]
        kh = kap / jnp.maximum(jnp.sqrt(jnp.sum(kap * kap, axis=-1, keepdims=True)), 1e-12)
        lhs.append(jnp.concatenate([r[:, sl] * g_incl[:, sl], -kh * g_excl[:, sl]], axis=0))
        b_rows.append(a[:, sl] * kh * g_inv[:, sl])
        k_rows.append(k2[:, sl] * g_inv[:, sl])
    s0 = [s_ref[h] for h in heads]
    gram = [_mm(lhs[h], jnp.concatenate([b_rows[h], k_rows[h]], axis=0), _NT, ps["gram"]) for h in heads]
    ls0 = [_mm(lhs[h], s0[h], _NT, ps["carry"]) for h in heads]
    u = [ls0[h][L:] + _mm(jnp.where(strict, gram[h][L:, L:], 0.0), v[:, sls[h]], _NN, ps["gram"]) for h in heads]
    n_pow = [jnp.where(strict, gram[h][L:, :L], 0.0) for h in heads]
    for i in range(steps):
        u = [u[h] + _mm(n_pow[h], u[h], _NN, ps["solve"]) for h in heads]
        if i + 1 < steps:
            n_pow = [_mm(n_pow[h], n_pow[h], _NN, ps["solve"]) for h in heads]

    for h, sl in enumerate(sls):
        v_h = v[:, sl]
        y = (ls0[h][:L] + _mm(jnp.where(lower, gram[h][:L, :L], 0.0), u[h], _NN, ps["out"])
             + _mm(jnp.where(lower, gram[h][:L, L:], 0.0), v_h, _NN, ps["out"]))
        s_ref[h] = (s0[h] + _mm(u[h], b_rows[h], _TN, ps["state"])
                    + _mm(v_h, k_rows[h], _TN, ps["state"])) * g_last[:, sl]

        mean = jnp.mean(y, axis=-1, keepdims=True)
        var = jnp.mean((y - mean) ** 2, axis=-1, keepdims=True)
        y = (y - mean) * lax.rsqrt(var + RW_GN_EPS) * gng_ref[:, sl] + gnb_ref[:, sl]
        y = y + jnp.sum(r[:, sl] * k2[:, sl] * rk_ref[:, sl], axis=-1, keepdims=True) * v_h
        o_ref[0, :, sl] = y * gate[:, sl]


def _rwkv(p, mu, w0, w_up, a0, a_up, g_up, k_k, k_a, r_k, gn_gain, gn_bias):
    b, t, c = p.shape
    row = lambda x: x.reshape(1, -1)
    vec = lambda n: pl.BlockSpec((1, n), lambda i, j: (0, 0))
    mat = lambda shp: pl.BlockSpec(shp, lambda i, j: (0, 0))
    return pl.pallas_call(
        _rwkv_kernel,
        out_shape=jax.ShapeDtypeStruct((b, t, BRANCH_W), F32),
        grid=(b, t // RW_CHUNK),
        in_specs=[
            pl.BlockSpec((1, RW_CHUNK, c), lambda i, j: (i, j, 0)),
            vec(c), vec(BRANCH_W), mat(w_up.shape), vec(BRANCH_W), mat(a_up.shape), mat(g_up.shape),
            vec(BRANCH_W), vec(BRANCH_W), vec(BRANCH_W), vec(BRANCH_W), vec(BRANCH_W),
        ],
        out_specs=pl.BlockSpec((1, RW_CHUNK, BRANCH_W), lambda i, j: (i, j, 0)),
        scratch_shapes=[pltpu.VMEM((1, c), F32), pltpu.VMEM((RW_HEADS, RW_HEAD, RW_HEAD), F32)],
        compiler_params=_params("parallel", "arbitrary"),
        name="rwkv7",
    )(p, row(mu), row(w0), w_up, row(a0), a_up, g_up, row(k_k), row(k_a), row(r_k), row(gn_gain),
      row(gn_bias))


def _log_sigmoid(x):
    return jnp.minimum(x, 0.0) - jnp.log1p(jnp.exp(-jnp.abs(x)))


def _mlstm_kernel(p_ref, cw_ref, cb_ref, gb_ref, ng_ref, o_ref, tail_ref, c_ref, n_ref, m_ref):
    L = ML_CHUNK
    qk_w = 2 * BRANCH_W

    @pl.when(pl.program_id(1) == 0)
    def _():
        tail_ref[...] = jnp.zeros_like(tail_ref)
        c_ref[...] = jnp.zeros_like(c_ref)
        n_ref[...] = jnp.zeros_like(n_ref)
        m_ref[...] = jnp.zeros_like(m_ref)

    x = p_ref[0]
    qk_in = x[:, :qk_w]
    ext = jnp.concatenate([tail_ref[...], qk_in], axis=0)
    tail_ref[...] = qk_in[L - 8:L, :]
    conv = cb_ref[...] + cw_ref[ML_CONV - 1:ML_CONV, :] * qk_in
    for j in range(ML_CONV - 1):
        conv = conv + cw_ref[j:j + 1, :] * pltpu.roll(ext, ML_CONV - 1 - j, 0)[8:8 + L]
    qk = conv * _sigmoid(conv)
    q = qk[:, :BRANCH_W]
    k = qk[:, BRANCH_W:] * ML_HEAD ** -0.5
    v = x[:, qk_w:qk_w + BRANCH_W]
    o_gate = x[:, qk_w + BRANCH_W:qk_w + 2 * BRANCH_W]

    gates = x[:, qk_w + 2 * BRANCH_W:] + gb_ref[...]
    log_f = _log_sigmoid(gates)
    b_all = _dot_hi(_tril(L).astype(F32), log_f)
    b_t = b_all.T
    g_t = gates.T
    lower = _tril(L)

    for h in range(ML_HEADS):
        sl = slice(h * ML_HEAD, (h + 1) * ML_HEAD)
        q_h, k_h, v_h = q[:, sl], k[:, sl], v[:, sl]
        b_col = b_all[:, ML_HEADS + h:ML_HEADS + h + 1]
        b_row = b_t[ML_HEADS + h:ML_HEADS + h + 1, :]
        i_col = gates[:, h:h + 1]
        i_row = g_t[h:h + 1, :]
        m_st = m_ref[h:h + 1, 0:1]
        n_st = n_ref[h:h + 1, :]
        c_st = c_ref[h]
        g_tot = b_col[L - 1:L, :]

        log_d = jnp.where(lower, b_col - b_row + i_row, -jnp.inf)
        m_j = jnp.maximum(b_col + m_st, jnp.max(log_d, axis=-1, keepdims=True))
        s = _dot_nt(q_h, k_h) * jnp.exp(log_d - m_j)
        inter = jnp.exp(b_col + m_st - m_j)
        num = _dot(s, v_h) + inter * _dot_nt(q_h, c_st)
        den = jnp.sum(s, axis=-1, keepdims=True) + inter * jnp.sum(q_h * n_st, axis=-1, keepdims=True)
        hh = num / jnp.maximum(jnp.abs(den), jnp.exp(-m_j))

        w_log = g_tot - b_col + i_col
        m_new = jnp.maximum(g_tot + m_st, jnp.max(w_log, axis=0, keepdims=True))
        wgt = jnp.exp(w_log - m_new)
        dec = jnp.exp(g_tot + m_st - m_new)
        c_ref[h] = dec * c_st + _dot_tn(wgt * v_h, k_h)
        n_ref[h:h + 1, :] = dec * n_st + jnp.sum(wgt * k_h, axis=0, keepdims=True)
        m_ref[h:h + 1, :] = jnp.broadcast_to(m_new, (1, ML_HEAD))

        hn = hh * lax.rsqrt(jnp.mean(hh * hh, axis=-1, keepdims=True) + NORM_EPS) * ng_ref[:, sl]
        o_ref[0, :, sl] = _sigmoid(o_gate[:, sl]) * hn


def _mlstm(p, conv_w, conv_b, gate_bias, norm_gain):
    b, t, c = p.shape
    return pl.pallas_call(
        _mlstm_kernel,
        out_shape=jax.ShapeDtypeStruct((b, t, BRANCH_W), F32),
        grid=(b, t // ML_CHUNK),
        in_specs=[
            pl.BlockSpec((1, ML_CHUNK, c), lambda i, j: (i, j, 0)),
            pl.BlockSpec(conv_w.shape, lambda i, j: (0, 0)),
            pl.BlockSpec((1, 2 * BRANCH_W), lambda i, j: (0, 0)),
            pl.BlockSpec((1, 128), lambda i, j: (0, 0)),
            pl.BlockSpec((1, BRANCH_W), lambda i, j: (0, 0)),
        ],
        out_specs=pl.BlockSpec((1, ML_CHUNK, BRANCH_W), lambda i, j: (i, j, 0)),
        scratch_shapes=[
            pltpu.VMEM((8, 2 * BRANCH_W), F32),
            pltpu.VMEM((ML_HEADS, ML_HEAD, ML_HEAD), F32),
            pltpu.VMEM((8, ML_HEAD), F32),
            pltpu.VMEM((8, ML_HEAD), F32),
        ],
        compiler_params=_params("parallel", "arbitrary"),
        name="mlstm",
    )(p, conv_w, conv_b.reshape(1, -1), gate_bias, norm_gain.reshape(1, -1))


def _dsa_kernel(pq_ref, pk_ref, kvn_ref, wuk_ref, wuv_ref, o_ref, ckv_ref, kidx_ref, sc_ref,
                mask_ref, *, top_k):
    qb = pl.program_id(1)
    tq = DSA_QBLOCK
    tk = ckv_ref.shape[0]

    @pl.when(qb == 0)
    def _():
        keys = pk_ref[0]
        ckv_ref[...] = _rms(keys[:, :DSA_LATENT], kvn_ref[...]).astype(BF16)
        kidx_ref[...] = keys[:, DSA_LATENT:DSA_LATENT + IDX_HEAD].astype(BF16)

    pq = pq_ref[0]
    q = pq[:, :BRANCH_W]
    q_idx = pq[:, BRANCH_W:2 * BRANCH_W]
    w_off = 2 * BRANCH_W + DSA_LATENT + IDX_HEAD
    w_idx = pq[:, w_off:w_off + IDX_HEADS] * (IDX_HEADS * IDX_HEAD) ** -0.5

    qi_all = jnp.concatenate([q_idx[:, h * IDX_HEAD:(h + 1) * IDX_HEAD] for h in range(IDX_HEADS)], axis=0)
    w_all = jnp.concatenate([w_idx[:, h:h + 1] for h in range(IDX_HEADS)], axis=0)
    s_all = jnp.maximum(_dot_nt(qi_all, kidx_ref[...]), 0.0) * w_all
    score = jnp.sum(s_all.reshape(IDX_HEADS, tq, tk), axis=0)

    q_pos = qb * tq + lax.broadcasted_iota(jnp.int32, (tq, 1), 0)
    q_chunk = jnp.where(q_pos < N_META, 0,
                        1 + lax.shift_right_arithmetic(q_pos - N_META, int(math.log2(STREAM_CHUNK))))
    n_allowed = N_META + STREAM_CHUNK * q_chunk
    allowed = lax.broadcasted_iota(jnp.int32, (tq, tk), 1) < n_allowed
    sc_ref[...] = jnp.where(allowed, score, -jnp.inf)

    def count(cond):
        return jnp.sum(jnp.where(cond, 1.0, 0.0), axis=-1, keepdims=True)

    def bisect(_, bracket):
        lo, hi = bracket
        mid = 0.5 * lo + 0.5 * hi
        ge = count(sc_ref[...] >= mid) >= top_k
        return jnp.where(ge, mid, lo), jnp.where(ge, hi, mid)

    def snap(lo):
        s = sc_ref[...]
        val = jnp.min(jnp.where(s >= lo, s, jnp.inf), axis=-1, keepdims=True)
        return val, count(s > val)

    row_max = jnp.max(sc_ref[...], axis=-1, keepdims=True)
    lo = jnp.min(jnp.where(allowed, score, jnp.inf), axis=-1, keepdims=True)
    hi = row_max + jnp.maximum(jnp.abs(row_max) * 2.0 ** -20, 1e-30)
    lo, hi = lax.fori_loop(0, DSA_BISECT_STEPS, bisect, (lo, hi))
    thr, n_above = snap(lo)

    def unsettled(state):
        return jnp.logical_and(jnp.max(state[3]) >= top_k, state[4] < DSA_BISECT_MAX_ROUNDS)

    def refine(state):
        lo, hi = lax.fori_loop(0, DSA_BISECT_REFINE, bisect, (state[0], state[1]))
        thr, n_above = snap(lo)
        return lo, hi, thr, n_above, state[4] + 1

    _, _, thr, n_above, _ = lax.while_loop(unsettled, refine, (lo, hi, thr, n_above, jnp.int32(0)))

    keep = sc_ref[...] >= thr
    mask_ref[...] = jnp.where(keep, 0.0, -jnp.inf)

    @pl.when(jnp.max(count(keep)) > top_k)
    def _():
        s = sc_ref[...]
        above = s > thr
        need = top_k - n_above
        tie = jnp.where(s == thr, 1.0, 0.0)
        before = (lax.broadcasted_iota(jnp.int32, (128, 128), 0)
                  < lax.broadcasted_iota(jnp.int32, (128, 128), 1)).astype(BF16)
        carry = jnp.zeros((tq, 1), F32)
        for j in range(tk // 128):
            cs = slice(j * 128, (j + 1) * 128)
            t_j = tie[:, cs]
            rank = carry + jnp.dot(t_j.astype(BF16), before, preferred_element_type=F32)
            sel = above[:, cs] | ((t_j > 0.0) & (rank < need))
            mask_ref[:, cs] = jnp.where(sel, 0.0, -jnp.inf)
            carry = carry + jnp.sum(t_j, axis=-1, keepdims=True)

    ckv = ckv_ref[...]
    q_all = jnp.concatenate([_dot(q[:, h * DSA_HEAD:(h + 1) * DSA_HEAD], wuk_ref[h])
                             for h in range(DSA_HEADS)], axis=0) * DSA_HEAD ** -0.5
    logits = _dot_nt(q_all, ckv).reshape(DSA_HEADS, tq, tk) + mask_ref[...][None]
    e = jnp.exp(logits - jnp.max(logits, axis=-1, keepdims=True))
    denom = jnp.sum(e, axis=-1, keepdims=True).reshape(DSA_HEADS * tq, 1)
    out = _dot(e.reshape(DSA_HEADS * tq, tk), ckv) / denom
    for h in range(DSA_HEADS):
        o_ref[0, :, h * DSA_HEAD:(h + 1) * DSA_HEAD] = _dot(out[h * tq:(h + 1) * tq], wuv_ref[h])


def _dsa(p, kv_norm, w_uk, w_uv, top_k):
    b, t, c = p.shape
    return pl.pallas_call(
        functools.partial(_dsa_kernel, top_k=top_k),
        out_shape=jax.ShapeDtypeStruct((b, t, BRANCH_W), F32),
        grid=(b, t // DSA_QBLOCK),
        in_specs=[
            pl.BlockSpec((1, DSA_QBLOCK, c), lambda i, j: (i, j, 0)),
            pl.BlockSpec((1, t, DSA_KEY_COL_BLOCK), lambda i, j: (i, 0, c // DSA_KEY_COL_BLOCK - 1)),
            pl.BlockSpec((1, DSA_LATENT), lambda i, j: (0, 0)),
            pl.BlockSpec(w_uk.shape, lambda i, j: (0, 0, 0)),
            pl.BlockSpec(w_uv.shape, lambda i, j: (0, 0, 0)),
        ],
        out_specs=pl.BlockSpec((1, DSA_QBLOCK, BRANCH_W), lambda i, j: (i, j, 0)),
        scratch_shapes=[
            pltpu.VMEM((t, DSA_LATENT), BF16),
            pltpu.VMEM((t, IDX_HEAD), BF16),
            pltpu.VMEM((DSA_QBLOCK, t), F32),
            pltpu.VMEM((DSA_QBLOCK, t), F32),
        ],
        compiler_params=_params("parallel", "arbitrary"),
        name="dsa",
    )(p, p, kv_norm.reshape(1, -1), w_uk, w_uv)


def _merge_kernel(h_ref, yr_ref, ym_ref, yd_ref, pg_ref, wb_ref, wo_ref, g_ref, o_ref):
    mixed = None
    for i, y_ref in enumerate((yr_ref, ym_ref, yd_ref)):
        proj = jnp.dot(y_ref[...].astype(BF16), wb_ref[i], preferred_element_type=F32)
        term = _sigmoid(pg_ref[:, i * D_MODEL:(i + 1) * D_MODEL]) * proj
        mixed = term if mixed is None else mixed + term
    out = jnp.dot(mixed.astype(BF16), wo_ref[...], preferred_element_type=F32)
    o_ref[...] = h_ref[...] + _rms(out, g_ref[...])


def _merge(h, y_rw, y_ml, y_dsa, p_gate, w_branch, w_out, gain):
    m, d = h.shape
    rows = lambda n: pl.BlockSpec((ROW_TILE, n), lambda i: (i, 0))
    return pl.pallas_call(
        _merge_kernel,
        out_shape=jax.ShapeDtypeStruct((m, d), F32),
        grid=(m // ROW_TILE,),
        in_specs=[
            rows(d), rows(BRANCH_W), rows(BRANCH_W), rows(BRANCH_W), rows(3 * d),
            pl.BlockSpec(w_branch.shape, lambda i: (0, 0, 0)),
            pl.BlockSpec(w_out.shape, lambda i: (0, 0)),
            pl.BlockSpec((1, d), lambda i: (0, 0)),
        ],
        out_specs=rows(d),
        compiler_params=_params("parallel"),
        name="merge",
    )(h, y_rw, y_ml, y_dsa, p_gate, w_branch, w_out, gain.reshape(1, d))


def _split_w_in(w):
    rw_end = RW_COLS
    ml_end = rw_end + 4 * BRANCH_W + 2 * ML_HEADS
    dsa_end = ml_end + BRANCH_W + DSA_LATENT + IDX_HEADS * IDX_HEAD + IDX_HEAD + IDX_HEADS
    w_rw, w_ml, w_dsa, w_gate = w[:, :rw_end], w[:, rw_end:ml_end], w[:, ml_end:dsa_end], w[:, dsa_end:]
    w_ml = jnp.pad(w_ml, ((0, 0), (0, ML_COLS_PAD - w_ml.shape[1])))
    q, c_kv, rest = w_dsa[:, :BRANCH_W], w_dsa[:, BRANCH_W:BRANCH_W + DSA_LATENT], w_dsa[:, BRANCH_W + DSA_LATENT:]
    q_idx, tail = rest[:, :IDX_HEADS * IDX_HEAD], rest[:, IDX_HEADS * IDX_HEAD:]
    w_dsa = jnp.concatenate([q, q_idx, c_kv, tail], axis=1)
    w_dsa = jnp.pad(w_dsa, ((0, 0), (0, DSA_COLS_PAD - w_dsa.shape[1])))
    return tuple(x.astype(BF16) for x in (w_rw, w_ml, w_dsa, w_gate))


def kernel(x, meta_tokens, norm_gain, ffn_w_in, ffn_w_out, w_in, rw_mu, rw_w0, rw_w_up, rw_a0, rw_a_up, rw_g_up, rw_k_k, rw_k_a, rw_r_k, rw_gn_gain, rw_gn_bias, ml_conv_w, ml_conv_b, ml_i_bias, ml_f_bias, ml_norm_gain, dsa_kv_norm, dsa_w_uk, dsa_w_uv, w_branch, w_out):
    bsz, seq, d = x.shape
    depth = norm_gain.shape[0]
    top_k = min(TOPK_MAX, seq // 4)
    t_len = seq + N_META
    t_pad = -(-t_len // SEQ_PAD_MULTIPLE) * SEQ_PAD_MULTIPLE
    h = jnp.concatenate([
        jnp.broadcast_to(meta_tokens.astype(x.dtype)[None], (bsz, N_META, d)), x,
        jnp.zeros((bsz, t_pad - t_len, d), x.dtype)], axis=1).reshape(bsz * t_pad, d)

    for l in range(depth):
        g = norm_gain[l]
        h = _ffn(h, g[0], g[1], ffn_w_in[l, 0].astype(BF16), ffn_w_out[l, 0].astype(BF16))

        w_rw, w_ml, w_dsa, w_gate = _split_w_in(w_in[l])
        p_rw = _proj(h, g[2], w_rw).reshape(bsz, t_pad, -1)
        p_ml = _proj(h, g[2], w_ml).reshape(bsz, t_pad, -1)
        p_dsa = _proj(h, g[2], w_dsa).reshape(bsz, t_pad, -1)
        p_gate = _proj(h, g[2], w_gate)

        y_rw = _rwkv(p_rw, rw_mu[l], rw_w0[l], rw_w_up[l], rw_a0[l], rw_a_up[l], rw_g_up[l], rw_k_k[l],
                     rw_k_a[l], rw_r_k[l], rw_gn_gain[l], rw_gn_bias[l])
        gate_bias = jnp.pad(jnp.concatenate([ml_i_bias[l], ml_f_bias[l]]), (0, 128 - 2 * ML_HEADS))
        y_ml = _mlstm(p_ml, ml_conv_w[l], ml_conv_b[l], gate_bias.reshape(1, 128), ml_norm_gain[l])
        y_dsa = _dsa(p_dsa, dsa_kv_norm[l], dsa_w_uk[l].astype(BF16), dsa_w_uv[l].astype(BF16), top_k)

        flat = lambda y: y.reshape(bsz * t_pad, BRANCH_W)
        h = _merge(h, flat(y_rw), flat(y_ml), flat(y_dsa), p_gate, w_branch[l].astype(BF16),
                   w_out[l].astype(BF16), g[3])
        h = _ffn(h, g[4], g[5], ffn_w_in[l, 1].astype(BF16), ffn_w_out[l, 1].astype(BF16))

    return h.reshape(bsz, t_pad, d)[:, N_META:t_len]
```

```python
import functools
import math

import jax
import jax.numpy as jnp
from jax import lax
from jax.experimental import pallas as pl
from jax.experimental.pallas import tpu as pltpu

F32 = jnp.float32
BF16 = jnp.bfloat16
HI = lax.Precision.HIGHEST

D_MODEL = 1024
D_FF = 2816
N_META = 16
STREAM_CHUNK = 64
BRANCH_W = 512
NORM_EPS = 1e-6

RW_HEADS, RW_HEAD = 8, 64
RW_COLS = 1792
RW_GN_EPS = 64e-5
RW_CHUNK = 64

ML_HEADS, ML_HEAD = 4, 128
ML_CHUNK = 64
ML_CONV = 4
ML_COLS_PAD = 2176

DSA_HEADS, DSA_HEAD, DSA_LATENT = 8, 64, 128
IDX_HEADS, IDX_HEAD = 8, 64
TOPK_MAX = 256
DSA_QBLOCK = 128
DSA_COLS_PAD = 1280
DSA_KEY_COL_BLOCK = 256

ROW_TILE = 512
FF_TILE = 1408
SEQ_PAD_MULTIPLE = 128
VMEM_LIMIT = 56 * 1024 * 1024
DSA_KEY_CHUNK = 512
DSA_SUM_ROWS = 16
DSA_BISECT_STEPS = 24
DSA_BISECT_REFINE = 8
DSA_BISECT_MAX_ROUNDS = 40


def _dot(a, b):
    return jnp.dot(a.astype(BF16), b.astype(BF16), preferred_element_type=F32)


def _dot_nt(a, b):
    return lax.dot_general(a.astype(BF16), b.astype(BF16), (((1,), (1,)), ((), ())),
                           preferred_element_type=F32)


def _dot_tn(a, b):
    return lax.dot_general(a.astype(BF16), b.astype(BF16), (((0,), (0,)), ((), ())),
                           preferred_element_type=F32)


def _dot_hi(a, b):
    return jnp.dot(a, b, preferred_element_type=F32, precision=HI)


def _dot_nt_hi(a, b):
    return lax.dot_general(a, b, (((1,), (1,)), ((), ())), preferred_element_type=F32, precision=HI)


def _dot_tn_hi(a, b):
    return lax.dot_general(a, b, (((0,), (0,)), ((), ())), preferred_element_type=F32, precision=HI)


_NN = ((1,), (0,))
_NT = ((1,), (1,))
_TN = ((0,), (0,))


def _mm(a, b, contract, passes):
    dn = (contract, ((), ()))
    if passes == 6:
        return lax.dot_general(a, b, dn, preferred_element_type=F32, precision=HI)
    dot = lambda x, y: lax.dot_general(x, y, dn, preferred_element_type=F32)
    a_hi, b_hi = a.astype(BF16), b.astype(BF16)
    if passes == 1:
        return dot(a_hi, b_hi)
    a_lo = (a - a_hi.astype(F32)).astype(BF16)
    b_lo = (b - b_hi.astype(F32)).astype(BF16)
    return dot(a_hi, b_hi) + (dot(a_hi, b_lo) + dot(a_lo, b_hi))


RW_PASSES = {"gram": 1, "carry": 1, "solve": 1, "out": 1, "state": 1}


def _rms(x, gain):
    return x * lax.rsqrt(jnp.mean(x * x, axis=-1, keepdims=True) + NORM_EPS) * gain


def _sigmoid(x):
    return jax.nn.sigmoid(x)


def _tril(n, strict=False):
    r = lax.broadcasted_iota(jnp.int32, (n, n), 0)
    c = lax.broadcasted_iota(jnp.int32, (n, n), 1)
    return (c < r) if strict else (c <= r)


def _params(*sem):
    return pltpu.CompilerParams(dimension_semantics=sem, vmem_limit_bytes=VMEM_LIMIT)


def _ffn_kernel(h_ref, gpre_ref, gpost_ref, wg_ref, wu_ref, wo_ref, o_ref, xn_ref, acc_ref):
    j = pl.program_id(1)

    @pl.when(j == 0)
    def _():
        xn_ref[...] = _rms(h_ref[...], gpre_ref[...]).astype(BF16)
        acc_ref[...] = jnp.zeros_like(acc_ref)

    xn = xn_ref[...]
    gate = jnp.dot(xn, wg_ref[...], preferred_element_type=F32)
    up = jnp.dot(xn, wu_ref[...], preferred_element_type=F32)
    act = gate * _sigmoid(gate) * up
    acc_ref[...] += jnp.dot(act.astype(BF16), wo_ref[...], preferred_element_type=F32)

    @pl.when(j == pl.num_programs(1) - 1)
    def _():
        o_ref[...] = h_ref[...] + 0.5 * _rms(acc_ref[...], gpost_ref[...])


def _ffn(h, g_pre, g_post, w_in, w_out):
    m, d = h.shape
    nj = D_FF // FF_TILE
    return pl.pallas_call(
        _ffn_kernel,
        out_shape=jax.ShapeDtypeStruct((m, d), F32),
        grid=(m // ROW_TILE, nj),
        in_specs=[
            pl.BlockSpec((ROW_TILE, d), lambda i, j: (i, 0)),
            pl.BlockSpec((1, d), lambda i, j: (0, 0)),
            pl.BlockSpec((1, d), lambda i, j: (0, 0)),
            pl.BlockSpec((d, FF_TILE), lambda i, j: (0, j)),
            pl.BlockSpec((d, FF_TILE), lambda i, j: (0, j + nj)),
            pl.BlockSpec((FF_TILE, d), lambda i, j: (j, 0)),
        ],
        out_specs=pl.BlockSpec((ROW_TILE, d), lambda i, j: (i, 0)),
        scratch_shapes=[pltpu.VMEM((ROW_TILE, d), BF16), pltpu.VMEM((ROW_TILE, d), F32)],
        compiler_params=_params("parallel", "arbitrary"),
        name="ffn",
    )(h, g_pre.reshape(1, d), g_post.reshape(1, d), w_in, w_in, w_out)


def _proj_kernel(h_ref, g_ref, w_ref, o_ref):
    o_ref[...] = jnp.dot(_rms(h_ref[...], g_ref[...]).astype(BF16), w_ref[...],
                         preferred_element_type=F32)


def _proj(h, gain, w):
    m, d = h.shape
    n = w.shape[1]
    return pl.pallas_call(
        _proj_kernel,
        out_shape=jax.ShapeDtypeStruct((m, n), F32),
        grid=(m // ROW_TILE,),
        in_specs=[
            pl.BlockSpec((ROW_TILE, d), lambda i: (i, 0)),
            pl.BlockSpec((1, d), lambda i: (0, 0)),
            pl.BlockSpec((d, n), lambda i: (0, 0)),
        ],
        out_specs=pl.BlockSpec((ROW_TILE, n), lambda i: (i, 0)),
        compiler_params=_params("parallel"),
        name="in_proj",
    )(h, gain.reshape(1, d), w)


def _rwkv_kernel(p_ref, mu_ref, w0_ref, wup_ref, a0_ref, aup_ref, gup_ref, kk_ref, ka_ref, rk_ref,
                 gng_ref, gnb_ref, o_ref, prev_ref, s_ref):
    L = RW_CHUNK

    @pl.when(pl.program_id(1) == 0)
    def _():
        prev_ref[...] = jnp.zeros_like(prev_ref)
        s_ref[...] = jnp.zeros_like(s_ref)

    p = p_ref[0]
    row = lax.broadcasted_iota(jnp.int32, (L, 1), 0)
    prev = jnp.where(row == 0, prev_ref[...], pltpu.roll(p, 1, 0))
    prev_ref[...] = p[L - 1:L, :]
    ps = p + (prev - p) * mu_ref[...]

    r = ps[:, 0:BRANCH_W]
    k = ps[:, BRANCH_W:2 * BRANCH_W]
    v = ps[:, 2 * BRANCH_W:3 * BRANCH_W]
    w_lo = ps[:, 1536:1600]
    a_lo = ps[:, 1600:1664]
    g_lo = ps[:, 1664:1792]

    d = w0_ref[...] + _dot(jnp.tanh(w_lo), wup_ref[...])
    log_w = -math.exp(-0.5) * _sigmoid(d)
    a = _sigmoid(a0_ref[...] + _dot(a_lo, aup_ref[...]))
    gate = _dot(_sigmoid(g_lo), gup_ref[...])
    kappa = k * kk_ref[...]
    k2 = k * (1.0 + (a - 1.0) * ka_ref[...])

    cum = _dot_hi(_tril(L).astype(F32), log_w)
    g_incl = jnp.exp(cum)
    g_excl = jnp.exp(cum - log_w)
    g_inv = jnp.exp(-cum)
    g_last = g_incl[L - 1:L, :]

    lower = _tril(L)
    strict = _tril(L, strict=True)

    steps = int(math.log2(L))
    heads = range(RW_HEADS)
    sls = [slice(h * RW_HEAD, (h + 1) * RW_HEAD) for h in heads]
    ps = RW_PASSES

    lhs, b_rows, k_rows = [], [], []
    for sl in sls:
        kap = kappa[:, sl]
        kh = kap / jnp.maximum(jnp.sqrt(jnp.sum(kap * kap, axis=-1, keepdims=True)), 1e-12)
        lhs.append(jnp.concatenate([r[:, sl] * g_incl[:, sl], -kh * g_excl[:, sl]], axis=0))
        b_rows.append(a[:, sl] * kh * g_inv[:, sl])
        k_rows.append(k2[:, sl] * g_inv[:, sl])
    s0 = [s_ref[h] for h in heads]
    gram = [_mm(lhs[h], jnp.concatenate([b_rows[h], k_rows[h]], axis=0), _NT, ps["gram"]) for h in heads]
    ls0 = [_mm(lhs[h], s0[h], _NT, ps["carry"]) for h in heads]
    u = [ls0[h][L:] + _mm(jnp.where(strict, gram[h][L:, L:], 0.0), v[:, sls[h]], _NN, ps["gram"]) for h in heads]
    n_pow = [jnp.where(strict, gram[h][L:, :L], 0.0) for h in heads]
    for i in range(steps):
        u = [u[h] + _mm(n_pow[h], u[h], _NN, ps["solve"]) for h in heads]
        if i + 1 < steps:
            n_pow = [_mm(n_pow[h], n_pow[h], _NN, ps["solve"]) for h in heads]

    for h, sl in enumerate(sls):
        v_h = v[:, sl]
        y = (ls0[h][:L] + _mm(jnp.where(lower, gram[h][:L, :L], 0.0), u[h], _NN, ps["out"])
             + _mm(jnp.where(lower, gram[h][:L, L:], 0.0), v_h, _NN, ps["out"]))
        s_ref[h] = (s0[h] + _mm(u[h], b_rows[h], _TN, ps["state"])
                    + _mm(v_h, k_rows[h], _TN, ps["state"])) * g_last[:, sl]

        mean = jnp.mean(y, axis=-1, keepdims=True)
        var = jnp.mean((y - mean) ** 2, axis=-1, keepdims=True)
        y = (y - mean) * lax.rsqrt(var + RW_GN_EPS) * gng_ref[:, sl] + gnb_ref[:, sl]
        y = y + jnp.sum(r[:, sl] * k2[:, sl] * rk_ref[:, sl], axis=-1, keepdims=True) * v_h
        o_ref[0, :, sl] = y * gate[:, sl]


def _rwkv(p, mu, w0, w_up, a0, a_up, g_up, k_k, k_a, r_k, gn_gain, gn_bias):
    b, t, c = p.shape
    row = lambda x: x.reshape(1, -1)
    vec = lambda n: pl.BlockSpec((1, n), lambda i, j: (0, 0))
    mat = lambda shp: pl.BlockSpec(shp, lambda i, j: (0, 0))
    return pl.pallas_call(
        _rwkv_kernel,
        out_shape=jax.ShapeDtypeStruct((b, t, BRANCH_W), F32),
        grid=(b, t // RW_CHUNK),
        in_specs=[
            pl.BlockSpec((1, RW_CHUNK, c), lambda i, j: (i, j, 0)),
            vec(c), vec(BRANCH_W), mat(w_up.shape), vec(BRANCH_W), mat(a_up.shape), mat(g_up.shape),
            vec(BRANCH_W), vec(BRANCH_W), vec(BRANCH_W), vec(BRANCH_W), vec(BRANCH_W),
        ],
        out_specs=pl.BlockSpec((1, RW_CHUNK, BRANCH_W), lambda i, j: (i, j, 0)),
        scratch_shapes=[pltpu.VMEM((1, c), F32), pltpu.VMEM((RW_HEADS, RW_HEAD, RW_HEAD), F32)],
        compiler_params=_params("parallel", "arbitrary"),
        name="rwkv7",
    )(p, row(mu), row(w0), w_up, row(a0), a_up, g_up, row(k_k), row(k_a), row(r_k), row(gn_gain),
      row(gn_bias))


def _log_sigmoid(x):
    return jnp.minimum(x, 0.0) - jnp.log1p(jnp.exp(-jnp.abs(x)))


def _mlstm_kernel(p_ref, cw_ref, cb_ref, gb_ref, ng_ref, o_ref, tail_ref, c_ref, n_ref, m_ref):
    L = ML_CHUNK
    qk_w = 2 * BRANCH_W

    @pl.when(pl.program_id(1) == 0)
    def _():
        tail_ref[...] = jnp.zeros_like(tail_ref)
        c_ref[...] = jnp.zeros_like(c_ref)
        n_ref[...] = jnp.zeros_like(n_ref)
        m_ref[...] = jnp.zeros_like(m_ref)

    x = p_ref[0]
    qk_in = x[:, :qk_w]
    ext = jnp.concatenate([tail_ref[...], qk_in], axis=0)
    tail_ref[...] = qk_in[L - 8:L, :]
    conv = cb_ref[...] + cw_ref[ML_CONV - 1:ML_CONV, :] * qk_in
    for j in range(ML_CONV - 1):
        conv = conv + cw_ref[j:j + 1, :] * pltpu.roll(ext, ML_CONV - 1 - j, 0)[8:8 + L]
    qk = conv * _sigmoid(conv)
    q = qk[:, :BRANCH_W]
    k = qk[:, BRANCH_W:] * ML_HEAD ** -0.5
    v = x[:, qk_w:qk_w + BRANCH_W]
    o_gate = x[:, qk_w + BRANCH_W:qk_w + 2 * BRANCH_W]

    gates = x[:, qk_w + 2 * BRANCH_W:] + gb_ref[...]
    log_f = _log_sigmoid(gates)
    b_all = _dot_hi(_tril(L).astype(F32), log_f)
    b_t = b_all.T
    g_t = gates.T
    lower = _tril(L)

    for h in range(ML_HEADS):
        sl = slice(h * ML_HEAD, (h + 1) * ML_HEAD)
        q_h, k_h, v_h = q[:, sl], k[:, sl], v[:, sl]
        b_col = b_all[:, ML_HEADS + h:ML_HEADS + h + 1]
        b_row = b_t[ML_HEADS + h:ML_HEADS + h + 1, :]
        i_col = gates[:, h:h + 1]
        i_row = g_t[h:h + 1, :]
        m_st = m_ref[h:h + 1, 0:1]
        n_st = n_ref[h:h + 1, :]
        c_st = c_ref[h]
        g_tot = b_col[L - 1:L, :]

        log_d = jnp.where(lower, b_col - b_row + i_row, -jnp.inf)
        m_j = jnp.maximum(b_col + m_st, jnp.max(log_d, axis=-1, keepdims=True))
        s = _dot_nt(q_h, k_h) * jnp.exp(log_d - m_j)
        inter = jnp.exp(b_col + m_st - m_j)
        num = _dot(s, v_h) + inter * _dot_nt(q_h, c_st)
        den = jnp.sum(s, axis=-1, keepdims=True) + inter * jnp.sum(q_h * n_st, axis=-1, keepdims=True)
        hh = num / jnp.maximum(jnp.abs(den), jnp.exp(-m_j))

        w_log = g_tot - b_col + i_col
        m_new = jnp.maximum(g_tot + m_st, jnp.max(w_log, axis=0, keepdims=True))
        wgt = jnp.exp(w_log - m_new)
        dec = jnp.exp(g_tot + m_st - m_new)
        c_ref[h] = dec * c_st + _dot_tn(wgt * v_h, k_h)
        n_ref[h:h + 1, :] = dec * n_st + jnp.sum(wgt * k_h, axis=0, keepdims=True)
        m_ref[h:h + 1, :] = jnp.broadcast_to(m_new, (1, ML_HEAD))

        hn = hh * lax.rsqrt(jnp.mean(hh * hh, axis=-1, keepdims=True) + NORM_EPS) * ng_ref[:, sl]
        o_ref[0, :, sl] = _sigmoid(o_gate[:, sl]) * hn


def _mlstm(p, conv_w, conv_b, gate_bias, norm_gain):
    b, t, c = p.shape
    return pl.pallas_call(
        _mlstm_kernel,
        out_shape=jax.ShapeDtypeStruct((b, t, BRANCH_W), F32),
        grid=(b, t // ML_CHUNK),
        in_specs=[
            pl.BlockSpec((1, ML_CHUNK, c), lambda i, j: (i, j, 0)),
            pl.BlockSpec(conv_w.shape, lambda i, j: (0, 0)),
            pl.BlockSpec((1, 2 * BRANCH_W), lambda i, j: (0, 0)),
            pl.BlockSpec((1, 128), lambda i, j: (0, 0)),
            pl.BlockSpec((1, BRANCH_W), lambda i, j: (0, 0)),
        ],
        out_specs=pl.BlockSpec((1, ML_CHUNK, BRANCH_W), lambda i, j: (i, j, 0)),
        scratch_shapes=[
            pltpu.VMEM((8, 2 * BRANCH_W), F32),
            pltpu.VMEM((ML_HEADS, ML_HEAD, ML_HEAD), F32),
            pltpu.VMEM((8, ML_HEAD), F32),
            pltpu.VMEM((8, ML_HEAD), F32),
        ],
        compiler_params=_params("parallel", "arbitrary"),
        name="mlstm",
    )(p, conv_w, conv_b.reshape(1, -1), gate_bias, norm_gain.reshape(1, -1))


def _sum_sublane_groups(x):
    n = x.shape[0] // 32
    g = x.reshape(4 * n, 8, x.shape[1])
    parts = [g[i * n:(i + 1) * n] for i in range(4)]
    return (jnp.sum(parts[0], axis=0) + jnp.sum(parts[1], axis=0)) + (jnp.sum(parts[2], axis=0) + jnp.sum(parts[3], axis=0))


def _dsa_kernel(pq_ref, pk_ref, kvn_ref, wukt_ref, wuv_ref, o_ref, ckv_ref, ckvt_ref, kidx_ref, sc_ref,
                *, top_k):
    qb = pl.program_id(1)
    tq = DSA_QBLOCK
    kc = DSA_KEY_CHUNK
    tk = pk_ref.shape[1]
    n_chunks_max = ckvt_ref.shape[0]
    hq = DSA_HEADS * tq

    @pl.when(qb == 0)
    def _():
        keys = pk_ref[0]
        ckv = _rms(keys[:, :DSA_LATENT], kvn_ref[...])
        pad_rows = n_chunks_max * kc - tk
        ckv_ref[0:tk, :] = ckv.astype(BF16)
        ckv_ref[tk:, :] = jnp.zeros((pad_rows, DSA_LATENT), BF16)
        kidx_ref[0:tk, :] = keys[:, DSA_LATENT:DSA_LATENT + IDX_HEAD].astype(BF16)
        kidx_ref[tk:, :] = jnp.zeros((pad_rows, IDX_HEAD), BF16)
        ckv_t = jnp.concatenate([ckv.T, jnp.zeros((DSA_LATENT, pad_rows), F32)], axis=1)
        extra = jnp.where(lax.broadcasted_iota(jnp.int32, (DSA_SUM_ROWS, kc), 0) == 0, 1.0, 0.0)
        for c in range(n_chunks_max):
            ckvt_ref[c] = jnp.concatenate([ckv_t[:, c * kc:(c + 1) * kc], extra], axis=0).astype(BF16)

    n_chunks = jnp.minimum(lax.shift_right_logical(qb * tq + tq + N_META + kc - 1, int(math.log2(kc))),
                           n_chunks_max)
    q_pos = qb * tq + lax.broadcasted_iota(jnp.int32, (1, tq), 1)
    q_chunk = jnp.where(q_pos < N_META, 0,
                        1 + lax.shift_right_arithmetic(q_pos - N_META, int(math.log2(STREAM_CHUNK))))
    n_allowed = jnp.minimum(N_META + STREAM_CHUNK * q_chunk, tk)

    def rows_of(c):
        return pl.ds(pl.multiple_of(c * kc, kc), kc)

    pq = pq_ref[0]
    q = pq[:, :BRANCH_W]
    qi_t = jnp.concatenate([pq[:, BRANCH_W + j * 128:BRANCH_W + (j + 1) * 128].T
                            for j in range(IDX_HEADS * IDX_HEAD // 128)], axis=0)
    qi_t = jnp.concatenate([qi_t[h * IDX_HEAD:(h + 1) * IDX_HEAD] for h in range(IDX_HEADS)], axis=1)
    tail_t = pq[:, DSA_COLS_PAD - 128:].T
    w_t = tail_t[IDX_HEAD:IDX_HEAD + IDX_HEADS] * (IDX_HEADS * IDX_HEAD) ** -0.5
    w_row = jnp.concatenate([w_t[h:h + 1] for h in range(IDX_HEADS)], axis=1)
    qi_t = qi_t.astype(BF16)

    def score_chunk(c, bounds):
        lo, hi = bounds
        s = jnp.maximum(jnp.dot(kidx_ref[rows_of(c), :], qi_t, preferred_element_type=F32), 0.0) * w_row
        score = s[:, 0:tq]
        for h in range(1, IDX_HEADS):
            score = score + s[:, h * tq:(h + 1) * tq]
        k_pos = c * kc + lax.broadcasted_iota(jnp.int32, (kc, 1), 0)
        allowed = k_pos < n_allowed
        sc_ref[rows_of(c), :] = jnp.where(allowed, score, -jnp.inf)
        lo = jnp.minimum(lo, jnp.min(jnp.where(allowed, score, jnp.inf), axis=0, keepdims=True))
        hi = jnp.maximum(hi, jnp.max(jnp.where(allowed, score, -jnp.inf), axis=0, keepdims=True))
        return lo, hi

    lo, row_max = lax.fori_loop(0, n_chunks, score_chunk,
                                (jnp.full((1, tq), jnp.inf, F32), jnp.full((1, tq), -jnp.inf, F32)))
    hi = row_max + jnp.maximum(jnp.abs(row_max) * 2.0 ** -20, 1e-30)

    def count(pred, level):
        def body(c, acc):
            return acc + _sum_sublane_groups(jnp.where(pred(sc_ref[rows_of(c), :], level), 1.0, 0.0))
        return jnp.sum(lax.fori_loop(0, n_chunks, body, jnp.zeros((8, tq), F32)), axis=0, keepdims=True)

    def bisect(_, bracket):
        lo, hi = bracket
        mid = 0.5 * lo + 0.5 * hi
        ge = count(jnp.greater_equal, mid) >= top_k
        return jnp.where(ge, mid, lo), jnp.where(ge, hi, mid)

    def snap(lo):
        def body(c, val):
            s = sc_ref[rows_of(c), :]
            return jnp.minimum(val, jnp.min(jnp.where(s >= lo, s, jnp.inf), axis=0, keepdims=True))
        val = lax.fori_loop(0, n_chunks, body, jnp.full((1, tq), jnp.inf, F32))
        return val, count(jnp.greater, val)

    lo, hi = lax.fori_loop(0, DSA_BISECT_STEPS, bisect, (lo, hi))
    thr, n_above = snap(lo)

    def unsettled(state):
        return jnp.logical_and(jnp.max(state[3]) >= top_k, state[4] < DSA_BISECT_MAX_ROUNDS)

    def refine(state):
        lo, hi = lax.fori_loop(0, DSA_BISECT_REFINE, bisect, (state[0], state[1]))
        thr, n_above = snap(lo)
        return lo, hi, thr, n_above, state[4] + 1

    _, _, thr, n_above, _ = lax.while_loop(unsettled, refine, (lo, hi, thr, n_above, jnp.int32(0)))

    @pl.when(jnp.max(count(jnp.greater_equal, thr)) > top_k)
    def _():
        need = top_k - n_above
        earlier = (lax.broadcasted_iota(jnp.int32, (kc, kc), 1)
                   < lax.broadcasted_iota(jnp.int32, (kc, kc), 0)).astype(BF16)

        def body(c, seen):
            s = sc_ref[rows_of(c), :]
            tie = jnp.where(s == thr, 1.0, 0.0)
            rank = seen + jnp.dot(earlier, tie.astype(BF16), preferred_element_type=F32)
            sc_ref[rows_of(c), :] = jnp.where((tie > 0.0) & (rank >= need), -jnp.inf, s)
            return seen + jnp.sum(tie, axis=0, keepdims=True)

        lax.fori_loop(0, n_chunks, body, jnp.zeros((1, tq), F32))

    q_t = jnp.concatenate([_mm(wukt_ref[h], q[:, h * DSA_HEAD:(h + 1) * DSA_HEAD], _NT, 1)
                           for h in range(DSA_HEADS)], axis=1)
    q_t = (q_t * (DSA_HEAD ** -0.5 * math.log2(math.e))).astype(BF16)

    def attend(c, carry):
        m, acc = carry
        bias = jnp.where(sc_ref[rows_of(c), :] >= thr, 0.0, -jnp.inf)
        logits = (jnp.dot(ckv_ref[rows_of(c), :], q_t, preferred_element_type=F32)
                  + jnp.concatenate([bias] * DSA_HEADS, axis=1))
        m_new = jnp.maximum(m, jnp.max(logits, axis=0, keepdims=True))
        m_safe = jnp.where(m_new == -jnp.inf, 0.0, m_new)
        e = jnp.exp2(logits - m_safe)
        acc = acc * jnp.exp2(m - m_safe) + jnp.dot(ckvt_ref[c], e.astype(BF16), preferred_element_type=F32)
        return m_new, acc

    _, acc = lax.fori_loop(0, n_chunks, attend,
                           (jnp.full((1, hq), -jnp.inf, F32), jnp.zeros((DSA_LATENT + DSA_SUM_ROWS, hq), F32)))
    out_t = acc[:DSA_LATENT] / acc[DSA_LATENT:DSA_LATENT + 1]
    for h in range(DSA_HEADS):
        o_ref[0, :, h * DSA_HEAD:(h + 1) * DSA_HEAD] = _mm(out_t[:, h * tq:(h + 1) * tq], wuv_ref[h], _TN, 1)


def _dsa(p, kv_norm, w_uk_t, w_uv, top_k):
    b, t, c = p.shape
    n_chunks = -(-t // DSA_KEY_CHUNK)
    return pl.pallas_call(
        functools.partial(_dsa_kernel, top_k=top_k),
        out_shape=jax.ShapeDtypeStruct((b, t, BRANCH_W), F32),
        grid=(b, t // DSA_QBLOCK),
        in_specs=[
            pl.BlockSpec((1, DSA_QBLOCK, c), lambda i, j: (i, j, 0)),
            pl.BlockSpec((1, t, DSA_KEY_COL_BLOCK), lambda i, j: (i, 0, c // DSA_KEY_COL_BLOCK - 1)),
            pl.BlockSpec((1, DSA_LATENT), lambda i, j: (0, 0)),
            pl.BlockSpec(w_uk_t.shape, lambda i, j: (0, 0, 0)),
            pl.BlockSpec(w_uv.shape, lambda i, j: (0, 0, 0)),
        ],
        out_specs=pl.BlockSpec((1, DSA_QBLOCK, BRANCH_W), lambda i, j: (i, j, 0)),
        scratch_shapes=[
            pltpu.VMEM((n_chunks * DSA_KEY_CHUNK, DSA_LATENT), BF16),
            pltpu.VMEM((n_chunks, DSA_LATENT + DSA_SUM_ROWS, DSA_KEY_CHUNK), BF16),
            pltpu.VMEM((n_chunks * DSA_KEY_CHUNK, IDX_HEAD), BF16),
            pltpu.VMEM((n_chunks * DSA_KEY_CHUNK, DSA_QBLOCK), F32),
        ],
        compiler_params=_params("parallel", "arbitrary"),
        name="dsa",
    )(p, p, kv_norm.reshape(1, -1), w_uk_t, w_uv)


def _merge_kernel(h_ref, yr_ref, ym_ref, yd_ref, pg_ref, wb_ref, wo_ref, g_ref, o_ref):
    mixed = None
    for i, y_ref in enumerate((yr_ref, ym_ref, yd_ref)):
        proj = jnp.dot(y_ref[...].astype(BF16), wb_ref[i], preferred_element_type=F32)
        term = _sigmoid(pg_ref[:, i * D_MODEL:(i + 1) * D_MODEL]) * proj
        mixed = term if mixed is None else mixed + term
    out = jnp.dot(mixed.astype(BF16), wo_ref[...], preferred_element_type=F32)
    o_ref[...] = h_ref[...] + _rms(out, g_ref[...])


def _merge(h, y_rw, y_ml, y_dsa, p_gate, w_branch, w_out, gain):
    m, d = h.shape
    rows = lambda n: pl.BlockSpec((ROW_TILE, n), lambda i: (i, 0))
    return pl.pallas_call(
        _merge_kernel,
        out_shape=jax.ShapeDtypeStruct((m, d), F32),
        grid=(m // ROW_TILE,),
        in_specs=[
            rows(d), rows(BRANCH_W), rows(BRANCH_W), rows(BRANCH_W), rows(3 * d),
            pl.BlockSpec(w_branch.shape, lambda i: (0, 0, 0)),
            pl.BlockSpec(w_out.shape, lambda i: (0, 0)),
            pl.BlockSpec((1, d), lambda i: (0, 0)),
        ],
        out_specs=rows(d),
        compiler_params=_params("parallel"),
        name="merge",
    )(h, y_rw, y_ml, y_dsa, p_gate, w_branch, w_out, gain.reshape(1, d))


def _split_w_in(w):
    rw_end = RW_COLS
    ml_end = rw_end + 4 * BRANCH_W + 2 * ML_HEADS
    dsa_end = ml_end + BRANCH_W + DSA_LATENT + IDX_HEADS * IDX_HEAD + IDX_HEAD + IDX_HEADS
    w_rw, w_ml, w_dsa, w_gate = w[:, :rw_end], w[:, rw_end:ml_end], w[:, ml_end:dsa_end], w[:, dsa_end:]
    w_ml = jnp.pad(w_ml, ((0, 0), (0, ML_COLS_PAD - w_ml.shape[1])))
    q, c_kv, rest = w_dsa[:, :BRANCH_W], w_dsa[:, BRANCH_W:BRANCH_W + DSA_LATENT], w_dsa[:, BRANCH_W + DSA_LATENT:]
    q_idx, tail = rest[:, :IDX_HEADS * IDX_HEAD], rest[:, IDX_HEADS * IDX_HEAD:]
    w_dsa = jnp.concatenate([q, q_idx, c_kv, tail], axis=1)
    w_dsa = jnp.pad(w_dsa, ((0, 0), (0, DSA_COLS_PAD - w_dsa.shape[1])))
    return tuple(x.astype(BF16) for x in (w_rw, w_ml, w_dsa, w_gate))


def kernel(x, meta_tokens, norm_gain, ffn_w_in, ffn_w_out, w_in, rw_mu, rw_w0, rw_w_up, rw_a0, rw_a_up, rw_g_up, rw_k_k, rw_k_a, rw_r_k, rw_gn_gain, rw_gn_bias, ml_conv_w, ml_conv_b, ml_i_bias, ml_f_bias, ml_norm_gain, dsa_kv_norm, dsa_w_uk, dsa_w_uv, w_branch, w_out):
    bsz, seq, d = x.shape
    depth = norm_gain.shape[0]
    top_k = min(TOPK_MAX, seq // 4)
    t_len = seq + N_META
    t_pad = -(-t_len // SEQ_PAD_MULTIPLE) * SEQ_PAD_MULTIPLE
    h = jnp.concatenate([
        jnp.broadcast_to(meta_tokens.astype(x.dtype)[None], (bsz, N_META, d)), x,
        jnp.zeros((bsz, t_pad - t_len, d), x.dtype)], axis=1).reshape(bsz * t_pad, d)

    for l in range(depth):
        g = norm_gain[l]
        h = _ffn(h, g[0], g[1], ffn_w_in[l, 0].astype(BF16), ffn_w_out[l, 0].astype(BF16))

        w_rw, w_ml, w_dsa, w_gate = _split_w_in(w_in[l])
        p_rw = _proj(h, g[2], w_rw).reshape(bsz, t_pad, -1)
        p_ml = _proj(h, g[2], w_ml).reshape(bsz, t_pad, -1)
        p_dsa = _proj(h, g[2], w_dsa).reshape(bsz, t_pad, -1)
        p_gate = _proj(h, g[2], w_gate)

        y_rw = _rwkv(p_rw, rw_mu[l], rw_w0[l], rw_w_up[l], rw_a0[l], rw_a_up[l], rw_g_up[l], rw_k_k[l],
                     rw_k_a[l], rw_r_k[l], rw_gn_gain[l], rw_gn_bias[l])
        gate_bias = jnp.pad(jnp.concatenate([ml_i_bias[l], ml_f_bias[l]]), (0, 128 - 2 * ML_HEADS))
        y_ml = _mlstm(p_ml, ml_conv_w[l], ml_conv_b[l], gate_bias.reshape(1, 128), ml_norm_gain[l])
        y_dsa = _dsa(p_dsa, dsa_kv_norm[l], jnp.swapaxes(dsa_w_uk[l], 1, 2).astype(BF16), dsa_w_uv[l].astype(BF16), top_k)

        flat = lambda y: y.reshape(bsz * t_pad, BRANCH_W)
        h = _merge(h, flat(y_rw), flat(y_ml), flat(y_dsa), p_gate, w_branch[l].astype(BF16),
                   w_out[l].astype(BF16), g[3])
        h = _ffn(h, g[4], g[5], ffn_w_in[l, 1].astype(BF16), ffn_w_out[l, 1].astype(BF16))

    return h.reshape(bsz, t_pad, d)[:, N_META:t_len]
```

```python
import functools
import math

import jax
import jax.numpy as jnp
from jax import lax
from jax.experimental import pallas as pl
from jax.experimental.pallas import tpu as pltpu

F32 = jnp.float32
BF16 = jnp.bfloat16
HI = lax.Precision.HIGHEST

D_MODEL = 1024
D_FF = 2816
N_META = 16
STREAM_CHUNK = 64
BRANCH_W = 512
NORM_EPS = 1e-6

RW_HEADS, RW_HEAD = 8, 64
RW_COLS = 1792
RW_GN_EPS = 64e-5
RW_CHUNK = 64

ML_HEADS, ML_HEAD = 4, 128
ML_CHUNK = 64
ML_CONV = 4
ML_COLS_PAD = 2176

DSA_HEADS, DSA_HEAD, DSA_LATENT = 8, 64, 128
IDX_HEADS, IDX_HEAD = 8, 64
TOPK_MAX = 256
DSA_QBLOCK = 128
DSA_COLS_PAD = 1280
DSA_KEY_COL_BLOCK = 256

ROW_TILE = 512
FF_TILE = 1408
SEQ_PAD_MULTIPLE = 128
RW_SEQ_GROUP = 2
ML_SEQ_GROUP = 4
VMEM_LIMIT = 56 * 1024 * 1024
DSA_KEY_CHUNK = 512
DSA_SUM_ROWS = 16
DSA_ATTEND_GROUPS = 1
DSA_BISECT_STEPS = 24
DSA_BISECT_REFINE = 8
DSA_BISECT_MAX_ROUNDS = 40


def _dot(a, b):
    return jnp.dot(a.astype(BF16), b.astype(BF16), preferred_element_type=F32)


def _dot_nt(a, b):
    return lax.dot_general(a.astype(BF16), b.astype(BF16), (((1,), (1,)), ((), ())),
                           preferred_element_type=F32)


def _dot_tn(a, b):
    return lax.dot_general(a.astype(BF16), b.astype(BF16), (((0,), (0,)), ((), ())),
                           preferred_element_type=F32)


def _dot_hi(a, b):
    return jnp.dot(a, b, preferred_element_type=F32, precision=HI)


def _dot_nt_hi(a, b):
    return lax.dot_general(a, b, (((1,), (1,)), ((), ())), preferred_element_type=F32, precision=HI)


def _dot_tn_hi(a, b):
    return lax.dot_general(a, b, (((0,), (0,)), ((), ())), preferred_element_type=F32, precision=HI)


_NN = ((1,), (0,))
_NT = ((1,), (1,))
_TN = ((0,), (0,))


def _mm(a, b, contract, passes):
    dn = (contract, ((), ()))
    if passes == 6:
        return lax.dot_general(a, b, dn, preferred_element_type=F32, precision=HI)
    dot = lambda x, y: lax.dot_general(x, y, dn, preferred_element_type=F32)
    a_hi, b_hi = a.astype(BF16), b.astype(BF16)
    if passes == 1:
        return dot(a_hi, b_hi)
    a_lo = (a - a_hi.astype(F32)).astype(BF16)
    b_lo = (b - b_hi.astype(F32)).astype(BF16)
    return dot(a_hi, b_hi) + (dot(a_hi, b_lo) + dot(a_lo, b_hi))


RW_PASSES = {"gram": 1, "carry": 1, "solve": 1, "out": 1, "state": 1}


def _rms(x, gain):
    return x * lax.rsqrt(jnp.mean(x * x, axis=-1, keepdims=True) + NORM_EPS) * gain


def _sigmoid(x):
    return jax.nn.sigmoid(x)


def _tril(n, strict=False):
    r = lax.broadcasted_iota(jnp.int32, (n, n), 0)
    c = lax.broadcasted_iota(jnp.int32, (n, n), 1)
    return (c < r) if strict else (c <= r)


def _params(*sem):
    return pltpu.CompilerParams(dimension_semantics=sem, vmem_limit_bytes=VMEM_LIMIT)


def _ffn_kernel(h_ref, gpre_ref, gpost_ref, wg_ref, wu_ref, wo_ref, o_ref, xn_ref, acc_ref):
    j = pl.program_id(1)

    @pl.when(j == 0)
    def _():
        xn_ref[...] = _rms(h_ref[...], gpre_ref[...]).astype(BF16)
        acc_ref[...] = jnp.zeros_like(acc_ref)

    xn = xn_ref[...]
    gate = jnp.dot(xn, wg_ref[...], preferred_element_type=F32)
    up = jnp.dot(xn, wu_ref[...], preferred_element_type=F32)
    act = gate * _sigmoid(gate) * up
    acc_ref[...] += jnp.dot(act.astype(BF16), wo_ref[...], preferred_element_type=F32)

    @pl.when(j == pl.num_programs(1) - 1)
    def _():
        o_ref[...] = h_ref[...] + 0.5 * _rms(acc_ref[...], gpost_ref[...])


def _ffn(h, g_pre, g_post, w_in, w_out):
    m, d = h.shape
    nj = D_FF // FF_TILE
    return pl.pallas_call(
        _ffn_kernel,
        out_shape=jax.ShapeDtypeStruct((m, d), F32),
        grid=(m // ROW_TILE, nj),
        in_specs=[
            pl.BlockSpec((ROW_TILE, d), lambda i, j: (i, 0)),
            pl.BlockSpec((1, d), lambda i, j: (0, 0)),
            pl.BlockSpec((1, d), lambda i, j: (0, 0)),
            pl.BlockSpec((d, FF_TILE), lambda i, j: (0, j)),
            pl.BlockSpec((d, FF_TILE), lambda i, j: (0, j + nj)),
            pl.BlockSpec((FF_TILE, d), lambda i, j: (j, 0)),
        ],
        out_specs=pl.BlockSpec((ROW_TILE, d), lambda i, j: (i, 0)),
        scratch_shapes=[pltpu.VMEM((ROW_TILE, d), BF16), pltpu.VMEM((ROW_TILE, d), F32)],
        compiler_params=_params("parallel", "arbitrary"),
        name="ffn",
    )(h, g_pre.reshape(1, d), g_post.reshape(1, d), w_in, w_in, w_out)


def _proj_kernel(h_ref, g_ref, w_ref, o_ref):
    o_ref[...] = jnp.dot(_rms(h_ref[...], g_ref[...]).astype(BF16), w_ref[...],
                         preferred_element_type=F32)


def _proj(h, gain, w):
    m, d = h.shape
    n = w.shape[1]
    return pl.pallas_call(
        _proj_kernel,
        out_shape=jax.ShapeDtypeStruct((m, n), F32),
        grid=(m // ROW_TILE,),
        in_specs=[
            pl.BlockSpec((ROW_TILE, d), lambda i: (i, 0)),
            pl.BlockSpec((1, d), lambda i: (0, 0)),
            pl.BlockSpec((d, n), lambda i: (0, 0)),
        ],
        out_specs=pl.BlockSpec((ROW_TILE, n), lambda i: (i, 0)),
        compiler_params=_params("parallel"),
        name="in_proj",
    )(h, gain.reshape(1, d), w)


def _rwkv_kernel(p_ref, mu_ref, w0_ref, wup_ref, a0_ref, aup_ref, gup_ref, kk_ref, ka_ref, rk_ref,
                 gng_ref, gnb_ref, o_ref, prev_ref, s_ref):
    @pl.when(pl.program_id(1) == 0)
    def _():
        prev_ref[...] = jnp.zeros_like(prev_ref)
        s_ref[...] = jnp.zeros_like(s_ref)

    L = RW_CHUNK
    lower = _tril(L)
    strict = _tril(L, strict=True)
    tril_f = lower.astype(F32)
    row = lax.broadcasted_iota(jnp.int32, (L, 1), 0)
    steps = int(math.log2(L))
    mmp = RW_PASSES
    seqs = range(p_ref.shape[0])

    r, k2, v, a, gate, kappa, g_incl, g_excl, g_inv = ([] for _ in range(9))
    for g in seqs:
        p = p_ref[g]
        prev = jnp.where(row == 0, prev_ref[g], pltpu.roll(p, 1, 0))
        prev_ref[g] = p[L - 1:L, :]
        ps = p + (prev - p) * mu_ref[...]
        k = ps[:, BRANCH_W:2 * BRANCH_W]
        d = w0_ref[...] + _dot(jnp.tanh(ps[:, 1536:1600]), wup_ref[...])
        log_w = -math.exp(-0.5) * _sigmoid(d)
        a.append(_sigmoid(a0_ref[...] + _dot(ps[:, 1600:1664], aup_ref[...])))
        gate.append(_dot(_sigmoid(ps[:, 1664:1792]), gup_ref[...]))
        r.append(ps[:, 0:BRANCH_W])
        v.append(ps[:, 2 * BRANCH_W:3 * BRANCH_W])
        kappa.append(k * kk_ref[...])
        k2.append(k * (1.0 + (a[g] - 1.0) * ka_ref[...]))
        cum = _dot_hi(tril_f, log_w)
        g_incl.append(jnp.exp(cum))
        g_excl.append(jnp.exp(cum - log_w))
        g_inv.append(jnp.exp(-cum))

    units = [(g, h) for g in seqs for h in range(RW_HEADS)]
    idx = range(len(units))
    sl = [slice(h * RW_HEAD, (h + 1) * RW_HEAD) for _, h in units]
    lhs, b_rows, k_rows = [], [], []
    for i, (g, h) in enumerate(units):
        kap = kappa[g][:, sl[i]]
        kh = kap / jnp.maximum(jnp.sqrt(jnp.sum(kap * kap, axis=-1, keepdims=True)), 1e-12)
        lhs.append(jnp.concatenate([r[g][:, sl[i]] * g_incl[g][:, sl[i]], -kh * g_excl[g][:, sl[i]]], axis=0))
        b_rows.append(a[g][:, sl[i]] * kh * g_inv[g][:, sl[i]])
        k_rows.append(k2[g][:, sl[i]] * g_inv[g][:, sl[i]])
    v_u = [v[g][:, sl[i]] for i, (g, h) in enumerate(units)]
    s0 = [s_ref[g * RW_HEADS + h] for g, h in units]
    gram = [_mm(lhs[i], jnp.concatenate([b_rows[i], k_rows[i]], axis=0), _NT, mmp["gram"]) for i in idx]
    ls0 = [_mm(lhs[i], s0[i], _NT, mmp["carry"]) for i in idx]
    u = [ls0[i][L:] + _mm(jnp.where(strict, gram[i][L:, L:], 0.0), v_u[i], _NN, mmp["gram"]) for i in idx]
    n_pow = [jnp.where(strict, gram[i][L:, :L], 0.0) for i in idx]
    for step in range(steps):
        u = [u[i] + _mm(n_pow[i], u[i], _NN, mmp["solve"]) for i in idx]
        if step + 1 < steps:
            n_pow = [_mm(n_pow[i], n_pow[i], _NN, mmp["solve"]) for i in idx]

    for i, (g, h) in enumerate(units):
        y = (ls0[i][:L] + _mm(jnp.where(lower, gram[i][:L, :L], 0.0), u[i], _NN, mmp["out"])
             + _mm(jnp.where(lower, gram[i][:L, L:], 0.0), v_u[i], _NN, mmp["out"]))
        s_ref[g * RW_HEADS + h] = (s0[i] + _mm(u[i], b_rows[i], _TN, mmp["state"])
                                   + _mm(v_u[i], k_rows[i], _TN, mmp["state"])) * g_incl[g][L - 1:L, sl[i]]

        mean = jnp.mean(y, axis=-1, keepdims=True)
        var = jnp.mean((y - mean) ** 2, axis=-1, keepdims=True)
        y = (y - mean) * lax.rsqrt(var + RW_GN_EPS) * gng_ref[:, sl[i]] + gnb_ref[:, sl[i]]
        bonus = jnp.sum(r[g][:, sl[i]] * k2[g][:, sl[i]] * rk_ref[:, sl[i]], axis=-1, keepdims=True)
        o_ref[g, :, sl[i]] = (y + bonus * v_u[i]) * gate[g][:, sl[i]]


def _rwkv(p, mu, w0, w_up, a0, a_up, g_up, k_k, k_a, r_k, gn_gain, gn_bias):
    b, t, c = p.shape
    group = math.gcd(b, RW_SEQ_GROUP)
    row = lambda x: x.reshape(1, -1)
    vec = lambda n: pl.BlockSpec((1, n), lambda i, j: (0, 0))
    mat = lambda shp: pl.BlockSpec(shp, lambda i, j: (0, 0))
    return pl.pallas_call(
        _rwkv_kernel,
        out_shape=jax.ShapeDtypeStruct((b, t, BRANCH_W), F32),
        grid=(b // group, t // RW_CHUNK),
        in_specs=[
            pl.BlockSpec((group, RW_CHUNK, c), lambda i, j: (i, j, 0)),
            vec(c), vec(BRANCH_W), mat(w_up.shape), vec(BRANCH_W), mat(a_up.shape), mat(g_up.shape),
            vec(BRANCH_W), vec(BRANCH_W), vec(BRANCH_W), vec(BRANCH_W), vec(BRANCH_W),
        ],
        out_specs=pl.BlockSpec((group, RW_CHUNK, BRANCH_W), lambda i, j: (i, j, 0)),
        scratch_shapes=[pltpu.VMEM((group, 1, c), F32),
                        pltpu.VMEM((group * RW_HEADS, RW_HEAD, RW_HEAD), F32)],
        compiler_params=_params("parallel", "arbitrary"),
        name="rwkv7",
    )(p, row(mu), row(w0), w_up, row(a0), a_up, g_up, row(k_k), row(k_a), row(r_k), row(gn_gain),
      row(gn_bias))


def _log_sigmoid(x):
    return jnp.minimum(x, 0.0) - jnp.log1p(jnp.exp(-jnp.abs(x)))


def _mlstm_kernel(p_ref, cw_ref, cb_ref, gb_ref, ng_ref, o_ref, tail_ref, c_ref, n_ref, m_ref):
    @pl.when(pl.program_id(1) == 0)
    def _():
        tail_ref[...] = jnp.zeros_like(tail_ref)
        c_ref[...] = jnp.zeros_like(c_ref)
        n_ref[...] = jnp.zeros_like(n_ref)
        m_ref[...] = jnp.zeros_like(m_ref)

    L = ML_CHUNK
    qk_w = 2 * BRANCH_W
    lower = _tril(L)
    tril_f = lower.astype(F32)
    seqs = range(p_ref.shape[0])

    x = [p_ref[g] for g in seqs]
    q, k, gates, b_all, b_t, g_t = [], [], [], [], [], []
    for g in seqs:
        qk_in = x[g][:, :qk_w]
        ext = jnp.concatenate([tail_ref[g], qk_in], axis=0)
        tail_ref[g] = qk_in[L - 8:L, :]
        conv = cb_ref[...] + cw_ref[ML_CONV - 1:ML_CONV, :] * qk_in
        for j in range(ML_CONV - 1):
            conv = conv + cw_ref[j:j + 1, :] * pltpu.roll(ext, ML_CONV - 1 - j, 0)[8:8 + L]
        qk = conv * _sigmoid(conv)
        q.append(qk[:, :BRANCH_W])
        k.append(qk[:, BRANCH_W:] * ML_HEAD ** -0.5)
        gates.append(x[g][:, qk_w + 2 * BRANCH_W:] + gb_ref[...])
        b_all.append(_dot_hi(tril_f, _log_sigmoid(gates[g])))
        b_t.append(b_all[g].T)
        g_t.append(gates[g].T)

    units = [(g, h) for g in seqs for h in range(ML_HEADS)]
    sl = [slice(h * ML_HEAD, (h + 1) * ML_HEAD) for _, h in units]
    q_u = [q[g][:, sl[i]] for i, (g, h) in enumerate(units)]
    k_u = [k[g][:, sl[i]] for i, (g, h) in enumerate(units)]
    v_u = [x[g][:, qk_w + h * ML_HEAD:qk_w + (h + 1) * ML_HEAD] for g, h in units]
    b_col = [b_all[g][:, ML_HEADS + h:ML_HEADS + h + 1] for g, h in units]
    b_row = [b_t[g][ML_HEADS + h:ML_HEADS + h + 1, :] for g, h in units]
    i_col = [gates[g][:, h:h + 1] for g, h in units]
    i_row = [g_t[g][h:h + 1, :] for g, h in units]
    m_st = [m_ref[g, h:h + 1, 0:1] for g, h in units]
    n_st = [n_ref[g, h:h + 1, :] for g, h in units]
    c_st = [c_ref[g * ML_HEADS + h] for g, h in units]
    idx = range(len(units))

    log_d = [jnp.where(lower, b_col[i] - b_row[i] + i_row[i], -jnp.inf) for i in idx]
    m_j = [jnp.maximum(b_col[i] + m_st[i], jnp.max(log_d[i], axis=-1, keepdims=True)) for i in idx]
    s = [_dot_nt(q_u[i], k_u[i]) * jnp.exp(log_d[i] - m_j[i]) for i in idx]
    inter = [jnp.exp(b_col[i] + m_st[i] - m_j[i]) for i in idx]
    num = [_dot(s[i], v_u[i]) + inter[i] * _dot_nt(q_u[i], c_st[i]) for i in idx]
    den = [jnp.sum(s[i], axis=-1, keepdims=True)
           + inter[i] * jnp.sum(q_u[i] * n_st[i], axis=-1, keepdims=True) for i in idx]
    hh = [num[i] / jnp.maximum(jnp.abs(den[i]), jnp.exp(-m_j[i])) for i in idx]

    g_tot = [b_col[i][L - 1:L, :] for i in idx]
    w_log = [g_tot[i] - b_col[i] + i_col[i] for i in idx]
    m_new = [jnp.maximum(g_tot[i] + m_st[i], jnp.max(w_log[i], axis=0, keepdims=True)) for i in idx]
    wgt = [jnp.exp(w_log[i] - m_new[i]) for i in idx]
    dec = [jnp.exp(g_tot[i] + m_st[i] - m_new[i]) for i in idx]
    c_new = [dec[i] * c_st[i] + _dot_tn(wgt[i] * v_u[i], k_u[i]) for i in idx]
    n_new = [dec[i] * n_st[i] + jnp.sum(wgt[i] * k_u[i], axis=0, keepdims=True) for i in idx]
    hn = [hh[i] * lax.rsqrt(jnp.mean(hh[i] * hh[i], axis=-1, keepdims=True) + NORM_EPS) * ng_ref[:, sl[i]]
          for i in idx]

    for i, (g, h) in enumerate(units):
        c_ref[g * ML_HEADS + h] = c_new[i]
        n_ref[g, h:h + 1, :] = n_new[i]
        m_ref[g, h:h + 1, :] = jnp.broadcast_to(m_new[i], (1, ML_HEAD))
        o_gate = x[g][:, qk_w + BRANCH_W + h * ML_HEAD:qk_w + BRANCH_W + (h + 1) * ML_HEAD]
        o_ref[g, :, sl[i]] = _sigmoid(o_gate) * hn[i]


def _mlstm(p, conv_w, conv_b, gate_bias, norm_gain):
    b, t, c = p.shape
    group = math.gcd(b, ML_SEQ_GROUP)
    return pl.pallas_call(
        _mlstm_kernel,
        out_shape=jax.ShapeDtypeStruct((b, t, BRANCH_W), F32),
        grid=(b // group, t // ML_CHUNK),
        in_specs=[
            pl.BlockSpec((group, ML_CHUNK, c), lambda i, j: (i, j, 0)),
            pl.BlockSpec(conv_w.shape, lambda i, j: (0, 0)),
            pl.BlockSpec((1, 2 * BRANCH_W), lambda i, j: (0, 0)),
            pl.BlockSpec((1, 128), lambda i, j: (0, 0)),
            pl.BlockSpec((1, BRANCH_W), lambda i, j: (0, 0)),
        ],
        out_specs=pl.BlockSpec((group, ML_CHUNK, BRANCH_W), lambda i, j: (i, j, 0)),
        scratch_shapes=[
            pltpu.VMEM((group, 8, 2 * BRANCH_W), F32),
            pltpu.VMEM((group * ML_HEADS, ML_HEAD, ML_HEAD), F32),
            pltpu.VMEM((group, 8, ML_HEAD), F32),
            pltpu.VMEM((group, 8, ML_HEAD), F32),
        ],
        compiler_params=_params("parallel", "arbitrary"),
        name="mlstm",
    )(p, conv_w, conv_b.reshape(1, -1), gate_bias, norm_gain.reshape(1, -1))


def _sum_sublane_groups(x):
    n = x.shape[0] // 32
    g = x.reshape(4 * n, 8, x.shape[1])
    parts = [g[i * n:(i + 1) * n] for i in range(4)]
    return (jnp.sum(parts[0], axis=0) + jnp.sum(parts[1], axis=0)) + (jnp.sum(parts[2], axis=0) + jnp.sum(parts[3], axis=0))


def _dsa_kernel(pq_ref, pk_ref, kvn_ref, wukt_ref, wuv_ref, o_ref, ckv_ref, ckvt_ref, kidx_ref, sc_ref,
                *, top_k):
    qb = pl.program_id(1)
    tq = DSA_QBLOCK
    kc = DSA_KEY_CHUNK
    tk = pk_ref.shape[1]
    n_chunks_max = ckvt_ref.shape[0]
    hq = DSA_HEADS * tq

    @pl.when(qb == 0)
    def _():
        keys = pk_ref[0]
        ckv = _rms(keys[:, :DSA_LATENT], kvn_ref[...])
        pad_rows = n_chunks_max * kc - tk
        ckv_ref[0:tk, :] = ckv.astype(BF16)
        ckv_ref[tk:, :] = jnp.zeros((pad_rows, DSA_LATENT), BF16)
        kidx_ref[0:tk, :] = keys[:, DSA_LATENT:DSA_LATENT + IDX_HEAD].astype(BF16)
        kidx_ref[tk:, :] = jnp.zeros((pad_rows, IDX_HEAD), BF16)
        ckv_t = jnp.concatenate([ckv.T, jnp.zeros((DSA_LATENT, pad_rows), F32)], axis=1)
        extra = jnp.where(lax.broadcasted_iota(jnp.int32, (DSA_SUM_ROWS, kc), 0) == 0, 1.0, 0.0)
        for c in range(n_chunks_max):
            ckvt_ref[c] = jnp.concatenate([ckv_t[:, c * kc:(c + 1) * kc], extra], axis=0).astype(BF16)

    n_chunks = jnp.minimum(lax.shift_right_logical(qb * tq + tq + N_META + kc - 1, int(math.log2(kc))),
                           n_chunks_max)
    q_pos = qb * tq + lax.broadcasted_iota(jnp.int32, (1, tq), 1)
    q_chunk = jnp.where(q_pos < N_META, 0,
                        1 + lax.shift_right_arithmetic(q_pos - N_META, int(math.log2(STREAM_CHUNK))))
    n_allowed = jnp.minimum(N_META + STREAM_CHUNK * q_chunk, tk)

    def rows_of(c):
        return pl.ds(pl.multiple_of(c * kc, kc), kc)

    pq = pq_ref[0]
    q = pq[:, :BRANCH_W]
    qi_t = jnp.concatenate([pq[:, BRANCH_W + j * 128:BRANCH_W + (j + 1) * 128].T
                            for j in range(IDX_HEADS * IDX_HEAD // 128)], axis=0)
    qi_t = jnp.concatenate([qi_t[h * IDX_HEAD:(h + 1) * IDX_HEAD] for h in range(IDX_HEADS)], axis=1)
    tail_t = pq[:, DSA_COLS_PAD - 128:].T
    w_t = tail_t[IDX_HEAD:IDX_HEAD + IDX_HEADS] * (IDX_HEADS * IDX_HEAD) ** -0.5
    w_row = jnp.concatenate([w_t[h:h + 1] for h in range(IDX_HEADS)], axis=1)
    qi_t = qi_t.astype(BF16)

    def score_chunk(c, bounds):
        lo, hi = bounds
        s = jnp.maximum(jnp.dot(kidx_ref[rows_of(c), :], qi_t, preferred_element_type=F32), 0.0) * w_row
        score = s[:, 0:tq]
        for h in range(1, IDX_HEADS):
            score = score + s[:, h * tq:(h + 1) * tq]
        k_pos = c * kc + lax.broadcasted_iota(jnp.int32, (kc, 1), 0)
        allowed = k_pos < n_allowed
        sc_ref[rows_of(c), :] = jnp.where(allowed, score, -jnp.inf)
        lo = jnp.minimum(lo, jnp.min(jnp.where(allowed, score, jnp.inf), axis=0, keepdims=True))
        hi = jnp.maximum(hi, jnp.max(jnp.where(allowed, score, -jnp.inf), axis=0, keepdims=True))
        return lo, hi

    lo, row_max = lax.fori_loop(0, n_chunks, score_chunk,
                                (jnp.full((1, tq), jnp.inf, F32), jnp.full((1, tq), -jnp.inf, F32)))
    hi = row_max + jnp.maximum(jnp.abs(row_max) * 2.0 ** -20, 1e-30)

    def count(pred, level):
        def body(c, acc):
            return acc + _sum_sublane_groups(jnp.where(pred(sc_ref[rows_of(c), :], level), 1.0, 0.0))
        return jnp.sum(lax.fori_loop(0, n_chunks, body, jnp.zeros((8, tq), F32)), axis=0, keepdims=True)

    def bisect(_, bracket):
        lo, hi = bracket
        mid = 0.5 * lo + 0.5 * hi
        ge = count(jnp.greater_equal, mid) >= top_k
        return jnp.where(ge, mid, lo), jnp.where(ge, hi, mid)

    def snap(lo):
        def body(c, val):
            s = sc_ref[rows_of(c), :]
            return jnp.minimum(val, jnp.min(jnp.where(s >= lo, s, jnp.inf), axis=0, keepdims=True))
        val = lax.fori_loop(0, n_chunks, body, jnp.full((1, tq), jnp.inf, F32))
        return val, count(jnp.greater, val)

    lo, hi = lax.fori_loop(0, DSA_BISECT_STEPS, bisect, (lo, hi))
    thr, n_above = snap(lo)

    def unsettled(state):
        return jnp.logical_and(jnp.max(state[3]) >= top_k, state[4] < DSA_BISECT_MAX_ROUNDS)

    def refine(state):
        lo, hi = lax.fori_loop(0, DSA_BISECT_REFINE, bisect, (state[0], state[1]))
        thr, n_above = snap(lo)
        return lo, hi, thr, n_above, state[4] + 1

    _, _, thr, n_above, _ = lax.while_loop(unsettled, refine, (lo, hi, thr, n_above, jnp.int32(0)))

    @pl.when(jnp.max(count(jnp.greater_equal, thr)) > top_k)
    def _():
        need = top_k - n_above
        earlier = (lax.broadcasted_iota(jnp.int32, (kc, kc), 1)
                   < lax.broadcasted_iota(jnp.int32, (kc, kc), 0)).astype(BF16)

        def body(c, seen):
            s = sc_ref[rows_of(c), :]
            tie = jnp.where(s == thr, 1.0, 0.0)
            rank = seen + jnp.dot(earlier, tie.astype(BF16), preferred_element_type=F32)
            sc_ref[rows_of(c), :] = jnp.where((tie > 0.0) & (rank >= need), -jnp.inf, s)
            return seen + jnp.sum(tie, axis=0, keepdims=True)

        lax.fori_loop(0, n_chunks, body, jnp.zeros((1, tq), F32))

    q_t = jnp.concatenate([_mm(wukt_ref[h], q[:, h * DSA_HEAD:(h + 1) * DSA_HEAD], _NT, 1)
                           for h in range(DSA_HEADS)], axis=1)
    q_t = (q_t * (DSA_HEAD ** -0.5 * math.log2(math.e))).astype(BF16)

    def attend(c, carry):
        m, acc = carry
        bias = jnp.where(sc_ref[rows_of(c), :] >= thr, 0.0, -jnp.inf)
        keys = ckv_ref[rows_of(c), :]
        keys_t = ckvt_ref[c]
        m_out, acc_out = [], []
        width = hq // DSA_ATTEND_GROUPS
        for g in range(DSA_ATTEND_GROUPS):
            cols = slice(g * width, (g + 1) * width)
            logits = (jnp.dot(keys, q_t[:, cols], preferred_element_type=F32)
                      + jnp.concatenate([bias] * (width // tq), axis=1))
            m_new = jnp.maximum(m[:, cols], jnp.max(logits, axis=0, keepdims=True))
            m_safe = jnp.where(m_new == -jnp.inf, 0.0, m_new)
            e = jnp.exp2(logits - m_safe)
            acc_out.append(acc[:, cols] * jnp.exp2(m[:, cols] - m_safe)
                           + jnp.dot(keys_t, e.astype(BF16), preferred_element_type=F32))
            m_out.append(m_new)
        return jnp.concatenate(m_out, axis=1), jnp.concatenate(acc_out, axis=1)

    _, acc = lax.fori_loop(0, n_chunks, attend,
                           (jnp.full((1, hq), -jnp.inf, F32), jnp.zeros((DSA_LATENT + DSA_SUM_ROWS, hq), F32)))
    out_t = acc[:DSA_LATENT] / acc[DSA_LATENT:DSA_LATENT + 1]
    for h in range(DSA_HEADS):
        o_ref[0, :, h * DSA_HEAD:(h + 1) * DSA_HEAD] = _mm(out_t[:, h * tq:(h + 1) * tq], wuv_ref[h], _TN, 1)


def _dsa(p, kv_norm, w_uk_t, w_uv, top_k):
    b, t, c = p.shape
    n_chunks = -(-t // DSA_KEY_CHUNK)
    return pl.pallas_call(
        functools.partial(_dsa_kernel, top_k=top_k),
        out_shape=jax.ShapeDtypeStruct((b, t, BRANCH_W), F32),
        grid=(b, t // DSA_QBLOCK),
        in_specs=[
            pl.BlockSpec((1, DSA_QBLOCK, c), lambda i, j: (i, j, 0)),
            pl.BlockSpec((1, t, DSA_KEY_COL_BLOCK), lambda i, j: (i, 0, c // DSA_KEY_COL_BLOCK - 1)),
            pl.BlockSpec((1, DSA_LATENT), lambda i, j: (0, 0)),
            pl.BlockSpec(w_uk_t.shape, lambda i, j: (0, 0, 0)),
            pl.BlockSpec(w_uv.shape, lambda i, j: (0, 0, 0)),
        ],
        out_specs=pl.BlockSpec((1, DSA_QBLOCK, BRANCH_W), lambda i, j: (i, j, 0)),
        scratch_shapes=[
            pltpu.VMEM((n_chunks * DSA_KEY_CHUNK, DSA_LATENT), BF16),
            pltpu.VMEM((n_chunks, DSA_LATENT + DSA_SUM_ROWS, DSA_KEY_CHUNK), BF16),
            pltpu.VMEM((n_chunks * DSA_KEY_CHUNK, IDX_HEAD), BF16),
            pltpu.VMEM((n_chunks * DSA_KEY_CHUNK, DSA_QBLOCK), F32),
        ],
        compiler_params=_params("parallel", "arbitrary"),
        name="dsa",
    )(p, p, kv_norm.reshape(1, -1), w_uk_t, w_uv)


def _merge_kernel(h_ref, yr_ref, ym_ref, yd_ref, pg_ref, wb_ref, wo_ref, g_ref, o_ref):
    mixed = None
    for i, y_ref in enumerate((yr_ref, ym_ref, yd_ref)):
        proj = jnp.dot(y_ref[...].astype(BF16), wb_ref[i], preferred_element_type=F32)
        term = _sigmoid(pg_ref[:, i * D_MODEL:(i + 1) * D_MODEL]) * proj
        mixed = term if mixed is None else mixed + term
    out = jnp.dot(mixed.astype(BF16), wo_ref[...], preferred_element_type=F32)
    o_ref[...] = h_ref[...] + _rms(out, g_ref[...])


def _merge(h, y_rw, y_ml, y_dsa, p_gate, w_branch, w_out, gain):
    m, d = h.shape
    rows = lambda n: pl.BlockSpec((ROW_TILE, n), lambda i: (i, 0))
    return pl.pallas_call(
        _merge_kernel,
        out_shape=jax.ShapeDtypeStruct((m, d), F32),
        grid=(m // ROW_TILE,),
        in_specs=[
            rows(d), rows(BRANCH_W), rows(BRANCH_W), rows(BRANCH_W), rows(3 * d),
            pl.BlockSpec(w_branch.shape, lambda i: (0, 0, 0)),
            pl.BlockSpec(w_out.shape, lambda i: (0, 0)),
            pl.BlockSpec((1, d), lambda i: (0, 0)),
        ],
        out_specs=rows(d),
        compiler_params=_params("parallel"),
        name="merge",
    )(h, y_rw, y_ml, y_dsa, p_gate, w_branch, w_out, gain.reshape(1, d))


def _split_w_in(w):
    rw_end = RW_COLS
    ml_end = rw_end + 4 * BRANCH_W + 2 * ML_HEADS
    dsa_end = ml_end + BRANCH_W + DSA_LATENT + IDX_HEADS * IDX_HEAD + IDX_HEAD + IDX_HEADS
    w_rw, w_ml, w_dsa, w_gate = w[:, :rw_end], w[:, rw_end:ml_end], w[:, ml_end:dsa_end], w[:, dsa_end:]
    w_ml = jnp.pad(w_ml, ((0, 0), (0, ML_COLS_PAD - w_ml.shape[1])))
    q, c_kv, rest = w_dsa[:, :BRANCH_W], w_dsa[:, BRANCH_W:BRANCH_W + DSA_LATENT], w_dsa[:, BRANCH_W + DSA_LATENT:]
    q_idx, tail = rest[:, :IDX_HEADS * IDX_HEAD], rest[:, IDX_HEADS * IDX_HEAD:]
    w_dsa = jnp.concatenate([q, q_idx, c_kv, tail], axis=1)
    w_dsa = jnp.pad(w_dsa, ((0, 0), (0, DSA_COLS_PAD - w_dsa.shape[1])))
    return tuple(x.astype(BF16) for x in (w_rw, w_ml, w_dsa, w_gate))


def kernel(x, meta_tokens, norm_gain, ffn_w_in, ffn_w_out, w_in, rw_mu, rw_w0, rw_w_up, rw_a0, rw_a_up, rw_g_up, rw_k_k, rw_k_a, rw_r_k, rw_gn_gain, rw_gn_bias, ml_conv_w, ml_conv_b, ml_i_bias, ml_f_bias, ml_norm_gain, dsa_kv_norm, dsa_w_uk, dsa_w_uv, w_branch, w_out):
    bsz, seq, d = x.shape
    depth = norm_gain.shape[0]
    top_k = min(TOPK_MAX, seq // 4)
    t_len = seq + N_META
    t_pad = -(-t_len // SEQ_PAD_MULTIPLE) * SEQ_PAD_MULTIPLE
    h = jnp.concatenate([
        jnp.broadcast_to(meta_tokens.astype(x.dtype)[None], (bsz, N_META, d)), x,
        jnp.zeros((bsz, t_pad - t_len, d), x.dtype)], axis=1).reshape(bsz * t_pad, d)

    for l in range(depth):
        g = norm_gain[l]
        h = _ffn(h, g[0], g[1], ffn_w_in[l, 0].astype(BF16), ffn_w_out[l, 0].astype(BF16))

        w_rw, w_ml, w_dsa, w_gate = _split_w_in(w_in[l])
        p_rw = _proj(h, g[2], w_rw).reshape(bsz, t_pad, -1)
        p_ml = _proj(h, g[2], w_ml).reshape(bsz, t_pad, -1)
        p_dsa = _proj(h, g[2], w_dsa).reshape(bsz, t_pad, -1)
        p_gate = _proj(h, g[2], w_gate)

        y_rw = _rwkv(p_rw, rw_mu[l], rw_w0[l], rw_w_up[l], rw_a0[l], rw_a_up[l], rw_g_up[l], rw_k_k[l],
                     rw_k_a[l], rw_r_k[l], rw_gn_gain[l], rw_gn_bias[l])
        gate_bias = jnp.pad(jnp.concatenate([ml_i_bias[l], ml_f_bias[l]]), (0, 128 - 2 * ML_HEADS))
        y_ml = _mlstm(p_ml, ml_conv_w[l], ml_conv_b[l], gate_bias.reshape(1, 128), ml_norm_gain[l])
        y_dsa = _dsa(p_dsa, dsa_kv_norm[l], jnp.swapaxes(dsa_w_uk[l], 1, 2).astype(BF16), dsa_w_uv[l].astype(BF16), top_k)

        flat = lambda y: y.reshape(bsz * t_pad, BRANCH_W)
        h = _merge(h, flat(y_rw), flat(y_ml), flat(y_dsa), p_gate, w_branch[l].astype(BF16),
                   w_out[l].astype(BF16), g[3])
        h = _ffn(h, g[4], g[5], ffn_w_in[l, 1].astype(BF16), ffn_w_out[l, 1].astype(BF16))

    return h.reshape(bsz, t_pad, d)[:, N_META:t_len]
```

```python
import functools
import math

import jax
import jax.numpy as jnp
from jax import lax
from jax.experimental import pallas as pl
from jax.experimental.pallas import tpu as pltpu

F32 = jnp.float32
BF16 = jnp.bfloat16
HI = lax.Precision.HIGHEST

D_MODEL = 1024
D_FF = 2816
N_META = 16
STREAM_CHUNK = 64
BRANCH_W = 512
NORM_EPS = 1e-6

RW_HEADS, RW_HEAD = 8, 64
RW_COLS = 1792
RW_GN_EPS = 64e-5
RW_CHUNK = 64

ML_HEADS, ML_HEAD = 4, 128
ML_CHUNK = 64
ML_CONV = 4
ML_COLS_PAD = 2176

DSA_HEADS, DSA_HEAD, DSA_LATENT = 8, 64, 128
IDX_HEADS, IDX_HEAD = 8, 64
TOPK_MAX = 256
DSA_QBLOCK = 128
DSA_COLS_PAD = 1280
DSA_KEY_COL_BLOCK = 256

ROW_TILE = 512
FF_TILE = 1408
SEQ_PAD_MULTIPLE = 128
RW_SEQ_GROUP = 2
ML_SEQ_GROUP = 4
VMEM_LIMIT = 56 * 1024 * 1024
DSA_KEY_CHUNK = 512
DSA_SUM_ROWS = 16
DSA_BISECT_STEPS = 24
DSA_BISECT_REFINE = 8
DSA_BISECT_MAX_ROUNDS = 40


def _dot(a, b):
    return jnp.dot(a.astype(BF16), b.astype(BF16), preferred_element_type=F32)


def _dot_nt(a, b):
    return lax.dot_general(a.astype(BF16), b.astype(BF16), (((1,), (1,)), ((), ())),
                           preferred_element_type=F32)


def _dot_tn(a, b):
    return lax.dot_general(a.astype(BF16), b.astype(BF16), (((0,), (0,)), ((), ())),
                           preferred_element_type=F32)


def _dot_hi(a, b):
    return jnp.dot(a, b, preferred_element_type=F32, precision=HI)


def _dot_nt_hi(a, b):
    return lax.dot_general(a, b, (((1,), (1,)), ((), ())), preferred_element_type=F32, precision=HI)


def _dot_tn_hi(a, b):
    return lax.dot_general(a, b, (((0,), (0,)), ((), ())), preferred_element_type=F32, precision=HI)


_NN = ((1,), (0,))
_NT = ((1,), (1,))
_TN = ((0,), (0,))


def _mm(a, b, contract, passes):
    dn = (contract, ((), ()))
    if passes == 6:
        return lax.dot_general(a, b, dn, preferred_element_type=F32, precision=HI)
    dot = lambda x, y: lax.dot_general(x, y, dn, preferred_element_type=F32)
    a_hi, b_hi = a.astype(BF16), b.astype(BF16)
    if passes == 1:
        return dot(a_hi, b_hi)
    a_lo = (a - a_hi.astype(F32)).astype(BF16)
    b_lo = (b - b_hi.astype(F32)).astype(BF16)
    return dot(a_hi, b_hi) + (dot(a_hi, b_lo) + dot(a_lo, b_hi))


RW_PASSES = {"gram": 1, "carry": 1, "solve": 1, "out": 1, "state": 1, "head_sum": 1}


def _rms(x, gain):
    return x * lax.rsqrt(jnp.mean(x * x, axis=-1, keepdims=True) + NORM_EPS) * gain


def _sigmoid(x):
    return jax.nn.sigmoid(x)


def _tril(n, strict=False):
    r = lax.broadcasted_iota(jnp.int32, (n, n), 0)
    c = lax.broadcasted_iota(jnp.int32, (n, n), 1)
    return (c < r) if strict else (c <= r)


def _params(*sem):
    return pltpu.CompilerParams(dimension_semantics=sem, vmem_limit_bytes=VMEM_LIMIT)


def _ffn_kernel(h_ref, gpre_ref, gpost_ref, wg_ref, wu_ref, wo_ref, o_ref, xn_ref, acc_ref):
    j = pl.program_id(1)

    @pl.when(j == 0)
    def _():
        xn_ref[...] = _rms(h_ref[...], gpre_ref[...]).astype(BF16)
        acc_ref[...] = jnp.zeros_like(acc_ref)

    xn = xn_ref[...]
    gate = jnp.dot(xn, wg_ref[...], preferred_element_type=F32)
    up = jnp.dot(xn, wu_ref[...], preferred_element_type=F32)
    act = gate * _sigmoid(gate) * up
    acc_ref[...] += jnp.dot(act.astype(BF16), wo_ref[...], preferred_element_type=F32)

    @pl.when(j == pl.num_programs(1) - 1)
    def _():
        o_ref[...] = h_ref[...] + 0.5 * _rms(acc_ref[...], gpost_ref[...])


def _ffn(h, g_pre, g_post, w_in, w_out):
    m, d = h.shape
    nj = D_FF // FF_TILE
    return pl.pallas_call(
        _ffn_kernel,
        out_shape=jax.ShapeDtypeStruct((m, d), F32),
        grid=(m // ROW_TILE, nj),
        in_specs=[
            pl.BlockSpec((ROW_TILE, d), lambda i, j: (i, 0)),
            pl.BlockSpec((1, d), lambda i, j: (0, 0)),
            pl.BlockSpec((1, d), lambda i, j: (0, 0)),
            pl.BlockSpec((d, FF_TILE), lambda i, j: (0, j)),
            pl.BlockSpec((d, FF_TILE), lambda i, j: (0, j + nj)),
            pl.BlockSpec((FF_TILE, d), lambda i, j: (j, 0)),
        ],
        out_specs=pl.BlockSpec((ROW_TILE, d), lambda i, j: (i, 0)),
        scratch_shapes=[pltpu.VMEM((ROW_TILE, d), BF16), pltpu.VMEM((ROW_TILE, d), F32)],
        compiler_params=_params("parallel", "arbitrary"),
        name="ffn",
    )(h, g_pre.reshape(1, d), g_post.reshape(1, d), w_in, w_in, w_out)


def _proj_kernel(h_ref, g_ref, w_ref, o_ref):
    o_ref[...] = jnp.dot(_rms(h_ref[...], g_ref[...]).astype(BF16), w_ref[...],
                         preferred_element_type=F32)


def _proj(h, gain, w):
    m, d = h.shape
    n = w.shape[1]
    return pl.pallas_call(
        _proj_kernel,
        out_shape=jax.ShapeDtypeStruct((m, n), F32),
        grid=(m // ROW_TILE,),
        in_specs=[
            pl.BlockSpec((ROW_TILE, d), lambda i: (i, 0)),
            pl.BlockSpec((1, d), lambda i: (0, 0)),
            pl.BlockSpec((d, n), lambda i: (0, 0)),
        ],
        out_specs=pl.BlockSpec((ROW_TILE, n), lambda i: (i, 0)),
        compiler_params=_params("parallel"),
        name="in_proj",
    )(h, gain.reshape(1, d), w)


def _rwkv_kernel(p_ref, mu_ref, w0_ref, wup_ref, a0_ref, aup_ref, gup_ref, kk_ref, ka_ref, rk_ref,
                 gng_ref, gnb_ref, hs_ref, o_ref, prev_ref, s_ref):
    @pl.when(pl.program_id(1) == 0)
    def _():
        prev_ref[...] = jnp.zeros_like(prev_ref)
        s_ref[...] = jnp.zeros_like(s_ref)

    L = RW_CHUNK
    lower = _tril(L)
    strict = _tril(L, strict=True)
    tril_f = lower.astype(F32)
    row = lax.broadcasted_iota(jnp.int32, (L, 1), 0)
    steps = int(math.log2(L))
    mmp = RW_PASSES
    seqs = range(p_ref.shape[0])

    def head_sum(x):
        return _mm(x, hs_ref[...], _NN, mmp["head_sum"])

    r, k2, v, gate, g_last, r_rows, a_rows, b_rows, k_rows, v_b = ([] for _ in range(10))
    for g in seqs:
        p = p_ref[g]
        prev = jnp.where(row == 0, prev_ref[g], pltpu.roll(p, 1, 0))
        prev_ref[g] = p[L - 1:L, :]
        ps = p + (prev - p) * mu_ref[...]
        k = ps[:, BRANCH_W:2 * BRANCH_W]
        d = w0_ref[...] + _dot(jnp.tanh(ps[:, 1536:1600]), wup_ref[...])
        log_w = -math.exp(-0.5) * _sigmoid(d)
        a = _sigmoid(a0_ref[...] + _dot(ps[:, 1600:1664], aup_ref[...]))
        gate.append(_dot(_sigmoid(ps[:, 1664:1792]), gup_ref[...]))
        r.append(ps[:, 0:BRANCH_W])
        v.append(ps[:, 2 * BRANCH_W:3 * BRANCH_W])
        kappa = k * kk_ref[...]
        kh = kappa / jnp.maximum(jnp.sqrt(head_sum(kappa * kappa)), 1e-12)
        k2.append(k * (1.0 + (a - 1.0) * ka_ref[...]))
        cum = _dot_hi(tril_f, log_w)
        g_incl = jnp.exp(cum)
        g_inv = jnp.exp(-cum)
        g_last.append(g_incl[L - 1:L, :])
        r_rows.append((r[g] * g_incl).astype(BF16))
        a_rows.append((-kh * jnp.exp(cum - log_w)).astype(BF16))
        b_rows.append((a * kh * g_inv).astype(BF16))
        k_rows.append((k2[g] * g_inv).astype(BF16))
        v_b.append(v[g].astype(BF16))

    units = [(g, h) for g in seqs for h in range(RW_HEADS)]
    idx = range(len(units))
    sl = [slice(h * RW_HEAD, (h + 1) * RW_HEAD) for _, h in units]
    lhs = [jnp.concatenate([r_rows[g][:, sl[i]], a_rows[g][:, sl[i]]], axis=0) for i, (g, h) in enumerate(units)]
    b_u = [b_rows[g][:, sl[i]] for i, (g, h) in enumerate(units)]
    k_u = [k_rows[g][:, sl[i]] for i, (g, h) in enumerate(units)]
    v_u = [v_b[g][:, sl[i]] for i, (g, h) in enumerate(units)]
    s0 = [s_ref[g * RW_HEADS + h] for g, h in units]
    gram = [_mm(lhs[i], jnp.concatenate([b_u[i], k_u[i]], axis=0), _NT, mmp["gram"]) for i in idx]
    ls0 = [_mm(lhs[i], s0[i], _NT, mmp["carry"]) for i in idx]
    u = [ls0[i][L:] + _mm(jnp.where(strict, gram[i][L:, L:], 0.0), v_u[i], _NN, mmp["gram"]) for i in idx]
    n_pow = [jnp.where(strict, gram[i][L:, :L], 0.0) for i in idx]
    for step in range(steps):
        u = [u[i] + _mm(n_pow[i], u[i], _NN, mmp["solve"]) for i in idx]
        if step + 1 < steps:
            n_pow = [_mm(n_pow[i], n_pow[i], _NN, mmp["solve"]) for i in idx]

    y_u = [ls0[i][:L] + _mm(jnp.where(lower, gram[i][:L, :L], 0.0), u[i], _NN, mmp["out"])
           + _mm(jnp.where(lower, gram[i][:L, L:], 0.0), v_u[i], _NN, mmp["out"]) for i in idx]
    for i, (g, h) in enumerate(units):
        s_ref[g * RW_HEADS + h] = (s0[i] + _mm(u[i], b_u[i], _TN, mmp["state"])
                                   + _mm(v_u[i], k_u[i], _TN, mmp["state"])) * g_last[g][:, sl[i]]

    for g in seqs:
        y = jnp.concatenate(y_u[g * RW_HEADS:(g + 1) * RW_HEADS], axis=1)
        dev = y - head_sum(y) * (1.0 / RW_HEAD)
        var = head_sum(dev * dev) * (1.0 / RW_HEAD)
        y = dev * lax.rsqrt(var + RW_GN_EPS) * gng_ref[...] + gnb_ref[...]
        bonus = head_sum(r[g] * k2[g] * rk_ref[...])
        o_ref[g] = (y + bonus * v[g]) * gate[g]


def _rwkv(p, mu, w0, w_up, a0, a_up, g_up, k_k, k_a, r_k, gn_gain, gn_bias):
    b, t, c = p.shape
    group = math.gcd(b, RW_SEQ_GROUP)
    head_of = jnp.arange(BRANCH_W) // RW_HEAD
    same_head = (head_of[:, None] == head_of[None, :]).astype(BF16)
    row = lambda x: x.reshape(1, -1)
    vec = lambda n: pl.BlockSpec((1, n), lambda i, j: (0, 0))
    mat = lambda shp: pl.BlockSpec(shp, lambda i, j: (0, 0))
    return pl.pallas_call(
        _rwkv_kernel,
        out_shape=jax.ShapeDtypeStruct((b, t, BRANCH_W), F32),
        grid=(b // group, t // RW_CHUNK),
        in_specs=[
            pl.BlockSpec((group, RW_CHUNK, c), lambda i, j: (i, j, 0)),
            vec(c), vec(BRANCH_W), mat(w_up.shape), vec(BRANCH_W), mat(a_up.shape), mat(g_up.shape),
            vec(BRANCH_W), vec(BRANCH_W), vec(BRANCH_W), vec(BRANCH_W), vec(BRANCH_W),
            mat((BRANCH_W, BRANCH_W)),
        ],
        out_specs=pl.BlockSpec((group, RW_CHUNK, BRANCH_W), lambda i, j: (i, j, 0)),
        scratch_shapes=[pltpu.VMEM((group, 1, c), F32),
                        pltpu.VMEM((group * RW_HEADS, RW_HEAD, RW_HEAD), F32)],
        compiler_params=_params("parallel", "arbitrary"),
        name="rwkv7",
    )(p, row(mu), row(w0), w_up, row(a0), a_up, g_up, row(k_k), row(k_a), row(r_k), row(gn_gain),
      row(gn_bias), same_head)


def _log_sigmoid(x):
    return jnp.minimum(x, 0.0) - jnp.log1p(jnp.exp(-jnp.abs(x)))


def _mlstm_kernel(p_ref, cw_ref, cb_ref, gb_ref, ng_ref, o_ref, tail_ref, c_ref, n_ref, m_ref):
    @pl.when(pl.program_id(1) == 0)
    def _():
        tail_ref[...] = jnp.zeros_like(tail_ref)
        c_ref[...] = jnp.zeros_like(c_ref)
        n_ref[...] = jnp.zeros_like(n_ref)
        m_ref[...] = jnp.zeros_like(m_ref)

    L = ML_CHUNK
    qk_w = 2 * BRANCH_W
    lower = _tril(L)
    tril_f = lower.astype(F32)
    seqs = range(p_ref.shape[0])

    x = [p_ref[g] for g in seqs]
    q, k, gates, b_all, b_t, g_t = [], [], [], [], [], []
    for g in seqs:
        qk_in = x[g][:, :qk_w]
        ext = jnp.concatenate([tail_ref[g], qk_in], axis=0)
        tail_ref[g] = qk_in[L - 8:L, :]
        conv = cb_ref[...] + cw_ref[ML_CONV - 1:ML_CONV, :] * qk_in
        for j in range(ML_CONV - 1):
            conv = conv + cw_ref[j:j + 1, :] * pltpu.roll(ext, ML_CONV - 1 - j, 0)[8:8 + L]
        qk = conv * _sigmoid(conv)
        q.append(qk[:, :BRANCH_W])
        k.append(qk[:, BRANCH_W:] * ML_HEAD ** -0.5)
        gates.append(x[g][:, qk_w + 2 * BRANCH_W:] + gb_ref[...])
        b_all.append(_dot_hi(tril_f, _log_sigmoid(gates[g])))
        b_t.append(b_all[g].T)
        g_t.append(gates[g].T)

    units = [(g, h) for g in seqs for h in range(ML_HEADS)]
    sl = [slice(h * ML_HEAD, (h + 1) * ML_HEAD) for _, h in units]
    q_u = [q[g][:, sl[i]] for i, (g, h) in enumerate(units)]
    k_u = [k[g][:, sl[i]] for i, (g, h) in enumerate(units)]
    v_u = [x[g][:, qk_w + h * ML_HEAD:qk_w + (h + 1) * ML_HEAD] for g, h in units]
    b_col = [b_all[g][:, ML_HEADS + h:ML_HEADS + h + 1] for g, h in units]
    b_row = [b_t[g][ML_HEADS + h:ML_HEADS + h + 1, :] for g, h in units]
    i_col = [gates[g][:, h:h + 1] for g, h in units]
    i_row = [g_t[g][h:h + 1, :] for g, h in units]
    m_st = [m_ref[g, h:h + 1, 0:1] for g, h in units]
    n_st = [n_ref[g, h:h + 1, :] for g, h in units]
    c_st = [c_ref[g * ML_HEADS + h] for g, h in units]
    idx = range(len(units))

    log_d = [jnp.where(lower, b_col[i] - b_row[i] + i_row[i], -jnp.inf) for i in idx]
    m_j = [jnp.maximum(b_col[i] + m_st[i], jnp.max(log_d[i], axis=-1, keepdims=True)) for i in idx]
    s = [_dot_nt(q_u[i], k_u[i]) * jnp.exp(log_d[i] - m_j[i]) for i in idx]
    inter = [jnp.exp(b_col[i] + m_st[i] - m_j[i]) for i in idx]
    num = [_dot(s[i], v_u[i]) + inter[i] * _dot_nt(q_u[i], c_st[i]) for i in idx]
    den = [jnp.sum(s[i], axis=-1, keepdims=True)
           + inter[i] * jnp.sum(q_u[i] * n_st[i], axis=-1, keepdims=True) for i in idx]
    hh = [num[i] / jnp.maximum(jnp.abs(den[i]), jnp.exp(-m_j[i])) for i in idx]

    g_tot = [b_col[i][L - 1:L, :] for i in idx]
    w_log = [g_tot[i] - b_col[i] + i_col[i] for i in idx]
    m_new = [jnp.maximum(g_tot[i] + m_st[i], jnp.max(w_log[i], axis=0, keepdims=True)) for i in idx]
    wgt = [jnp.exp(w_log[i] - m_new[i]) for i in idx]
    dec = [jnp.exp(g_tot[i] + m_st[i] - m_new[i]) for i in idx]
    c_new = [dec[i] * c_st[i] + _dot_tn(wgt[i] * v_u[i], k_u[i]) for i in idx]
    n_new = [dec[i] * n_st[i] + jnp.sum(wgt[i] * k_u[i], axis=0, keepdims=True) for i in idx]
    hn = [hh[i] * lax.rsqrt(jnp.mean(hh[i] * hh[i], axis=-1, keepdims=True) + NORM_EPS) * ng_ref[:, sl[i]]
          for i in idx]

    for i, (g, h) in enumerate(units):
        c_ref[g * ML_HEADS + h] = c_new[i]
        n_ref[g, h:h + 1, :] = n_new[i]
        m_ref[g, h:h + 1, :] = jnp.broadcast_to(m_new[i], (1, ML_HEAD))
        o_gate = x[g][:, qk_w + BRANCH_W + h * ML_HEAD:qk_w + BRANCH_W + (h + 1) * ML_HEAD]
        o_ref[g, :, sl[i]] = _sigmoid(o_gate) * hn[i]


def _mlstm(p, conv_w, conv_b, gate_bias, norm_gain):
    b, t, c = p.shape
    group = math.gcd(b, ML_SEQ_GROUP)
    return pl.pallas_call(
        _mlstm_kernel,
        out_shape=jax.ShapeDtypeStruct((b, t, BRANCH_W), F32),
        grid=(b // group, t // ML_CHUNK),
        in_specs=[
            pl.BlockSpec((group, ML_CHUNK, c), lambda i, j: (i, j, 0)),
            pl.BlockSpec(conv_w.shape, lambda i, j: (0, 0)),
            pl.BlockSpec((1, 2 * BRANCH_W), lambda i, j: (0, 0)),
            pl.BlockSpec((1, 128), lambda i, j: (0, 0)),
            pl.BlockSpec((1, BRANCH_W), lambda i, j: (0, 0)),
        ],
        out_specs=pl.BlockSpec((group, ML_CHUNK, BRANCH_W), lambda i, j: (i, j, 0)),
        scratch_shapes=[
            pltpu.VMEM((group, 8, 2 * BRANCH_W), F32),
            pltpu.VMEM((group * ML_HEADS, ML_HEAD, ML_HEAD), F32),
            pltpu.VMEM((group, 8, ML_HEAD), F32),
            pltpu.VMEM((group, 8, ML_HEAD), F32),
        ],
        compiler_params=_params("parallel", "arbitrary"),
        name="mlstm",
    )(p, conv_w, conv_b.reshape(1, -1), gate_bias, norm_gain.reshape(1, -1))


def _sum_sublane_groups(x):
    n = x.shape[0] // 32
    g = x.reshape(4 * n, 8, x.shape[1])
    parts = [g[i * n:(i + 1) * n] for i in range(4)]
    return (jnp.sum(parts[0], axis=0) + jnp.sum(parts[1], axis=0)) + (jnp.sum(parts[2], axis=0) + jnp.sum(parts[3], axis=0))


def _dsa_kernel(pq_ref, pk_ref, kvn_ref, wukt_ref, wuv_ref, o_ref, ckv_ref, ckvt_ref, kidx_ref, sc_ref,
                *, top_k):
    qb = pl.program_id(1)
    tq = DSA_QBLOCK
    kc = DSA_KEY_CHUNK
    tk = pk_ref.shape[1]
    n_chunks_max = ckvt_ref.shape[0]
    hq = DSA_HEADS * tq

    @pl.when(qb == 0)
    def _():
        keys = pk_ref[0]
        ckv = _rms(keys[:, :DSA_LATENT], kvn_ref[...])
        pad_rows = n_chunks_max * kc - tk
        ckv_ref[0:tk, :] = ckv.astype(BF16)
        ckv_ref[tk:, :] = jnp.zeros((pad_rows, DSA_LATENT), BF16)
        kidx_ref[0:tk, :] = keys[:, DSA_LATENT:DSA_LATENT + IDX_HEAD].astype(BF16)
        kidx_ref[tk:, :] = jnp.zeros((pad_rows, IDX_HEAD), BF16)
        ckv_t = jnp.concatenate([ckv.T, jnp.zeros((DSA_LATENT, pad_rows), F32)], axis=1)
        extra = jnp.where(lax.broadcasted_iota(jnp.int32, (DSA_SUM_ROWS, kc), 0) == 0, 1.0, 0.0)
        for c in range(n_chunks_max):
            ckvt_ref[c] = jnp.concatenate([ckv_t[:, c * kc:(c + 1) * kc], extra], axis=0).astype(BF16)

    n_chunks = jnp.minimum(lax.shift_right_logical(qb * tq + tq + N_META + kc - 1, int(math.log2(kc))),
                           n_chunks_max)
    q_pos = qb * tq + lax.broadcasted_iota(jnp.int32, (1, tq), 1)
    q_chunk = jnp.where(q_pos < N_META, 0,
                        1 + lax.shift_right_arithmetic(q_pos - N_META, int(math.log2(STREAM_CHUNK))))
    n_allowed = jnp.minimum(N_META + STREAM_CHUNK * q_chunk, tk)

    def rows_of(c):
        return pl.ds(pl.multiple_of(c * kc, kc), kc)

    pq = pq_ref[0]
    q = pq[:, :BRANCH_W]
    qi_t = jnp.concatenate([pq[:, BRANCH_W + j * 128:BRANCH_W + (j + 1) * 128].T
                            for j in range(IDX_HEADS * IDX_HEAD // 128)], axis=0)
    qi_t = jnp.concatenate([qi_t[h * IDX_HEAD:(h + 1) * IDX_HEAD] for h in range(IDX_HEADS)], axis=1)
    tail_t = pq[:, DSA_COLS_PAD - 128:].T
    w_t = tail_t[IDX_HEAD:IDX_HEAD + IDX_HEADS] * (IDX_HEADS * IDX_HEAD) ** -0.5
    w_row = jnp.concatenate([w_t[h:h + 1] for h in range(IDX_HEADS)], axis=1)
    qi_t = qi_t.astype(BF16)

    def score_chunk(c, bounds):
        lo, hi = bounds
        s = jnp.maximum(jnp.dot(kidx_ref[rows_of(c), :], qi_t, preferred_element_type=F32), 0.0) * w_row
        score = s[:, 0:tq]
        for h in range(1, IDX_HEADS):
            score = score + s[:, h * tq:(h + 1) * tq]
        k_pos = c * kc + lax.broadcasted_iota(jnp.int32, (kc, 1), 0)
        allowed = k_pos < n_allowed
        sc_ref[rows_of(c), :] = jnp.where(allowed, score, -jnp.inf)
        lo = jnp.minimum(lo, jnp.min(jnp.where(allowed, score, jnp.inf), axis=0, keepdims=True))
        hi = jnp.maximum(hi, jnp.max(jnp.where(allowed, score, -jnp.inf), axis=0, keepdims=True))
        return lo, hi

    lo, row_max = lax.fori_loop(0, n_chunks, score_chunk,
                                (jnp.full((1, tq), jnp.inf, F32), jnp.full((1, tq), -jnp.inf, F32)))
    hi = row_max + jnp.maximum(jnp.abs(row_max) * 2.0 ** -20, 1e-30)

    def count(pred, level):
        def body(c, acc):
            return acc + _sum_sublane_groups(jnp.where(pred(sc_ref[rows_of(c), :], level), 1.0, 0.0))
        return jnp.sum(lax.fori_loop(0, n_chunks, body, jnp.zeros((8, tq), F32)), axis=0, keepdims=True)

    def bisect(_, bracket):
        lo, hi = bracket
        mid = 0.5 * lo + 0.5 * hi
        ge = count(jnp.greater_equal, mid) >= top_k
        return jnp.where(ge, mid, lo), jnp.where(ge, hi, mid)

    def snap(lo):
        def body(c, val):
            s = sc_ref[rows_of(c), :]
            return jnp.minimum(val, jnp.min(jnp.where(s >= lo, s, jnp.inf), axis=0, keepdims=True))
        val = lax.fori_loop(0, n_chunks, body, jnp.full((1, tq), jnp.inf, F32))
        return val, count(jnp.greater, val)

    lo, hi = lax.fori_loop(0, DSA_BISECT_STEPS, bisect, (lo, hi))
    thr, n_above = snap(lo)

    def unsettled(state):
        return jnp.logical_and(jnp.max(state[3]) >= top_k, state[4] < DSA_BISECT_MAX_ROUNDS)

    def refine(state):
        lo, hi = lax.fori_loop(0, DSA_BISECT_REFINE, bisect, (state[0], state[1]))
        thr, n_above = snap(lo)
        return lo, hi, thr, n_above, state[4] + 1

    _, _, thr, n_above, _ = lax.while_loop(unsettled, refine, (lo, hi, thr, n_above, jnp.int32(0)))

    @pl.when(jnp.max(count(jnp.greater_equal, thr)) > top_k)
    def _():
        need = top_k - n_above
        earlier = (lax.broadcasted_iota(jnp.int32, (kc, kc), 1)
                   < lax.broadcasted_iota(jnp.int32, (kc, kc), 0)).astype(BF16)

        def body(c, seen):
            s = sc_ref[rows_of(c), :]
            tie = jnp.where(s == thr, 1.0, 0.0)
            rank = seen + jnp.dot(earlier, tie.astype(BF16), preferred_element_type=F32)
            sc_ref[rows_of(c), :] = jnp.where((tie > 0.0) & (rank >= need), -jnp.inf, s)
            return seen + jnp.sum(tie, axis=0, keepdims=True)

        lax.fori_loop(0, n_chunks, body, jnp.zeros((1, tq), F32))

    q_t = jnp.concatenate([_mm(wukt_ref[h], q[:, h * DSA_HEAD:(h + 1) * DSA_HEAD], _NT, 1)
                           for h in range(DSA_HEADS)], axis=1)
    q_t = (q_t * (DSA_HEAD ** -0.5 * math.log2(math.e))).astype(BF16)

    def attend(c, carry):
        m, acc = carry
        bias = jnp.where(sc_ref[rows_of(c), :] >= thr, 0.0, -jnp.inf)
        logits = (jnp.dot(ckv_ref[rows_of(c), :], q_t, preferred_element_type=F32)
                  + jnp.concatenate([bias] * DSA_HEADS, axis=1))
        m_new = jnp.maximum(m, jnp.max(logits, axis=0, keepdims=True))
        m_safe = jnp.where(m_new == -jnp.inf, 0.0, m_new)
        e = jnp.exp2(logits - m_safe)
        acc = acc * jnp.exp2(m - m_safe) + jnp.dot(ckvt_ref[c], e.astype(BF16), preferred_element_type=F32)
        return m_new, acc

    _, acc = lax.fori_loop(0, n_chunks, attend,
                           (jnp.full((1, hq), -jnp.inf, F32), jnp.zeros((DSA_LATENT + DSA_SUM_ROWS, hq), F32)))
    out_t = acc[:DSA_LATENT] / acc[DSA_LATENT:DSA_LATENT + 1]
    for h in range(DSA_HEADS):
        o_ref[0, :, h * DSA_HEAD:(h + 1) * DSA_HEAD] = _mm(out_t[:, h * tq:(h + 1) * tq], wuv_ref[h], _TN, 1)


def _dsa(p, kv_norm, w_uk_t, w_uv, top_k):
    b, t, c = p.shape
    n_chunks = -(-t // DSA_KEY_CHUNK)
    return pl.pallas_call(
        functools.partial(_dsa_kernel, top_k=top_k),
        out_shape=jax.ShapeDtypeStruct((b, t, BRANCH_W), F32),
        grid=(b, t // DSA_QBLOCK),
        in_specs=[
            pl.BlockSpec((1, DSA_QBLOCK, c), lambda i, j: (i, j, 0)),
            pl.BlockSpec((1, t, DSA_KEY_COL_BLOCK), lambda i, j: (i, 0, c // DSA_KEY_COL_BLOCK - 1)),
            pl.BlockSpec((1, DSA_LATENT), lambda i, j: (0, 0)),
            pl.BlockSpec(w_uk_t.shape, lambda i, j: (0, 0, 0)),
            pl.BlockSpec(w_uv.shape, lambda i, j: (0, 0, 0)),
        ],
        out_specs=pl.BlockSpec((1, DSA_QBLOCK, BRANCH_W), lambda i, j: (i, j, 0)),
        scratch_shapes=[
            pltpu.VMEM((n_chunks * DSA_KEY_CHUNK, DSA_LATENT), BF16),
            pltpu.VMEM((n_chunks, DSA_LATENT + DSA_SUM_ROWS, DSA_KEY_CHUNK), BF16),
            pltpu.VMEM((n_chunks * DSA_KEY_CHUNK, IDX_HEAD), BF16),
            pltpu.VMEM((n_chunks * DSA_KEY_CHUNK, DSA_QBLOCK), F32),
        ],
        compiler_params=_params("parallel", "arbitrary"),
        name="dsa",
    )(p, p, kv_norm.reshape(1, -1), w_uk_t, w_uv)


def _merge_kernel(h_ref, yr_ref, ym_ref, yd_ref, pg_ref, wb_ref, wo_ref, g_ref, o_ref):
    mixed = None
    for i, y_ref in enumerate((yr_ref, ym_ref, yd_ref)):
        proj = jnp.dot(y_ref[...].astype(BF16), wb_ref[i], preferred_element_type=F32)
        term = _sigmoid(pg_ref[:, i * D_MODEL:(i + 1) * D_MODEL]) * proj
        mixed = term if mixed is None else mixed + term
    out = jnp.dot(mixed.astype(BF16), wo_ref[...], preferred_element_type=F32)
    o_ref[...] = h_ref[...] + _rms(out, g_ref[...])


def _merge(h, y_rw, y_ml, y_dsa, p_gate, w_branch, w_out, gain):
    m, d = h.shape
    rows = lambda n: pl.BlockSpec((ROW_TILE, n), lambda i: (i, 0))
    return pl.pallas_call(
        _merge_kernel,
        out_shape=jax.ShapeDtypeStruct((m, d), F32),
        grid=(m // ROW_TILE,),
        in_specs=[
            rows(d), rows(BRANCH_W), rows(BRANCH_W), rows(BRANCH_W), rows(3 * d),
            pl.BlockSpec(w_branch.shape, lambda i: (0, 0, 0)),
            pl.BlockSpec(w_out.shape, lambda i: (0, 0)),
            pl.BlockSpec((1, d), lambda i: (0, 0)),
        ],
        out_specs=rows(d),
        compiler_params=_params("parallel"),
        name="merge",
    )(h, y_rw, y_ml, y_dsa, p_gate, w_branch, w_out, gain.reshape(1, d))


def _split_w_in(w):
    rw_end = RW_COLS
    ml_end = rw_end + 4 * BRANCH_W + 2 * ML_HEADS
    dsa_end = ml_end + BRANCH_W + DSA_LATENT + IDX_HEADS * IDX_HEAD + IDX_HEAD + IDX_HEADS
    w_rw, w_ml, w_dsa, w_gate = w[:, :rw_end], w[:, rw_end:ml_end], w[:, ml_end:dsa_end], w[:, dsa_end:]
    w_ml = jnp.pad(w_ml, ((0, 0), (0, ML_COLS_PAD - w_ml.shape[1])))
    q, c_kv, rest = w_dsa[:, :BRANCH_W], w_dsa[:, BRANCH_W:BRANCH_W + DSA_LATENT], w_dsa[:, BRANCH_W + DSA_LATENT:]
    q_idx, tail = rest[:, :IDX_HEADS * IDX_HEAD], rest[:, IDX_HEADS * IDX_HEAD:]
    w_dsa = jnp.concatenate([q, q_idx, c_kv, tail], axis=1)
    w_dsa = jnp.pad(w_dsa, ((0, 0), (0, DSA_COLS_PAD - w_dsa.shape[1])))
    return tuple(x.astype(BF16) for x in (w_rw, w_ml, w_dsa, w_gate))


def kernel(x, meta_tokens, norm_gain, ffn_w_in, ffn_w_out, w_in, rw_mu, rw_w0, rw_w_up, rw_a0, rw_a_up, rw_g_up, rw_k_k, rw_k_a, rw_r_k, rw_gn_gain, rw_gn_bias, ml_conv_w, ml_conv_b, ml_i_bias, ml_f_bias, ml_norm_gain, dsa_kv_norm, dsa_w_uk, dsa_w_uv, w_branch, w_out):
    bsz, seq, d = x.shape
    depth = norm_gain.shape[0]
    top_k = min(TOPK_MAX, seq // 4)
    t_len = seq + N_META
    t_pad = -(-t_len // SEQ_PAD_MULTIPLE) * SEQ_PAD_MULTIPLE
    h = jnp.concatenate([
        jnp.broadcast_to(meta_tokens.astype(x.dtype)[None], (bsz, N_META, d)), x,
        jnp.zeros((bsz, t_pad - t_len, d), x.dtype)], axis=1).reshape(bsz * t_pad, d)

    for l in range(depth):
        g = norm_gain[l]
        h = _ffn(h, g[0], g[1], ffn_w_in[l, 0].astype(BF16), ffn_w_out[l, 0].astype(BF16))

        w_rw, w_ml, w_dsa, w_gate = _split_w_in(w_in[l])
        p_rw = _proj(h, g[2], w_rw).reshape(bsz, t_pad, -1)
        p_ml = _proj(h, g[2], w_ml).reshape(bsz, t_pad, -1)
        p_dsa = _proj(h, g[2], w_dsa).reshape(bsz, t_pad, -1)
        p_gate = _proj(h, g[2], w_gate)

        y_rw = _rwkv(p_rw, rw_mu[l], rw_w0[l], rw_w_up[l], rw_a0[l], rw_a_up[l], rw_g_up[l], rw_k_k[l],
                     rw_k_a[l], rw_r_k[l], rw_gn_gain[l], rw_gn_bias[l])
        gate_bias = jnp.pad(jnp.concatenate([ml_i_bias[l], ml_f_bias[l]]), (0, 128 - 2 * ML_HEADS))
        y_ml = _mlstm(p_ml, ml_conv_w[l], ml_conv_b[l], gate_bias.reshape(1, 128), ml_norm_gain[l])
        y_dsa = _dsa(p_dsa, dsa_kv_norm[l], jnp.swapaxes(dsa_w_uk[l], 1, 2).astype(BF16), dsa_w_uv[l].astype(BF16), top_k)

        flat = lambda y: y.reshape(bsz * t_pad, BRANCH_W)
        h = _merge(h, flat(y_rw), flat(y_ml), flat(y_dsa), p_gate, w_branch[l].astype(BF16),
                   w_out[l].astype(BF16), g[3])
        h = _ffn(h, g[4], g[5], ffn_w_in[l, 1].astype(BF16), ffn_w_out[l, 1].astype(BF16))

    return h.reshape(bsz, t_pad, d)[:, N_META:t_len]
```

```python
import functools
import math

import jax
import jax.numpy as jnp
from jax import lax
from jax.experimental import pallas as pl
from jax.experimental.pallas import tpu as pltpu

F32 = jnp.float32
BF16 = jnp.bfloat16
HI = lax.Precision.HIGHEST

D_MODEL = 1024
D_FF = 2816
N_META = 16
STREAM_CHUNK = 64
BRANCH_W = 512
NORM_EPS = 1e-6

RW_HEADS, RW_HEAD = 8, 64
RW_COLS = 1792
RW_GN_EPS = 64e-5
RW_CHUNK = 64

ML_HEADS, ML_HEAD = 4, 128
ML_CHUNK = 64
ML_CONV = 4
ML_COLS_PAD = 2176

DSA_HEADS, DSA_HEAD, DSA_LATENT = 8, 64, 128
IDX_HEADS, IDX_HEAD = 8, 64
TOPK_MAX = 256
DSA_QBLOCK = 128
DSA_COLS_PAD = 1280
DSA_KEY_COL_BLOCK = 256

ROW_TILE = 512
FF_TILE = 1408
SEQ_PAD_MULTIPLE = 128
RW_SEQ_GROUP = 2
ML_SEQ_GROUP = 4
VMEM_LIMIT = 56 * 1024 * 1024
DSA_KEY_CHUNK = 512
DSA_SUM_ROWS = 16
DSA_BISECT_STEPS = 16
DSA_BISECT_REFINE = 4
DSA_BISECT_EXTRA_ROUNDS = 2
DSA_BISECT_MAX_ROUNDS = 80


def _dot(a, b):
    return jnp.dot(a.astype(BF16), b.astype(BF16), preferred_element_type=F32)


def _dot_nt(a, b):
    return lax.dot_general(a.astype(BF16), b.astype(BF16), (((1,), (1,)), ((), ())),
                           preferred_element_type=F32)


def _dot_tn(a, b):
    return lax.dot_general(a.astype(BF16), b.astype(BF16), (((0,), (0,)), ((), ())),
                           preferred_element_type=F32)


def _dot_hi(a, b):
    return jnp.dot(a, b, preferred_element_type=F32, precision=HI)


def _dot_nt_hi(a, b):
    return lax.dot_general(a, b, (((1,), (1,)), ((), ())), preferred_element_type=F32, precision=HI)


def _dot_tn_hi(a, b):
    return lax.dot_general(a, b, (((0,), (0,)), ((), ())), preferred_element_type=F32, precision=HI)


_NN = ((1,), (0,))
_NT = ((1,), (1,))
_TN = ((0,), (0,))


def _mm(a, b, contract, passes):
    dn = (contract, ((), ()))
    if passes == 6:
        return lax.dot_general(a, b, dn, preferred_element_type=F32, precision=HI)
    dot = lambda x, y: lax.dot_general(x, y, dn, preferred_element_type=F32)
    a_hi, b_hi = a.astype(BF16), b.astype(BF16)
    if passes == 1:
        return dot(a_hi, b_hi)
    a_lo = (a - a_hi.astype(F32)).astype(BF16)
    b_lo = (b - b_hi.astype(F32)).astype(BF16)
    return dot(a_hi, b_hi) + (dot(a_hi, b_lo) + dot(a_lo, b_hi))


RW_PASSES = {"gram": 1, "carry": 1, "solve": 1, "out": 1, "state": 1, "head_sum": 1}


def _rms(x, gain):
    return x * lax.rsqrt(jnp.mean(x * x, axis=-1, keepdims=True) + NORM_EPS) * gain


def _sigmoid(x):
    return jax.nn.sigmoid(x)


def _tril(n, strict=False):
    r = lax.broadcasted_iota(jnp.int32, (n, n), 0)
    c = lax.broadcasted_iota(jnp.int32, (n, n), 1)
    return (c < r) if strict else (c <= r)


def _params(*sem):
    return pltpu.CompilerParams(dimension_semantics=sem, vmem_limit_bytes=VMEM_LIMIT)


def _ffn_kernel(h_ref, gpre_ref, gpost_ref, wg_ref, wu_ref, wo_ref, o_ref, xn_ref, acc_ref):
    j = pl.program_id(1)

    @pl.when(j == 0)
    def _():
        xn_ref[...] = _rms(h_ref[...], gpre_ref[...]).astype(BF16)
        acc_ref[...] = jnp.zeros_like(acc_ref)

    xn = xn_ref[...]
    gate = jnp.dot(xn, wg_ref[...], preferred_element_type=F32)
    up = jnp.dot(xn, wu_ref[...], preferred_element_type=F32)
    act = gate * _sigmoid(gate) * up
    acc_ref[...] += jnp.dot(act.astype(BF16), wo_ref[...], preferred_element_type=F32)

    @pl.when(j == pl.num_programs(1) - 1)
    def _():
        o_ref[...] = h_ref[...] + 0.5 * _rms(acc_ref[...], gpost_ref[...])


def _ffn(h, g_pre, g_post, w_in, w_out):
    m, d = h.shape
    nj = D_FF // FF_TILE
    return pl.pallas_call(
        _ffn_kernel,
        out_shape=jax.ShapeDtypeStruct((m, d), F32),
        grid=(m // ROW_TILE, nj),
        in_specs=[
            pl.BlockSpec((ROW_TILE, d), lambda i, j: (i, 0)),
            pl.BlockSpec((1, d), lambda i, j: (0, 0)),
            pl.BlockSpec((1, d), lambda i, j: (0, 0)),
            pl.BlockSpec((d, FF_TILE), lambda i, j: (0, j)),
            pl.BlockSpec((d, FF_TILE), lambda i, j: (0, j + nj)),
            pl.BlockSpec((FF_TILE, d), lambda i, j: (j, 0)),
        ],
        out_specs=pl.BlockSpec((ROW_TILE, d), lambda i, j: (i, 0)),
        scratch_shapes=[pltpu.VMEM((ROW_TILE, d), BF16), pltpu.VMEM((ROW_TILE, d), F32)],
        compiler_params=_params("parallel", "arbitrary"),
        name="ffn",
    )(h, g_pre.reshape(1, d), g_post.reshape(1, d), w_in, w_in, w_out)


def _proj_kernel(h_ref, g_ref, w_ref, o_ref):
    o_ref[...] = jnp.dot(_rms(h_ref[...], g_ref[...]).astype(BF16), w_ref[...],
                         preferred_element_type=F32)


def _proj(h, gain, w):
    m, d = h.shape
    n = w.shape[1]
    return pl.pallas_call(
        _proj_kernel,
        out_shape=jax.ShapeDtypeStruct((m, n), F32),
        grid=(m // ROW_TILE,),
        in_specs=[
            pl.BlockSpec((ROW_TILE, d), lambda i: (i, 0)),
            pl.BlockSpec((1, d), lambda i: (0, 0)),
            pl.BlockSpec((d, n), lambda i: (0, 0)),
        ],
        out_specs=pl.BlockSpec((ROW_TILE, n), lambda i: (i, 0)),
        compiler_params=_params("parallel"),
        name="in_proj",
    )(h, gain.reshape(1, d), w)


def _rwkv_kernel(p_ref, mu_ref, w0_ref, wup_ref, a0_ref, aup_ref, gup_ref, kk_ref, ka_ref, rk_ref,
                 gng_ref, gnb_ref, hs_ref, o_ref, prev_ref, s_ref):
    @pl.when(pl.program_id(1) == 0)
    def _():
        prev_ref[...] = jnp.zeros_like(prev_ref)
        s_ref[...] = jnp.zeros_like(s_ref)

    L = RW_CHUNK
    lower = _tril(L)
    strict = _tril(L, strict=True)
    tril_f = lower.astype(F32)
    row = lax.broadcasted_iota(jnp.int32, (L, 1), 0)
    steps = int(math.log2(L))
    mmp = RW_PASSES
    seqs = range(p_ref.shape[0])

    def head_sum(x):
        return _mm(x, hs_ref[...], _NN, mmp["head_sum"])

    r, k2, v, gate, g_last, r_rows, a_rows, b_rows, k_rows, v_b = ([] for _ in range(10))
    for g in seqs:
        p = p_ref[g]
        prev = jnp.where(row == 0, prev_ref[g], pltpu.roll(p, 1, 0))
        prev_ref[g] = p[L - 1:L, :]
        ps = p + (prev - p) * mu_ref[...]
        k = ps[:, BRANCH_W:2 * BRANCH_W]
        d = w0_ref[...] + _dot(jnp.tanh(ps[:, 1536:1600]), wup_ref[...])
        log_w = -math.exp(-0.5) * _sigmoid(d)
        a = _sigmoid(a0_ref[...] + _dot(ps[:, 1600:1664], aup_ref[...]))
        gate.append(_dot(_sigmoid(ps[:, 1664:1792]), gup_ref[...]))
        r.append(ps[:, 0:BRANCH_W])
        v.append(ps[:, 2 * BRANCH_W:3 * BRANCH_W])
        kappa = k * kk_ref[...]
        kh = kappa / jnp.maximum(jnp.sqrt(head_sum(kappa * kappa)), 1e-12)
        k2.append(k * (1.0 + (a - 1.0) * ka_ref[...]))
        cum = _dot_hi(tril_f, log_w)
        g_incl = jnp.exp(cum)
        g_inv = jnp.exp(-cum)
        g_last.append(g_incl[L - 1:L, :])
        r_rows.append((r[g] * g_incl).astype(BF16))
        a_rows.append((-kh * jnp.exp(cum - log_w)).astype(BF16))
        b_rows.append((a * kh * g_inv).astype(BF16))
        k_rows.append((k2[g] * g_inv).astype(BF16))
        v_b.append(v[g].astype(BF16))

    units = [(g, h) for g in seqs for h in range(RW_HEADS)]
    idx = range(len(units))
    sl = [slice(h * RW_HEAD, (h + 1) * RW_HEAD) for _, h in units]
    lhs = [jnp.concatenate([r_rows[g][:, sl[i]], a_rows[g][:, sl[i]]], axis=0) for i, (g, h) in enumerate(units)]
    b_u = [b_rows[g][:, sl[i]] for i, (g, h) in enumerate(units)]
    k_u = [k_rows[g][:, sl[i]] for i, (g, h) in enumerate(units)]
    v_u = [v_b[g][:, sl[i]] for i, (g, h) in enumerate(units)]
    s0 = [s_ref[g * RW_HEADS + h] for g, h in units]
    gram = [_mm(lhs[i], jnp.concatenate([b_u[i], k_u[i]], axis=0), _NT, mmp["gram"]) for i in idx]
    ls0 = [_mm(lhs[i], s0[i], _NT, mmp["carry"]) for i in idx]
    u = [ls0[i][L:] + _mm(jnp.where(strict, gram[i][L:, L:], 0.0), v_u[i], _NN, mmp["gram"]) for i in idx]
    n_pow = [jnp.where(strict, gram[i][L:, :L], 0.0) for i in idx]
    for step in range(steps):
        u = [u[i] + _mm(n_pow[i], u[i], _NN, mmp["solve"]) for i in idx]
        if step + 1 < steps:
            n_pow = [_mm(n_pow[i], n_pow[i], _NN, mmp["solve"]) for i in idx]

    y_u = [ls0[i][:L] + _mm(jnp.where(lower, gram[i][:L, :L], 0.0), u[i], _NN, mmp["out"])
           + _mm(jnp.where(lower, gram[i][:L, L:], 0.0), v_u[i], _NN, mmp["out"]) for i in idx]
    for i, (g, h) in enumerate(units):
        s_ref[g * RW_HEADS + h] = (s0[i] + _mm(u[i], b_u[i], _TN, mmp["state"])
                                   + _mm(v_u[i], k_u[i], _TN, mmp["state"])) * g_last[g][:, sl[i]]

    for g in seqs:
        y = jnp.concatenate(y_u[g * RW_HEADS:(g + 1) * RW_HEADS], axis=1)
        dev = y - head_sum(y) * (1.0 / RW_HEAD)
        var = head_sum(dev * dev) * (1.0 / RW_HEAD)
        y = dev * lax.rsqrt(var + RW_GN_EPS) * gng_ref[...] + gnb_ref[...]
        bonus = head_sum(r[g] * k2[g] * rk_ref[...])
        o_ref[g] = (y + bonus * v[g]) * gate[g]


def _rwkv(p, mu, w0, w_up, a0, a_up, g_up, k_k, k_a, r_k, gn_gain, gn_bias):
    b, t, c = p.shape
    group = math.gcd(b, RW_SEQ_GROUP)
    head_of = jnp.arange(BRANCH_W) // RW_HEAD
    same_head = (head_of[:, None] == head_of[None, :]).astype(BF16)
    row = lambda x: x.reshape(1, -1)
    vec = lambda n: pl.BlockSpec((1, n), lambda i, j: (0, 0))
    mat = lambda shp: pl.BlockSpec(shp, lambda i, j: (0, 0))
    return pl.pallas_call(
        _rwkv_kernel,
        out_shape=jax.ShapeDtypeStruct((b, t, BRANCH_W), F32),
        grid=(b // group, t // RW_CHUNK),
        in_specs=[
            pl.BlockSpec((group, RW_CHUNK, c), lambda i, j: (i, j, 0)),
            vec(c), vec(BRANCH_W), mat(w_up.shape), vec(BRANCH_W), mat(a_up.shape), mat(g_up.shape),
            vec(BRANCH_W), vec(BRANCH_W), vec(BRANCH_W), vec(BRANCH_W), vec(BRANCH_W),
            mat((BRANCH_W, BRANCH_W)),
        ],
        out_specs=pl.BlockSpec((group, RW_CHUNK, BRANCH_W), lambda i, j: (i, j, 0)),
        scratch_shapes=[pltpu.VMEM((group, 1, c), F32),
                        pltpu.VMEM((group * RW_HEADS, RW_HEAD, RW_HEAD), F32)],
        compiler_params=_params("parallel", "arbitrary"),
        name="rwkv7",
    )(p, row(mu), row(w0), w_up, row(a0), a_up, g_up, row(k_k), row(k_a), row(r_k), row(gn_gain),
      row(gn_bias), same_head)


def _log_sigmoid(x):
    return jnp.minimum(x, 0.0) - jnp.log1p(jnp.exp(-jnp.abs(x)))


def _mlstm_kernel(p_ref, cw_ref, cb_ref, gb_ref, ng_ref, o_ref, tail_ref, c_ref, n_ref, m_ref):
    @pl.when(pl.program_id(1) == 0)
    def _():
        tail_ref[...] = jnp.zeros_like(tail_ref)
        c_ref[...] = jnp.zeros_like(c_ref)
        n_ref[...] = jnp.zeros_like(n_ref)
        m_ref[...] = jnp.zeros_like(m_ref)

    L = ML_CHUNK
    qk_w = 2 * BRANCH_W
    lower = _tril(L)
    tril_f = lower.astype(F32)
    seqs = range(p_ref.shape[0])

    x = [p_ref[g] for g in seqs]
    q, k, gates, b_all, b_t, g_t = [], [], [], [], [], []
    for g in seqs:
        qk_in = x[g][:, :qk_w]
        ext = jnp.concatenate([tail_ref[g], qk_in], axis=0)
        tail_ref[g] = qk_in[L - 8:L, :]
        conv = cb_ref[...] + cw_ref[ML_CONV - 1:ML_CONV, :] * qk_in
        for j in range(ML_CONV - 1):
            conv = conv + cw_ref[j:j + 1, :] * pltpu.roll(ext, ML_CONV - 1 - j, 0)[8:8 + L]
        qk = conv * _sigmoid(conv)
        q.append(qk[:, :BRANCH_W])
        k.append(qk[:, BRANCH_W:] * ML_HEAD ** -0.5)
        gates.append(x[g][:, qk_w + 2 * BRANCH_W:] + gb_ref[...])
        b_all.append(_dot_hi(tril_f, _log_sigmoid(gates[g])))
        b_t.append(b_all[g].T)
        g_t.append(gates[g].T)

    units = [(g, h) for g in seqs for h in range(ML_HEADS)]
    sl = [slice(h * ML_HEAD, (h + 1) * ML_HEAD) for _, h in units]
    q_u = [q[g][:, sl[i]] for i, (g, h) in enumerate(units)]
    k_u = [k[g][:, sl[i]] for i, (g, h) in enumerate(units)]
    v_u = [x[g][:, qk_w + h * ML_HEAD:qk_w + (h + 1) * ML_HEAD] for g, h in units]
    b_col = [b_all[g][:, ML_HEADS + h:ML_HEADS + h + 1] for g, h in units]
    b_row = [b_t[g][ML_HEADS + h:ML_HEADS + h + 1, :] for g, h in units]
    i_col = [gates[g][:, h:h + 1] for g, h in units]
    i_row = [g_t[g][h:h + 1, :] for g, h in units]
    m_st = [m_ref[g, h:h + 1, 0:1] for g, h in units]
    n_st = [n_ref[g, h:h + 1, :] for g, h in units]
    c_st = [c_ref[g * ML_HEADS + h] for g, h in units]
    idx = range(len(units))

    log_d = [jnp.where(lower, b_col[i] - b_row[i] + i_row[i], -jnp.inf) for i in idx]
    m_j = [jnp.maximum(b_col[i] + m_st[i], jnp.max(log_d[i], axis=-1, keepdims=True)) for i in idx]
    s = [_dot_nt(q_u[i], k_u[i]) * jnp.exp(log_d[i] - m_j[i]) for i in idx]
    inter = [jnp.exp(b_col[i] + m_st[i] - m_j[i]) for i in idx]
    num = [_dot(s[i], v_u[i]) + inter[i] * _dot_nt(q_u[i], c_st[i]) for i in idx]
    den = [jnp.sum(s[i], axis=-1, keepdims=True)
           + inter[i] * jnp.sum(q_u[i] * n_st[i], axis=-1, keepdims=True) for i in idx]
    hh = [num[i] / jnp.maximum(jnp.abs(den[i]), jnp.exp(-m_j[i])) for i in idx]

    g_tot = [b_col[i][L - 1:L, :] for i in idx]
    w_log = [g_tot[i] - b_col[i] + i_col[i] for i in idx]
    m_new = [jnp.maximum(g_tot[i] + m_st[i], jnp.max(w_log[i], axis=0, keepdims=True)) for i in idx]
    wgt = [jnp.exp(w_log[i] - m_new[i]) for i in idx]
    dec = [jnp.exp(g_tot[i] + m_st[i] - m_new[i]) for i in idx]
    c_new = [dec[i] * c_st[i] + _dot_tn(wgt[i] * v_u[i], k_u[i]) for i in idx]
    n_new = [dec[i] * n_st[i] + jnp.sum(wgt[i] * k_u[i], axis=0, keepdims=True) for i in idx]
    hn = [hh[i] * lax.rsqrt(jnp.mean(hh[i] * hh[i], axis=-1, keepdims=True) + NORM_EPS) * ng_ref[:, sl[i]]
          for i in idx]

    for i, (g, h) in enumerate(units):
        c_ref[g * ML_HEADS + h] = c_new[i]
        n_ref[g, h:h + 1, :] = n_new[i]
        m_ref[g, h:h + 1, :] = jnp.broadcast_to(m_new[i], (1, ML_HEAD))
        o_gate = x[g][:, qk_w + BRANCH_W + h * ML_HEAD:qk_w + BRANCH_W + (h + 1) * ML_HEAD]
        o_ref[g, :, sl[i]] = _sigmoid(o_gate) * hn[i]


def _mlstm(p, conv_w, conv_b, gate_bias, norm_gain):
    b, t, c = p.shape
    group = math.gcd(b, ML_SEQ_GROUP)
    return pl.pallas_call(
        _mlstm_kernel,
        out_shape=jax.ShapeDtypeStruct((b, t, BRANCH_W), F32),
        grid=(b // group, t // ML_CHUNK),
        in_specs=[
            pl.BlockSpec((group, ML_CHUNK, c), lambda i, j: (i, j, 0)),
            pl.BlockSpec(conv_w.shape, lambda i, j: (0, 0)),
            pl.BlockSpec((1, 2 * BRANCH_W), lambda i, j: (0, 0)),
            pl.BlockSpec((1, 128), lambda i, j: (0, 0)),
            pl.BlockSpec((1, BRANCH_W), lambda i, j: (0, 0)),
        ],
        out_specs=pl.BlockSpec((group, ML_CHUNK, BRANCH_W), lambda i, j: (i, j, 0)),
        scratch_shapes=[
            pltpu.VMEM((group, 8, 2 * BRANCH_W), F32),
            pltpu.VMEM((group * ML_HEADS, ML_HEAD, ML_HEAD), F32),
            pltpu.VMEM((group, 8, ML_HEAD), F32),
            pltpu.VMEM((group, 8, ML_HEAD), F32),
        ],
        compiler_params=_params("parallel", "arbitrary"),
        name="mlstm",
    )(p, conv_w, conv_b.reshape(1, -1), gate_bias, norm_gain.reshape(1, -1))


def _sum_sublane_groups(x):
    n = x.shape[0] // 32
    g = x.reshape(4 * n, 8, x.shape[1])
    parts = [g[i * n:(i + 1) * n] for i in range(4)]
    return (jnp.sum(parts[0], axis=0) + jnp.sum(parts[1], axis=0)) + (jnp.sum(parts[2], axis=0) + jnp.sum(parts[3], axis=0))


def _dsa_kernel(pq_ref, pk_ref, kvn_ref, wukt_ref, wuv_ref, o_ref, ckv_ref, ckvt_ref, kidx_ref, sc_ref,
                *, top_k):
    qb = pl.program_id(1)
    tq = DSA_QBLOCK
    kc = DSA_KEY_CHUNK
    tk = pk_ref.shape[1]
    n_chunks_max = ckvt_ref.shape[0]
    hq = DSA_HEADS * tq

    @pl.when(qb == 0)
    def _():
        keys = pk_ref[0]
        ckv = _rms(keys[:, :DSA_LATENT], kvn_ref[...])
        pad_rows = n_chunks_max * kc - tk
        ckv_ref[0:tk, :] = ckv.astype(BF16)
        ckv_ref[tk:, :] = jnp.zeros((pad_rows, DSA_LATENT), BF16)
        kidx_ref[0:tk, :] = keys[:, DSA_LATENT:DSA_LATENT + IDX_HEAD].astype(BF16)
        kidx_ref[tk:, :] = jnp.zeros((pad_rows, IDX_HEAD), BF16)
        ckv_t = jnp.concatenate([ckv.T, jnp.zeros((DSA_LATENT, pad_rows), F32)], axis=1)
        extra = jnp.where(lax.broadcasted_iota(jnp.int32, (DSA_SUM_ROWS, kc), 0) == 0, 1.0, 0.0)
        for c in range(n_chunks_max):
            ckvt_ref[c] = jnp.concatenate([ckv_t[:, c * kc:(c + 1) * kc], extra], axis=0).astype(BF16)

    n_chunks = jnp.minimum(lax.shift_right_logical(qb * tq + tq + N_META + kc - 1, int(math.log2(kc))),
                           n_chunks_max)
    q_pos = qb * tq + lax.broadcasted_iota(jnp.int32, (1, tq), 1)
    q_chunk = jnp.where(q_pos < N_META, 0,
                        1 + lax.shift_right_arithmetic(q_pos - N_META, int(math.log2(STREAM_CHUNK))))
    n_allowed = jnp.minimum(N_META + STREAM_CHUNK * q_chunk, tk)

    def rows_of(c):
        return pl.ds(pl.multiple_of(c * kc, kc), kc)

    pq = pq_ref[0]
    q = pq[:, :BRANCH_W]
    qi_t = jnp.concatenate([pq[:, BRANCH_W + j * 128:BRANCH_W + (j + 1) * 128].T
                            for j in range(IDX_HEADS * IDX_HEAD // 128)], axis=0)
    qi_t = jnp.concatenate([qi_t[h * IDX_HEAD:(h + 1) * IDX_HEAD] for h in range(IDX_HEADS)], axis=1)
    tail_t = pq[:, DSA_COLS_PAD - 128:].T
    w_t = tail_t[IDX_HEAD:IDX_HEAD + IDX_HEADS] * (IDX_HEADS * IDX_HEAD) ** -0.5
    w_row = jnp.concatenate([w_t[h:h + 1] for h in range(IDX_HEADS)], axis=1)
    qi_t = qi_t.astype(BF16)

    def score_chunk(c, bounds):
        lo, hi = bounds
        s = jnp.maximum(jnp.dot(kidx_ref[rows_of(c), :], qi_t, preferred_element_type=F32), 0.0) * w_row
        score = s[:, 0:tq]
        for h in range(1, IDX_HEADS):
            score = score + s[:, h * tq:(h + 1) * tq]
        k_pos = c * kc + lax.broadcasted_iota(jnp.int32, (kc, 1), 0)
        allowed = k_pos < n_allowed
        sc_ref[rows_of(c), :] = jnp.where(allowed, score, -jnp.inf)
        lo = jnp.minimum(lo, jnp.min(jnp.where(allowed, score, jnp.inf), axis=0, keepdims=True))
        hi = jnp.maximum(hi, jnp.max(jnp.where(allowed, score, -jnp.inf), axis=0, keepdims=True))
        return lo, hi

    lo, row_max = lax.fori_loop(0, n_chunks, score_chunk,
                                (jnp.full((1, tq), jnp.inf, F32), jnp.full((1, tq), -jnp.inf, F32)))
    hi = row_max + jnp.maximum(jnp.abs(row_max) * 2.0 ** -20, 1e-30)

    def count(pred, level):
        def body(c, acc):
            return acc + _sum_sublane_groups(jnp.where(pred(sc_ref[rows_of(c), :], level), 1.0, 0.0))
        return jnp.sum(lax.fori_loop(0, n_chunks, body, jnp.zeros((8, tq), F32)), axis=0, keepdims=True)

    def bisect(_, bracket):
        lo, hi, n_lo = bracket
        mid = 0.5 * lo + 0.5 * hi
        n_mid = count(jnp.greater_equal, mid)
        ge = n_mid >= top_k
        return jnp.where(ge, mid, lo), jnp.where(ge, hi, mid), jnp.where(ge, n_mid, n_lo)

    def snap(lo):
        def body(c, val):
            s = sc_ref[rows_of(c), :]
            return jnp.minimum(val, jnp.min(jnp.where(s >= lo, s, jnp.inf), axis=0, keepdims=True))
        val = lax.fori_loop(0, n_chunks, body, jnp.full((1, tq), jnp.inf, F32))
        return val, count(jnp.greater, val)

    bracket = (lo, hi, n_allowed.astype(F32))
    bracket = lax.fori_loop(0, DSA_BISECT_STEPS, bisect, bracket)

    def keeps_too_many(state):
        return jnp.logical_and(jnp.max(state[0][2]) > top_k, state[1] < DSA_BISECT_EXTRA_ROUNDS)

    def halve_more(state):
        return lax.fori_loop(0, DSA_BISECT_REFINE, bisect, state[0]), state[1] + 1

    bracket, _ = lax.while_loop(keeps_too_many, halve_more, (bracket, jnp.int32(0)))
    thr, n_above = snap(bracket[0])

    def unsettled(state):
        return jnp.logical_and(jnp.max(state[2]) >= top_k, state[3] < DSA_BISECT_MAX_ROUNDS)

    def refine(state):
        bracket = lax.fori_loop(0, DSA_BISECT_REFINE, bisect, state[0])
        thr, n_above = snap(bracket[0])
        return bracket, thr, n_above, state[3] + 1

    bracket, thr, n_above, _ = lax.while_loop(unsettled, refine, (bracket, thr, n_above, jnp.int32(0)))

    @pl.when(jnp.max(bracket[2]) > top_k)
    def _():
        need = top_k - n_above
        earlier = (lax.broadcasted_iota(jnp.int32, (kc, kc), 1)
                   < lax.broadcasted_iota(jnp.int32, (kc, kc), 0)).astype(BF16)

        def body(c, seen):
            s = sc_ref[rows_of(c), :]
            tie = jnp.where(s == thr, 1.0, 0.0)
            rank = seen + jnp.dot(earlier, tie.astype(BF16), preferred_element_type=F32)
            sc_ref[rows_of(c), :] = jnp.where((tie > 0.0) & (rank >= need), -jnp.inf, s)
            return seen + jnp.sum(tie, axis=0, keepdims=True)

        lax.fori_loop(0, n_chunks, body, jnp.zeros((1, tq), F32))

    q_t = jnp.concatenate([_mm(wukt_ref[h], q[:, h * DSA_HEAD:(h + 1) * DSA_HEAD], _NT, 1)
                           for h in range(DSA_HEADS)], axis=1)
    q_t = (q_t * (DSA_HEAD ** -0.5 * math.log2(math.e))).astype(BF16)

    def finite_or_zero(m):
        return jnp.where(m == -jnp.inf, 0.0, m)

    def attend(chunks, m, acc):
        n = range(len(chunks))
        logits = [jnp.dot(ckv_ref[rows_of(c), :], q_t, preferred_element_type=F32)
                  + jnp.concatenate([jnp.where(sc_ref[rows_of(c), :] >= thr, 0.0, -jnp.inf)] * DSA_HEADS, axis=1)
                  for c in chunks]
        m_new = [jnp.maximum(m[s], jnp.max(logits[s], axis=0, keepdims=True)) for s in n]
        m_safe = [finite_or_zero(m_new[s]) for s in n]
        e = [jnp.exp2(logits[s] - m_safe[s]).astype(BF16) for s in n]
        acc = [acc[s] * jnp.exp2(m[s] - m_safe[s])
               + jnp.dot(ckvt_ref[chunks[s]], e[s], preferred_element_type=F32) for s in n]
        return m_new, acc

    def attend_pair(i, carry):
        m, acc = attend([2 * i, 2 * i + 1], carry[:2], carry[2:])
        return m[0], m[1], acc[0], acc[1]

    def attend_last(_, carry):
        m, acc = attend([n_chunks - 1], carry[:1], carry[2:3])
        return m[0], carry[1], acc[0], carry[3]

    m_init = jnp.full((1, hq), -jnp.inf, F32)
    acc_init = jnp.zeros((DSA_LATENT + DSA_SUM_ROWS, hq), F32)
    streams = lax.fori_loop(0, lax.shift_right_logical(n_chunks, 1), attend_pair,
                            (m_init, m_init, acc_init, acc_init))
    m0, m1, acc0, acc1 = lax.fori_loop(0, n_chunks & 1, attend_last, streams)
    m_safe = finite_or_zero(jnp.maximum(m0, m1))
    acc = acc0 * jnp.exp2(m0 - m_safe) + acc1 * jnp.exp2(m1 - m_safe)
    out_t = acc[:DSA_LATENT] / acc[DSA_LATENT:DSA_LATENT + 1]
    for h in range(DSA_HEADS):
        o_ref[0, :, h * DSA_HEAD:(h + 1) * DSA_HEAD] = _mm(out_t[:, h * tq:(h + 1) * tq], wuv_ref[h], _TN, 1)


def _dsa(p, kv_norm, w_uk_t, w_uv, top_k):
    b, t, c = p.shape
    n_chunks = -(-t // DSA_KEY_CHUNK)
    return pl.pallas_call(
        functools.partial(_dsa_kernel, top_k=top_k),
        out_shape=jax.ShapeDtypeStruct((b, t, BRANCH_W), F32),
        grid=(b, t // DSA_QBLOCK),
        in_specs=[
            pl.BlockSpec((1, DSA_QBLOCK, c), lambda i, j: (i, j, 0)),
            pl.BlockSpec((1, t, DSA_KEY_COL_BLOCK), lambda i, j: (i, 0, c // DSA_KEY_COL_BLOCK - 1)),
            pl.BlockSpec((1, DSA_LATENT), lambda i, j: (0, 0)),
            pl.BlockSpec(w_uk_t.shape, lambda i, j: (0, 0, 0)),
            pl.BlockSpec(w_uv.shape, lambda i, j: (0, 0, 0)),
        ],
        out_specs=pl.BlockSpec((1, DSA_QBLOCK, BRANCH_W), lambda i, j: (i, j, 0)),
        scratch_shapes=[
            pltpu.VMEM((n_chunks * DSA_KEY_CHUNK, DSA_LATENT), BF16),
            pltpu.VMEM((n_chunks, DSA_LATENT + DSA_SUM_ROWS, DSA_KEY_CHUNK), BF16),
            pltpu.VMEM((n_chunks * DSA_KEY_CHUNK, IDX_HEAD), BF16),
            pltpu.VMEM((n_chunks * DSA_KEY_CHUNK, DSA_QBLOCK), F32),
        ],
        compiler_params=_params("parallel", "arbitrary"),
        name="dsa",
    )(p, p, kv_norm.reshape(1, -1), w_uk_t, w_uv)


def _merge_kernel(h_ref, yr_ref, ym_ref, yd_ref, pg_ref, wb_ref, wo_ref, g_ref, o_ref):
    mixed = None
    for i, y_ref in enumerate((yr_ref, ym_ref, yd_ref)):
        proj = jnp.dot(y_ref[...].astype(BF16), wb_ref[i], preferred_element_type=F32)
        term = _sigmoid(pg_ref[:, i * D_MODEL:(i + 1) * D_MODEL]) * proj
        mixed = term if mixed is None else mixed + term
    out = jnp.dot(mixed.astype(BF16), wo_ref[...], preferred_element_type=F32)
    o_ref[...] = h_ref[...] + _rms(out, g_ref[...])


def _merge(h, y_rw, y_ml, y_dsa, p_gate, w_branch, w_out, gain):
    m, d = h.shape
    rows = lambda n: pl.BlockSpec((ROW_TILE, n), lambda i: (i, 0))
    return pl.pallas_call(
        _merge_kernel,
        out_shape=jax.ShapeDtypeStruct((m, d), F32),
        grid=(m // ROW_TILE,),
        in_specs=[
            rows(d), rows(BRANCH_W), rows(BRANCH_W), rows(BRANCH_W), rows(3 * d),
            pl.BlockSpec(w_branch.shape, lambda i: (0, 0, 0)),
            pl.BlockSpec(w_out.shape, lambda i: (0, 0)),
            pl.BlockSpec((1, d), lambda i: (0, 0)),
        ],
        out_specs=rows(d),
        compiler_params=_params("parallel"),
        name="merge",
    )(h, y_rw, y_ml, y_dsa, p_gate, w_branch, w_out, gain.reshape(1, d))


def _split_w_in(w):
    rw_end = RW_COLS
    ml_end = rw_end + 4 * BRANCH_W + 2 * ML_HEADS
    dsa_end = ml_end + BRANCH_W + DSA_LATENT + IDX_HEADS * IDX_HEAD + IDX_HEAD + IDX_HEADS
    w_rw, w_ml, w_dsa, w_gate = w[:, :rw_end], w[:, rw_end:ml_end], w[:, ml_end:dsa_end], w[:, dsa_end:]
    w_ml = jnp.pad(w_ml, ((0, 0), (0, ML_COLS_PAD - w_ml.shape[1])))
    q, c_kv, rest = w_dsa[:, :BRANCH_W], w_dsa[:, BRANCH_W:BRANCH_W + DSA_LATENT], w_dsa[:, BRANCH_W + DSA_LATENT:]
    q_idx, tail = rest[:, :IDX_HEADS * IDX_HEAD], rest[:, IDX_HEADS * IDX_HEAD:]
    w_dsa = jnp.concatenate([q, q_idx, c_kv, tail], axis=1)
    w_dsa = jnp.pad(w_dsa, ((0, 0), (0, DSA_COLS_PAD - w_dsa.shape[1])))
    return tuple(x.astype(BF16) for x in (w_rw, w_ml, w_dsa, w_gate))


def kernel(x, meta_tokens, norm_gain, ffn_w_in, ffn_w_out, w_in, rw_mu, rw_w0, rw_w_up, rw_a0, rw_a_up, rw_g_up, rw_k_k, rw_k_a, rw_r_k, rw_gn_gain, rw_gn_bias, ml_conv_w, ml_conv_b, ml_i_bias, ml_f_bias, ml_norm_gain, dsa_kv_norm, dsa_w_uk, dsa_w_uv, w_branch, w_out):
    bsz, seq, d = x.shape
    depth = norm_gain.shape[0]
    top_k = min(TOPK_MAX, seq // 4)
    t_len = seq + N_META
    t_pad = -(-t_len // SEQ_PAD_MULTIPLE) * SEQ_PAD_MULTIPLE
    h = jnp.concatenate([
        jnp.broadcast_to(meta_tokens.astype(x.dtype)[None], (bsz, N_META, d)), x,
        jnp.zeros((bsz, t_pad - t_len, d), x.dtype)], axis=1).reshape(bsz * t_pad, d)

    for l in range(depth):
        g = norm_gain[l]
        h = _ffn(h, g[0], g[1], ffn_w_in[l, 0].astype(BF16), ffn_w_out[l, 0].astype(BF16))

        w_rw, w_ml, w_dsa, w_gate = _split_w_in(w_in[l])
        p_rw = _proj(h, g[2], w_rw).reshape(bsz, t_pad, -1)
        p_ml = _proj(h, g[2], w_ml).reshape(bsz, t_pad, -1)
        p_dsa = _proj(h, g[2], w_dsa).reshape(bsz, t_pad, -1)
        p_gate = _proj(h, g[2], w_gate)

        y_rw = _rwkv(p_rw, rw_mu[l], rw_w0[l], rw_w_up[l], rw_a0[l], rw_a_up[l], rw_g_up[l], rw_k_k[l],
                     rw_k_a[l], rw_r_k[l], rw_gn_gain[l], rw_gn_bias[l])
        gate_bias = jnp.pad(jnp.concatenate([ml_i_bias[l], ml_f_bias[l]]), (0, 128 - 2 * ML_HEADS))
        y_ml = _mlstm(p_ml, ml_conv_w[l], ml_conv_b[l], gate_bias.reshape(1, 128), ml_norm_gain[l])
        y_dsa = _dsa(p_dsa, dsa_kv_norm[l], jnp.swapaxes(dsa_w_uk[l], 1, 2).astype(BF16), dsa_w_uv[l].astype(BF16), top_k)

        flat = lambda y: y.reshape(bsz * t_pad, BRANCH_W)
        h = _merge(h, flat(y_rw), flat(y_ml), flat(y_dsa), p_gate, w_branch[l].astype(BF16),
                   w_out[l].astype(BF16), g[3])
        h = _ffn(h, g[4], g[5], ffn_w_in[l, 1].astype(BF16), ffn_w_out[l, 1].astype(BF16))

    return h.reshape(bsz, t_pad, d)[:, N_META:t_len]
```

```python
import functools
import math

import jax
import jax.numpy as jnp
from jax import lax
from jax.experimental import pallas as pl
from jax.experimental.pallas import tpu as pltpu

F32 = jnp.float32
BF16 = jnp.bfloat16
HI = lax.Precision.HIGHEST

D_MODEL = 1024
D_FF = 2816
N_META = 16
STREAM_CHUNK = 64
BRANCH_W = 512
NORM_EPS = 1e-6

RW_HEADS, RW_HEAD = 8, 64
RW_COLS = 1792
RW_GN_EPS = 64e-5
RW_CHUNK = 64

ML_HEADS, ML_HEAD = 4, 128
ML_CHUNK = 64
ML_CONV = 4
ML_COLS_PAD = 2176

DSA_HEADS, DSA_HEAD, DSA_LATENT = 8, 64, 128
IDX_HEADS, IDX_HEAD = 8, 64
TOPK_MAX = 256
DSA_QBLOCK = 128
DSA_COLS_PAD = 1280
DSA_KEY_COL_BLOCK = 256

ROW_TILE = 512
FF_TILE = 1408
SEQ_PAD_MULTIPLE = 128
RW_SEQ_GROUP = 2
ML_SEQ_GROUP = 4
VMEM_LIMIT = 56 * 1024 * 1024
DSA_KEY_CHUNK = 512
DSA_SUM_ROWS = 16
DSA_BISECT_STEPS = 16
DSA_BISECT_REFINE = 4
DSA_BISECT_EXTRA_ROUNDS = 2
DSA_BISECT_MAX_ROUNDS = 80


def _dot(a, b):
    return jnp.dot(a.astype(BF16), b.astype(BF16), preferred_element_type=F32)


def _dot_nt(a, b):
    return lax.dot_general(a.astype(BF16), b.astype(BF16), (((1,), (1,)), ((), ())),
                           preferred_element_type=F32)


def _dot_tn(a, b):
    return lax.dot_general(a.astype(BF16), b.astype(BF16), (((0,), (0,)), ((), ())),
                           preferred_element_type=F32)


def _dot_hi(a, b):
    return jnp.dot(a, b, preferred_element_type=F32, precision=HI)


def _dot_nt_hi(a, b):
    return lax.dot_general(a, b, (((1,), (1,)), ((), ())), preferred_element_type=F32, precision=HI)


def _dot_tn_hi(a, b):
    return lax.dot_general(a, b, (((0,), (0,)), ((), ())), preferred_element_type=F32, precision=HI)


_NN = ((1,), (0,))
_NT = ((1,), (1,))
_TN = ((0,), (0,))


def _mm(a, b, contract, passes):
    dn = (contract, ((), ()))
    if passes == 6:
        return lax.dot_general(a, b, dn, preferred_element_type=F32, precision=HI)
    dot = lambda x, y: lax.dot_general(x, y, dn, preferred_element_type=F32)
    a_hi, b_hi = a.astype(BF16), b.astype(BF16)
    if passes == 1:
        return dot(a_hi, b_hi)
    a_lo = (a - a_hi.astype(F32)).astype(BF16)
    b_lo = (b - b_hi.astype(F32)).astype(BF16)
    return dot(a_hi, b_hi) + (dot(a_hi, b_lo) + dot(a_lo, b_hi))


RW_PASSES = {"gram": 1, "carry": 1, "solve": 1, "out": 1, "state": 1, "head_sum": 1}


def _rms(x, gain):
    return x * lax.rsqrt(jnp.mean(x * x, axis=-1, keepdims=True) + NORM_EPS) * gain


def _sigmoid(x):
    return jax.nn.sigmoid(x)


def _tril(n, strict=False):
    r = lax.broadcasted_iota(jnp.int32, (n, n), 0)
    c = lax.broadcasted_iota(jnp.int32, (n, n), 1)
    return (c < r) if strict else (c <= r)


def _params(*sem):
    return pltpu.CompilerParams(dimension_semantics=sem, vmem_limit_bytes=VMEM_LIMIT)


def _ffn_kernel(h_ref, gpre_ref, gpost_ref, wg_ref, wu_ref, wo_ref, o_ref, xn_ref, acc_ref):
    j = pl.program_id(1)

    @pl.when(j == 0)
    def _():
        xn_ref[...] = _rms(h_ref[...], gpre_ref[...]).astype(BF16)
        acc_ref[...] = jnp.zeros_like(acc_ref)

    xn = xn_ref[...]
    gate = jnp.dot(xn, wg_ref[...], preferred_element_type=F32)
    up = jnp.dot(xn, wu_ref[...], preferred_element_type=F32)
    act = gate * _sigmoid(gate) * up
    acc_ref[...] += jnp.dot(act.astype(BF16), wo_ref[...], preferred_element_type=F32)

    @pl.when(j == pl.num_programs(1) - 1)
    def _():
        o_ref[...] = h_ref[...] + 0.5 * _rms(acc_ref[...], gpost_ref[...])


def _ffn(h, g_pre, g_post, w_in, w_out):
    m, d = h.shape
    nj = D_FF // FF_TILE
    return pl.pallas_call(
        _ffn_kernel,
        out_shape=jax.ShapeDtypeStruct((m, d), F32),
        grid=(m // ROW_TILE, nj),
        in_specs=[
            pl.BlockSpec((ROW_TILE, d), lambda i, j: (i, 0)),
            pl.BlockSpec((1, d), lambda i, j: (0, 0)),
            pl.BlockSpec((1, d), lambda i, j: (0, 0)),
            pl.BlockSpec((d, FF_TILE), lambda i, j: (0, j)),
            pl.BlockSpec((d, FF_TILE), lambda i, j: (0, j + nj)),
            pl.BlockSpec((FF_TILE, d), lambda i, j: (j, 0)),
        ],
        out_specs=pl.BlockSpec((ROW_TILE, d), lambda i, j: (i, 0)),
        scratch_shapes=[pltpu.VMEM((ROW_TILE, d), BF16), pltpu.VMEM((ROW_TILE, d), F32)],
        compiler_params=_params("parallel", "arbitrary"),
        name="ffn",
    )(h, g_pre.reshape(1, d), g_post.reshape(1, d), w_in, w_in, w_out)


def _proj_kernel(h_ref, g_ref, *refs):
    w_refs, o_refs = refs[:len(refs) // 2], refs[len(refs) // 2:]
    xn = _rms(h_ref[...], g_ref[...]).astype(BF16)
    for w_ref, o_ref in zip(w_refs, o_refs):
        o_ref[...] = jnp.dot(xn, w_ref[...], preferred_element_type=F32)


def _proj(h, gain, weights):
    m, d = h.shape
    return pl.pallas_call(
        _proj_kernel,
        out_shape=[jax.ShapeDtypeStruct((m, w.shape[1]), F32) for w in weights],
        grid=(m // ROW_TILE,),
        in_specs=[
            pl.BlockSpec((ROW_TILE, d), lambda i: (i, 0)),
            pl.BlockSpec((1, d), lambda i: (0, 0)),
        ] + [pl.BlockSpec(w.shape, lambda i: (0, 0)) for w in weights],
        out_specs=[pl.BlockSpec((ROW_TILE, w.shape[1]), lambda i: (i, 0)) for w in weights],
        compiler_params=_params("parallel"),
        name="in_proj",
    )(h, gain.reshape(1, d), *weights)


def _rwkv_kernel(p_ref, mu_ref, w0_ref, wup_ref, a0_ref, aup_ref, gup_ref, kk_ref, ka_ref, rk_ref,
                 gng_ref, gnb_ref, hs_ref, o_ref, prev_ref, s_ref):
    @pl.when(pl.program_id(1) == 0)
    def _():
        prev_ref[...] = jnp.zeros_like(prev_ref)
        s_ref[...] = jnp.zeros_like(s_ref)

    L = RW_CHUNK
    lower = _tril(L)
    strict = _tril(L, strict=True)
    tril_f = lower.astype(F32)
    row = lax.broadcasted_iota(jnp.int32, (L, 1), 0)
    steps = int(math.log2(L))
    mmp = RW_PASSES
    seqs = range(p_ref.shape[0])

    def head_sum(x):
        return _mm(x, hs_ref[...], _NN, mmp["head_sum"])

    r, k2, v, gate, g_last, r_rows, a_rows, b_rows, k_rows, v_b = ([] for _ in range(10))
    for g in seqs:
        p = p_ref[g]
        prev = jnp.where(row == 0, prev_ref[g], pltpu.roll(p, 1, 0))
        prev_ref[g] = p[L - 1:L, :]
        ps = p + (prev - p) * mu_ref[...]
        k = ps[:, BRANCH_W:2 * BRANCH_W]
        d = w0_ref[...] + _dot(jnp.tanh(ps[:, 1536:1600]), wup_ref[...])
        log_w = -math.exp(-0.5) * _sigmoid(d)
        a = _sigmoid(a0_ref[...] + _dot(ps[:, 1600:1664], aup_ref[...]))
        gate.append(_dot(_sigmoid(ps[:, 1664:1792]), gup_ref[...]))
        r.append(ps[:, 0:BRANCH_W])
        v.append(ps[:, 2 * BRANCH_W:3 * BRANCH_W])
        kappa = k * kk_ref[...]
        kh = kappa / jnp.maximum(jnp.sqrt(head_sum(kappa * kappa)), 1e-12)
        k2.append(k * (1.0 + (a - 1.0) * ka_ref[...]))
        cum = _dot_hi(tril_f, log_w)
        g_incl = jnp.exp(cum)
        g_inv = jnp.exp(-cum)
        g_last.append(g_incl[L - 1:L, :])
        r_rows.append((r[g] * g_incl).astype(BF16))
        a_rows.append((-kh * jnp.exp(cum - log_w)).astype(BF16))
        b_rows.append((a * kh * g_inv).astype(BF16))
        k_rows.append((k2[g] * g_inv).astype(BF16))
        v_b.append(v[g].astype(BF16))

    units = [(g, h) for g in seqs for h in range(RW_HEADS)]
    idx = range(len(units))
    sl = [slice(h * RW_HEAD, (h + 1) * RW_HEAD) for _, h in units]
    lhs = [jnp.concatenate([r_rows[g][:, sl[i]], a_rows[g][:, sl[i]]], axis=0) for i, (g, h) in enumerate(units)]
    b_u = [b_rows[g][:, sl[i]] for i, (g, h) in enumerate(units)]
    k_u = [k_rows[g][:, sl[i]] for i, (g, h) in enumerate(units)]
    v_u = [v_b[g][:, sl[i]] for i, (g, h) in enumerate(units)]
    s0 = [s_ref[g * RW_HEADS + h] for g, h in units]
    gram = [_mm(lhs[i], jnp.concatenate([b_u[i], k_u[i]], axis=0), _NT, mmp["gram"]) for i in idx]
    ls0 = [_mm(lhs[i], s0[i], _NT, mmp["carry"]) for i in idx]
    u = [ls0[i][L:] + _mm(jnp.where(strict, gram[i][L:, L:], 0.0), v_u[i], _NN, mmp["gram"]) for i in idx]
    n_pow = [jnp.where(strict, gram[i][L:, :L], 0.0) for i in idx]
    for step in range(steps):
        u = [u[i] + _mm(n_pow[i], u[i], _NN, mmp["solve"]) for i in idx]
        if step + 1 < steps:
            n_pow = [_mm(n_pow[i], n_pow[i], _NN, mmp["solve"]) for i in idx]

    y_u = [ls0[i][:L] + _mm(jnp.where(lower, gram[i][:L, :L], 0.0), u[i], _NN, mmp["out"])
           + _mm(jnp.where(lower, gram[i][:L, L:], 0.0), v_u[i], _NN, mmp["out"]) for i in idx]
    for i, (g, h) in enumerate(units):
        s_ref[g * RW_HEADS + h] = (s0[i] + _mm(u[i], b_u[i], _TN, mmp["state"])
                                   + _mm(v_u[i], k_u[i], _TN, mmp["state"])) * g_last[g][:, sl[i]]

    for g in seqs:
        y = jnp.concatenate(y_u[g * RW_HEADS:(g + 1) * RW_HEADS], axis=1)
        dev = y - head_sum(y) * (1.0 / RW_HEAD)
        var = head_sum(dev * dev) * (1.0 / RW_HEAD)
        y = dev * lax.rsqrt(var + RW_GN_EPS) * gng_ref[...] + gnb_ref[...]
        bonus = head_sum(r[g] * k2[g] * rk_ref[...])
        o_ref[g] = (y + bonus * v[g]) * gate[g]


def _rwkv(p, mu, w0, w_up, a0, a_up, g_up, k_k, k_a, r_k, gn_gain, gn_bias):
    b, t, c = p.shape
    group = math.gcd(b, RW_SEQ_GROUP)
    head_of = jnp.arange(BRANCH_W) // RW_HEAD
    same_head = (head_of[:, None] == head_of[None, :]).astype(BF16)
    row = lambda x: x.reshape(1, -1)
    vec = lambda n: pl.BlockSpec((1, n), lambda i, j: (0, 0))
    mat = lambda shp: pl.BlockSpec(shp, lambda i, j: (0, 0))
    return pl.pallas_call(
        _rwkv_kernel,
        out_shape=jax.ShapeDtypeStruct((b, t, BRANCH_W), F32),
        grid=(b // group, t // RW_CHUNK),
        in_specs=[
            pl.BlockSpec((group, RW_CHUNK, c), lambda i, j: (i, j, 0)),
            vec(c), vec(BRANCH_W), mat(w_up.shape), vec(BRANCH_W), mat(a_up.shape), mat(g_up.shape),
            vec(BRANCH_W), vec(BRANCH_W), vec(BRANCH_W), vec(BRANCH_W), vec(BRANCH_W),
            mat((BRANCH_W, BRANCH_W)),
        ],
        out_specs=pl.BlockSpec((group, RW_CHUNK, BRANCH_W), lambda i, j: (i, j, 0)),
        scratch_shapes=[pltpu.VMEM((group, 1, c), F32),
                        pltpu.VMEM((group * RW_HEADS, RW_HEAD, RW_HEAD), F32)],
        compiler_params=_params("parallel", "arbitrary"),
        name="rwkv7",
    )(p, row(mu), row(w0), w_up, row(a0), a_up, g_up, row(k_k), row(k_a), row(r_k), row(gn_gain),
      row(gn_bias), same_head)


def _log_sigmoid(x):
    return jnp.minimum(x, 0.0) - jnp.log1p(jnp.exp(-jnp.abs(x)))


def _mlstm_kernel(p_ref, cw_ref, cb_ref, gb_ref, ng_ref, o_ref, tail_ref, c_ref, n_ref, m_ref):
    @pl.when(pl.program_id(1) == 0)
    def _():
        tail_ref[...] = jnp.zeros_like(tail_ref)
        c_ref[...] = jnp.zeros_like(c_ref)
        n_ref[...] = jnp.zeros_like(n_ref)
        m_ref[...] = jnp.zeros_like(m_ref)

    L = ML_CHUNK
    qk_w = 2 * BRANCH_W
    lower = _tril(L)
    tril_f = lower.astype(F32)
    seqs = range(p_ref.shape[0])

    x = [p_ref[g] for g in seqs]
    q, k, gates, b_all, b_t, g_t = [], [], [], [], [], []
    for g in seqs:
        qk_in = x[g][:, :qk_w]
        ext = jnp.concatenate([tail_ref[g], qk_in], axis=0)
        tail_ref[g] = qk_in[L - 8:L, :]
        conv = cb_ref[...] + cw_ref[ML_CONV - 1:ML_CONV, :] * qk_in
        for j in range(ML_CONV - 1):
            conv = conv + cw_ref[j:j + 1, :] * pltpu.roll(ext, ML_CONV - 1 - j, 0)[8:8 + L]
        qk = conv * _sigmoid(conv)
        q.append(qk[:, :BRANCH_W])
        k.append(qk[:, BRANCH_W:] * ML_HEAD ** -0.5)
        gates.append(x[g][:, qk_w + 2 * BRANCH_W:] + gb_ref[...])
        b_all.append(_dot_hi(tril_f, _log_sigmoid(gates[g])))
        b_t.append(b_all[g].T)
        g_t.append(gates[g].T)

    units = [(g, h) for g in seqs for h in range(ML_HEADS)]
    sl = [slice(h * ML_HEAD, (h + 1) * ML_HEAD) for _, h in units]
    q_u = [q[g][:, sl[i]] for i, (g, h) in enumerate(units)]
    k_u = [k[g][:, sl[i]] for i, (g, h) in enumerate(units)]
    v_u = [x[g][:, qk_w + h * ML_HEAD:qk_w + (h + 1) * ML_HEAD] for g, h in units]
    b_col = [b_all[g][:, ML_HEADS + h:ML_HEADS + h + 1] for g, h in units]
    b_row = [b_t[g][ML_HEADS + h:ML_HEADS + h + 1, :] for g, h in units]
    i_col = [gates[g][:, h:h + 1] for g, h in units]
    i_row = [g_t[g][h:h + 1, :] for g, h in units]
    m_st = [m_ref[g, h:h + 1, 0:1] for g, h in units]
    n_st = [n_ref[g, h:h + 1, :] for g, h in units]
    c_st = [c_ref[g * ML_HEADS + h] for g, h in units]
    idx = range(len(units))

    log_d = [jnp.where(lower, b_col[i] - b_row[i] + i_row[i], -jnp.inf) for i in idx]
    m_j = [jnp.maximum(b_col[i] + m_st[i], jnp.max(log_d[i], axis=-1, keepdims=True)) for i in idx]
    s = [_dot_nt(q_u[i], k_u[i]) * jnp.exp(log_d[i] - m_j[i]) for i in idx]
    inter = [jnp.exp(b_col[i] + m_st[i] - m_j[i]) for i in idx]
    num = [_dot(s[i], v_u[i]) + inter[i] * _dot_nt(q_u[i], c_st[i]) for i in idx]
    den = [jnp.sum(s[i], axis=-1, keepdims=True)
           + inter[i] * jnp.sum(q_u[i] * n_st[i], axis=-1, keepdims=True) for i in idx]
    hh = [num[i] / jnp.maximum(jnp.abs(den[i]), jnp.exp(-m_j[i])) for i in idx]

    g_tot = [b_col[i][L - 1:L, :] for i in idx]
    w_log = [g_tot[i] - b_col[i] + i_col[i] for i in idx]
    m_new = [jnp.maximum(g_tot[i] + m_st[i], jnp.max(w_log[i], axis=0, keepdims=True)) for i in idx]
    wgt = [jnp.exp(w_log[i] - m_new[i]) for i in idx]
    dec = [jnp.exp(g_tot[i] + m_st[i] - m_new[i]) for i in idx]
    c_new = [dec[i] * c_st[i] + _dot_tn(wgt[i] * v_u[i], k_u[i]) for i in idx]
    n_new = [dec[i] * n_st[i] + jnp.sum(wgt[i] * k_u[i], axis=0, keepdims=True) for i in idx]
    hn = [hh[i] * lax.rsqrt(jnp.mean(hh[i] * hh[i], axis=-1, keepdims=True) + NORM_EPS) * ng_ref[:, sl[i]]
          for i in idx]

    for i, (g, h) in enumerate(units):
        c_ref[g * ML_HEADS + h] = c_new[i]
        n_ref[g, h:h + 1, :] = n_new[i]
        m_ref[g, h:h + 1, :] = jnp.broadcast_to(m_new[i], (1, ML_HEAD))
        o_gate = x[g][:, qk_w + BRANCH_W + h * ML_HEAD:qk_w + BRANCH_W + (h + 1) * ML_HEAD]
        o_ref[g, :, sl[i]] = _sigmoid(o_gate) * hn[i]


def _mlstm(p, conv_w, conv_b, gate_bias, norm_gain):
    b, t, c = p.shape
    group = math.gcd(b, ML_SEQ_GROUP)
    return pl.pallas_call(
        _mlstm_kernel,
        out_shape=jax.ShapeDtypeStruct((b, t, BRANCH_W), F32),
        grid=(b // group, t // ML_CHUNK),
        in_specs=[
            pl.BlockSpec((group, ML_CHUNK, c), lambda i, j: (i, j, 0)),
            pl.BlockSpec(conv_w.shape, lambda i, j: (0, 0)),
            pl.BlockSpec((1, 2 * BRANCH_W), lambda i, j: (0, 0)),
            pl.BlockSpec((1, 128), lambda i, j: (0, 0)),
            pl.BlockSpec((1, BRANCH_W), lambda i, j: (0, 0)),
        ],
        out_specs=pl.BlockSpec((group, ML_CHUNK, BRANCH_W), lambda i, j: (i, j, 0)),
        scratch_shapes=[
            pltpu.VMEM((group, 8, 2 * BRANCH_W), F32),
            pltpu.VMEM((group * ML_HEADS, ML_HEAD, ML_HEAD), F32),
            pltpu.VMEM((group, 8, ML_HEAD), F32),
            pltpu.VMEM((group, 8, ML_HEAD), F32),
        ],
        compiler_params=_params("parallel", "arbitrary"),
        name="mlstm",
    )(p, conv_w, conv_b.reshape(1, -1), gate_bias, norm_gain.reshape(1, -1))


def _sum_sublane_groups(x):
    n = x.shape[0] // 32
    g = x.reshape(4 * n, 8, x.shape[1])
    parts = [g[i * n:(i + 1) * n] for i in range(4)]
    return (jnp.sum(parts[0], axis=0) + jnp.sum(parts[1], axis=0)) + (jnp.sum(parts[2], axis=0) + jnp.sum(parts[3], axis=0))


def _dsa_kernel(pq_ref, pk_ref, kvn_ref, wukt_ref, wuv_ref, o_ref, ckv_ref, ckvt_ref, kidx_ref, sc_ref,
                *, top_k):
    qb = pl.program_id(1)
    tq = DSA_QBLOCK
    kc = DSA_KEY_CHUNK
    tk = pk_ref.shape[1]
    n_chunks_max = ckvt_ref.shape[0]
    hq = DSA_HEADS * tq

    @pl.when(qb == 0)
    def _():
        keys = pk_ref[0]
        ckv = _rms(keys[:, :DSA_LATENT], kvn_ref[...])
        pad_rows = n_chunks_max * kc - tk
        ckv_ref[0:tk, :] = ckv.astype(BF16)
        ckv_ref[tk:, :] = jnp.zeros((pad_rows, DSA_LATENT), BF16)
        kidx_ref[0:tk, :] = keys[:, DSA_LATENT:DSA_LATENT + IDX_HEAD].astype(BF16)
        kidx_ref[tk:, :] = jnp.zeros((pad_rows, IDX_HEAD), BF16)
        ckv_t = jnp.concatenate([ckv.T, jnp.zeros((DSA_LATENT, pad_rows), F32)], axis=1)
        extra = jnp.where(lax.broadcasted_iota(jnp.int32, (DSA_SUM_ROWS, kc), 0) == 0, 1.0, 0.0)
        for c in range(n_chunks_max):
            ckvt_ref[c] = jnp.concatenate([ckv_t[:, c * kc:(c + 1) * kc], extra], axis=0).astype(BF16)

    n_chunks = jnp.minimum(lax.shift_right_logical(qb * tq + tq + N_META + kc - 1, int(math.log2(kc))),
                           n_chunks_max)
    q_pos = qb * tq + lax.broadcasted_iota(jnp.int32, (1, tq), 1)
    q_chunk = jnp.where(q_pos < N_META, 0,
                        1 + lax.shift_right_arithmetic(q_pos - N_META, int(math.log2(STREAM_CHUNK))))
    n_allowed = jnp.minimum(N_META + STREAM_CHUNK * q_chunk, tk)

    def rows_of(c):
        return pl.ds(pl.multiple_of(c * kc, kc), kc)

    pq = pq_ref[0]
    q = pq[:, :BRANCH_W]
    qi_t = jnp.concatenate([pq[:, BRANCH_W + j * 128:BRANCH_W + (j + 1) * 128].T
                            for j in range(IDX_HEADS * IDX_HEAD // 128)], axis=0)
    qi_t = jnp.concatenate([qi_t[h * IDX_HEAD:(h + 1) * IDX_HEAD] for h in range(IDX_HEADS)], axis=1)
    tail_t = pq[:, DSA_COLS_PAD - 128:].T
    w_t = tail_t[IDX_HEAD:IDX_HEAD + IDX_HEADS] * (IDX_HEADS * IDX_HEAD) ** -0.5
    w_row = jnp.concatenate([w_t[h:h + 1] for h in range(IDX_HEADS)], axis=1)
    qi_t = qi_t.astype(BF16)

    def score_chunk(c, bounds):
        lo, hi = bounds
        s = jnp.maximum(jnp.dot(kidx_ref[rows_of(c), :], qi_t, preferred_element_type=F32), 0.0) * w_row
        score = s[:, 0:tq]
        for h in range(1, IDX_HEADS):
            score = score + s[:, h * tq:(h + 1) * tq]
        k_pos = c * kc + lax.broadcasted_iota(jnp.int32, (kc, 1), 0)
        allowed = k_pos < n_allowed
        sc_ref[rows_of(c), :] = jnp.where(allowed, score, -jnp.inf)
        lo = jnp.minimum(lo, jnp.min(jnp.where(allowed, score, jnp.inf), axis=0, keepdims=True))
        hi = jnp.maximum(hi, jnp.max(jnp.where(allowed, score, -jnp.inf), axis=0, keepdims=True))
        return lo, hi

    lo, row_max = lax.fori_loop(0, n_chunks, score_chunk,
                                (jnp.full((1, tq), jnp.inf, F32), jnp.full((1, tq), -jnp.inf, F32)))
    hi = row_max + jnp.maximum(jnp.abs(row_max) * 2.0 ** -20, 1e-30)

    def count(pred, level):
        def body(c, acc):
            return acc + _sum_sublane_groups(jnp.where(pred(sc_ref[rows_of(c), :], level), 1.0, 0.0))
        return jnp.sum(lax.fori_loop(0, n_chunks, body, jnp.zeros((8, tq), F32)), axis=0, keepdims=True)

    def bisect(_, bracket):
        lo, hi, n_lo = bracket
        mid = 0.5 * lo + 0.5 * hi
        n_mid = count(jnp.greater_equal, mid)
        ge = n_mid >= top_k
        return jnp.where(ge, mid, lo), jnp.where(ge, hi, mid), jnp.where(ge, n_mid, n_lo)

    def snap(lo):
        def body(c, val):
            s = sc_ref[rows_of(c), :]
            return jnp.minimum(val, jnp.min(jnp.where(s >= lo, s, jnp.inf), axis=0, keepdims=True))
        val = lax.fori_loop(0, n_chunks, body, jnp.full((1, tq), jnp.inf, F32))
        return val, count(jnp.greater, val)

    bracket = (lo, hi, n_allowed.astype(F32))
    bracket = lax.fori_loop(0, DSA_BISECT_STEPS, bisect, bracket)

    def keeps_too_many(state):
        return jnp.logical_and(jnp.max(state[0][2]) > top_k, state[1] < DSA_BISECT_EXTRA_ROUNDS)

    def halve_more(state):
        return lax.fori_loop(0, DSA_BISECT_REFINE, bisect, state[0]), state[1] + 1

    bracket, _ = lax.while_loop(keeps_too_many, halve_more, (bracket, jnp.int32(0)))
    thr, n_above = snap(bracket[0])

    def unsettled(state):
        return jnp.logical_and(jnp.max(state[2]) >= top_k, state[3] < DSA_BISECT_MAX_ROUNDS)

    def refine(state):
        bracket = lax.fori_loop(0, DSA_BISECT_REFINE, bisect, state[0])
        thr, n_above = snap(bracket[0])
        return bracket, thr, n_above, state[3] + 1

    bracket, thr, n_above, _ = lax.while_loop(unsettled, refine, (bracket, thr, n_above, jnp.int32(0)))

    @pl.when(jnp.max(bracket[2]) > top_k)
    def _():
        need = top_k - n_above
        earlier = (lax.broadcasted_iota(jnp.int32, (kc, kc), 1)
                   < lax.broadcasted_iota(jnp.int32, (kc, kc), 0)).astype(BF16)

        def body(c, seen):
            s = sc_ref[rows_of(c), :]
            tie = jnp.where(s == thr, 1.0, 0.0)
            rank = seen + jnp.dot(earlier, tie.astype(BF16), preferred_element_type=F32)
            sc_ref[rows_of(c), :] = jnp.where((tie > 0.0) & (rank >= need), -jnp.inf, s)
            return seen + jnp.sum(tie, axis=0, keepdims=True)

        lax.fori_loop(0, n_chunks, body, jnp.zeros((1, tq), F32))

    q_t = jnp.concatenate([_mm(wukt_ref[h], q[:, h * DSA_HEAD:(h + 1) * DSA_HEAD], _NT, 1)
                           for h in range(DSA_HEADS)], axis=1)
    q_t = (q_t * (DSA_HEAD ** -0.5 * math.log2(math.e))).astype(BF16)

    def finite_or_zero(m):
        return jnp.where(m == -jnp.inf, 0.0, m)

    def attend(chunks, m, acc):
        n = range(len(chunks))
        logits = [jnp.dot(ckv_ref[rows_of(c), :], q_t, preferred_element_type=F32)
                  + jnp.concatenate([jnp.where(sc_ref[rows_of(c), :] >= thr, 0.0, -jnp.inf)] * DSA_HEADS, axis=1)
                  for c in chunks]
        m_new = [jnp.maximum(m[s], jnp.max(logits[s], axis=0, keepdims=True)) for s in n]
        m_safe = [finite_or_zero(m_new[s]) for s in n]
        e = [jnp.exp2(logits[s] - m_safe[s]).astype(BF16) for s in n]
        acc = [acc[s] * jnp.exp2(m[s] - m_safe[s])
               + jnp.dot(ckvt_ref[chunks[s]], e[s], preferred_element_type=F32) for s in n]
        return m_new, acc

    def attend_pair(i, carry):
        m, acc = attend([2 * i, 2 * i + 1], carry[:2], carry[2:])
        return m[0], m[1], acc[0], acc[1]

    def attend_last(_, carry):
        m, acc = attend([n_chunks - 1], carry[:1], carry[2:3])
        return m[0], carry[1], acc[0], carry[3]

    m_init = jnp.full((1, hq), -jnp.inf, F32)
    acc_init = jnp.zeros((DSA_LATENT + DSA_SUM_ROWS, hq), F32)
    streams = lax.fori_loop(0, lax.shift_right_logical(n_chunks, 1), attend_pair,
                            (m_init, m_init, acc_init, acc_init))
    m0, m1, acc0, acc1 = lax.fori_loop(0, n_chunks & 1, attend_last, streams)
    m_safe = finite_or_zero(jnp.maximum(m0, m1))
    acc = acc0 * jnp.exp2(m0 - m_safe) + acc1 * jnp.exp2(m1 - m_safe)
    out_t = acc[:DSA_LATENT] / acc[DSA_LATENT:DSA_LATENT + 1]
    for h in range(DSA_HEADS):
        o_ref[0, :, h * DSA_HEAD:(h + 1) * DSA_HEAD] = _mm(out_t[:, h * tq:(h + 1) * tq], wuv_ref[h], _TN, 1)


def _dsa(p, kv_norm, w_uk_t, w_uv, top_k):
    b, t, c = p.shape
    n_chunks = -(-t // DSA_KEY_CHUNK)
    return pl.pallas_call(
        functools.partial(_dsa_kernel, top_k=top_k),
        out_shape=jax.ShapeDtypeStruct((b, t, BRANCH_W), F32),
        grid=(b, t // DSA_QBLOCK),
        in_specs=[
            pl.BlockSpec((1, DSA_QBLOCK, c), lambda i, j: (i, j, 0)),
            pl.BlockSpec((1, t, DSA_KEY_COL_BLOCK), lambda i, j: (i, 0, c // DSA_KEY_COL_BLOCK - 1)),
            pl.BlockSpec((1, DSA_LATENT), lambda i, j: (0, 0)),
            pl.BlockSpec(w_uk_t.shape, lambda i, j: (0, 0, 0)),
            pl.BlockSpec(w_uv.shape, lambda i, j: (0, 0, 0)),
        ],
        out_specs=pl.BlockSpec((1, DSA_QBLOCK, BRANCH_W), lambda i, j: (i, j, 0)),
        scratch_shapes=[
            pltpu.VMEM((n_chunks * DSA_KEY_CHUNK, DSA_LATENT), BF16),
            pltpu.VMEM((n_chunks, DSA_LATENT + DSA_SUM_ROWS, DSA_KEY_CHUNK), BF16),
            pltpu.VMEM((n_chunks * DSA_KEY_CHUNK, IDX_HEAD), BF16),
            pltpu.VMEM((n_chunks * DSA_KEY_CHUNK, DSA_QBLOCK), F32),
        ],
        compiler_params=_params("parallel", "arbitrary"),
        name="dsa",
    )(p, p, kv_norm.reshape(1, -1), w_uk_t, w_uv)


def _merge_kernel(h_ref, yr_ref, ym_ref, yd_ref, wg_ref, wb_ref, wo_ref, gpre_ref, gpost_ref, o_ref):
    h = h_ref[...]
    xn = _rms(h, gpre_ref[...]).astype(BF16)
    mixed = None
    for i, y_ref in enumerate((yr_ref, ym_ref, yd_ref)):
        gate = jnp.dot(xn, wg_ref[:, i * D_MODEL:(i + 1) * D_MODEL], preferred_element_type=F32)
        proj = jnp.dot(y_ref[...].astype(BF16), wb_ref[i], preferred_element_type=F32)
        term = _sigmoid(gate) * proj
        mixed = term if mixed is None else mixed + term
    out = jnp.dot(mixed.astype(BF16), wo_ref[...], preferred_element_type=F32)
    o_ref[...] = h + _rms(out, gpost_ref[...])


def _merge(h, y_rw, y_ml, y_dsa, w_gate, w_branch, w_out, gain_pre, gain_post):
    m, d = h.shape
    rows = lambda n: pl.BlockSpec((ROW_TILE, n), lambda i: (i, 0))
    return pl.pallas_call(
        _merge_kernel,
        out_shape=jax.ShapeDtypeStruct((m, d), F32),
        grid=(m // ROW_TILE,),
        in_specs=[
            rows(d), rows(BRANCH_W), rows(BRANCH_W), rows(BRANCH_W),
            pl.BlockSpec(w_gate.shape, lambda i: (0, 0)),
            pl.BlockSpec(w_branch.shape, lambda i: (0, 0, 0)),
            pl.BlockSpec(w_out.shape, lambda i: (0, 0)),
            pl.BlockSpec((1, d), lambda i: (0, 0)),
            pl.BlockSpec((1, d), lambda i: (0, 0)),
        ],
        out_specs=rows(d),
        compiler_params=_params("parallel"),
        name="merge",
    )(h, y_rw, y_ml, y_dsa, w_gate, w_branch, w_out, gain_pre.reshape(1, d), gain_post.reshape(1, d))


def _split_w_in(w):
    rw_end = RW_COLS
    ml_end = rw_end + 4 * BRANCH_W + 2 * ML_HEADS
    dsa_end = ml_end + BRANCH_W + DSA_LATENT + IDX_HEADS * IDX_HEAD + IDX_HEAD + IDX_HEADS
    w_rw, w_ml, w_dsa, w_gate = w[:, :rw_end], w[:, rw_end:ml_end], w[:, ml_end:dsa_end], w[:, dsa_end:]
    w_ml = jnp.pad(w_ml, ((0, 0), (0, ML_COLS_PAD - w_ml.shape[1])))
    q, c_kv, rest = w_dsa[:, :BRANCH_W], w_dsa[:, BRANCH_W:BRANCH_W + DSA_LATENT], w_dsa[:, BRANCH_W + DSA_LATENT:]
    q_idx, tail = rest[:, :IDX_HEADS * IDX_HEAD], rest[:, IDX_HEADS * IDX_HEAD:]
    w_dsa = jnp.concatenate([q, q_idx, c_kv, tail], axis=1)
    w_dsa = jnp.pad(w_dsa, ((0, 0), (0, DSA_COLS_PAD - w_dsa.shape[1])))
    return tuple(x.astype(BF16) for x in (w_rw, w_ml, w_dsa, w_gate))


def kernel(x, meta_tokens, norm_gain, ffn_w_in, ffn_w_out, w_in, rw_mu, rw_w0, rw_w_up, rw_a0, rw_a_up, rw_g_up, rw_k_k, rw_k_a, rw_r_k, rw_gn_gain, rw_gn_bias, ml_conv_w, ml_conv_b, ml_i_bias, ml_f_bias, ml_norm_gain, dsa_kv_norm, dsa_w_uk, dsa_w_uv, w_branch, w_out):
    bsz, seq, d = x.shape
    depth = norm_gain.shape[0]
    top_k = min(TOPK_MAX, seq // 4)
    t_len = seq + N_META
    t_pad = -(-t_len // SEQ_PAD_MULTIPLE) * SEQ_PAD_MULTIPLE
    h = jnp.concatenate([
        jnp.broadcast_to(meta_tokens.astype(x.dtype)[None], (bsz, N_META, d)), x,
        jnp.zeros((bsz, t_pad - t_len, d), x.dtype)], axis=1).reshape(bsz * t_pad, d)

    for l in range(depth):
        g = norm_gain[l]
        h = _ffn(h, g[0], g[1], ffn_w_in[l, 0].astype(BF16), ffn_w_out[l, 0].astype(BF16))

        w_rw, w_ml, w_dsa, w_gate = _split_w_in(w_in[l])
        p_rw, p_ml, p_dsa = (p.reshape(bsz, t_pad, -1) for p in _proj(h, g[2], (w_rw, w_ml, w_dsa)))

        y_rw = _rwkv(p_rw, rw_mu[l], rw_w0[l], rw_w_up[l], rw_a0[l], rw_a_up[l], rw_g_up[l], rw_k_k[l],
                     rw_k_a[l], rw_r_k[l], rw_gn_gain[l], rw_gn_bias[l])
        gate_bias = jnp.pad(jnp.concatenate([ml_i_bias[l], ml_f_bias[l]]), (0, 128 - 2 * ML_HEADS))
        y_ml = _mlstm(p_ml, ml_conv_w[l], ml_conv_b[l], gate_bias.reshape(1, 128), ml_norm_gain[l])
        y_dsa = _dsa(p_dsa, dsa_kv_norm[l], jnp.swapaxes(dsa_w_uk[l], 1, 2).astype(BF16), dsa_w_uv[l].astype(BF16), top_k)

        flat = lambda y: y.reshape(bsz * t_pad, BRANCH_W)
        h = _merge(h, flat(y_rw), flat(y_ml), flat(y_dsa), w_gate, w_branch[l].astype(BF16),
                   w_out[l].astype(BF16), g[2], g[3])
        h = _ffn(h, g[4], g[5], ffn_w_in[l, 1].astype(BF16), ffn_w_out[l, 1].astype(BF16))

    return h.reshape(bsz, t_pad, d)[:, N_META:t_len]
```

```python
import functools
import math

import jax
import jax.numpy as jnp
from jax import lax
from jax.experimental import pallas as pl
from jax.experimental.pallas import tpu as pltpu

F32 = jnp.float32
BF16 = jnp.bfloat16
HI = lax.Precision.HIGHEST

D_MODEL = 1024
D_FF = 2816
N_META = 16
STREAM_CHUNK = 64
BRANCH_W = 512
NORM_EPS = 1e-6

RW_HEADS, RW_HEAD = 8, 64
RW_COLS = 1792
RW_GN_EPS = 64e-5
RW_CHUNK = 64

ML_HEADS, ML_HEAD = 4, 128
ML_CHUNK = 64
ML_CONV = 4
ML_COLS_PAD = 2176

DSA_HEADS, DSA_HEAD, DSA_LATENT = 8, 64, 128
IDX_HEADS, IDX_HEAD = 8, 64
TOPK_MAX = 256
DSA_QBLOCK = 128
DSA_COLS_PAD = 1280
DSA_KEY_COL_BLOCK = 256

ROW_TILE = 512
FF_TILE = 1408
SEQ_PAD_MULTIPLE = 128
RW_SEQ_GROUP = 2
ML_SEQ_GROUP = 4
VMEM_LIMIT = 56 * 1024 * 1024
DSA_KEY_CHUNK = 512
DSA_SUM_ROWS = 16
DSA_BISECT_STEPS = 16
DSA_BISECT_REFINE = 4
DSA_BISECT_EXTRA_ROUNDS = 2
DSA_BISECT_MAX_ROUNDS = 80


def _dot(a, b):
    return jnp.dot(a.astype(BF16), b.astype(BF16), preferred_element_type=F32)


def _dot_nt(a, b):
    return lax.dot_general(a.astype(BF16), b.astype(BF16), (((1,), (1,)), ((), ())),
                           preferred_element_type=F32)


def _dot_tn(a, b):
    return lax.dot_general(a.astype(BF16), b.astype(BF16), (((0,), (0,)), ((), ())),
                           preferred_element_type=F32)


def _dot_hi(a, b):
    return jnp.dot(a, b, preferred_element_type=F32, precision=HI)


def _rms(x, gain):
    return x * lax.rsqrt(jnp.mean(x * x, axis=-1, keepdims=True) + NORM_EPS) * gain


def _sigmoid(x):
    return jax.nn.sigmoid(x)


def _tril(n, strict=False):
    r = lax.broadcasted_iota(jnp.int32, (n, n), 0)
    c = lax.broadcasted_iota(jnp.int32, (n, n), 1)
    return (c < r) if strict else (c <= r)


def _params(*sem):
    return pltpu.CompilerParams(dimension_semantics=sem, vmem_limit_bytes=VMEM_LIMIT)


def _ffn_kernel(h_ref, gpre_ref, gpost_ref, wg_ref, wu_ref, wo_ref, o_ref, xn_ref, acc_ref):
    j = pl.program_id(1)

    @pl.when(j == 0)
    def _():
        xn_ref[...] = _rms(h_ref[...], gpre_ref[...]).astype(BF16)
        acc_ref[...] = jnp.zeros_like(acc_ref)

    xn = xn_ref[...]
    gate = jnp.dot(xn, wg_ref[...], preferred_element_type=F32)
    up = jnp.dot(xn, wu_ref[...], preferred_element_type=F32)
    act = gate * _sigmoid(gate) * up
    acc_ref[...] += jnp.dot(act.astype(BF16), wo_ref[...], preferred_element_type=F32)

    @pl.when(j == pl.num_programs(1) - 1)
    def _():
        o_ref[...] = h_ref[...] + 0.5 * _rms(acc_ref[...], gpost_ref[...])


def _ffn(h, g_pre, g_post, w_in, w_out):
    m, d = h.shape
    nj = D_FF // FF_TILE
    return pl.pallas_call(
        _ffn_kernel,
        out_shape=jax.ShapeDtypeStruct((m, d), F32),
        grid=(m // ROW_TILE, nj),
        in_specs=[
            pl.BlockSpec((ROW_TILE, d), lambda i, j: (i, 0)),
            pl.BlockSpec((1, d), lambda i, j: (0, 0)),
            pl.BlockSpec((1, d), lambda i, j: (0, 0)),
            pl.BlockSpec((d, FF_TILE), lambda i, j: (0, j)),
            pl.BlockSpec((d, FF_TILE), lambda i, j: (0, j + nj)),
            pl.BlockSpec((FF_TILE, d), lambda i, j: (j, 0)),
        ],
        out_specs=pl.BlockSpec((ROW_TILE, d), lambda i, j: (i, 0)),
        scratch_shapes=[pltpu.VMEM((ROW_TILE, d), BF16), pltpu.VMEM((ROW_TILE, d), F32)],
        compiler_params=_params("parallel", "arbitrary"),
        name="ffn",
    )(h, g_pre.reshape(1, d), g_post.reshape(1, d), w_in, w_in, w_out)


def _proj_kernel(h_ref, g_ref, *refs):
    w_refs, o_refs = refs[:len(refs) // 2], refs[len(refs) // 2:]
    xn = _rms(h_ref[...], g_ref[...]).astype(BF16)
    for w_ref, o_ref in zip(w_refs, o_refs):
        o_ref[...] = jnp.dot(xn, w_ref[...], preferred_element_type=F32)


def _proj(h, gain, weights):
    m, d = h.shape
    return pl.pallas_call(
        _proj_kernel,
        out_shape=[jax.ShapeDtypeStruct((m, w.shape[1]), F32) for w in weights],
        grid=(m // ROW_TILE,),
        in_specs=[
            pl.BlockSpec((ROW_TILE, d), lambda i: (i, 0)),
            pl.BlockSpec((1, d), lambda i: (0, 0)),
        ] + [pl.BlockSpec(w.shape, lambda i: (0, 0)) for w in weights],
        out_specs=[pl.BlockSpec((ROW_TILE, w.shape[1]), lambda i: (i, 0)) for w in weights],
        compiler_params=_params("parallel"),
        name="in_proj",
    )(h, gain.reshape(1, d), *weights)


def _rwkv_kernel(p_ref, mu_ref, w0_ref, wup_ref, a0_ref, aup_ref, gup_ref, kk_ref, ka_ref, rk_ref,
                 gng_ref, gnb_ref, hs_ref, o_ref, prev_ref, s_ref):
    @pl.when(pl.program_id(1) == 0)
    def _():
        prev_ref[...] = jnp.zeros_like(prev_ref)
        s_ref[...] = jnp.zeros_like(s_ref)

    L = RW_CHUNK
    lower = _tril(L)
    strict = _tril(L, strict=True)
    tril_f = lower.astype(F32)
    row = lax.broadcasted_iota(jnp.int32, (L, 1), 0)
    steps = int(math.log2(L))
    seqs = range(p_ref.shape[0])

    def head_sum(x):
        return _dot(x, hs_ref[...])

    r, k2, v, gate, g_last, r_rows, a_rows, b_rows, k_rows, v_b = ([] for _ in range(10))
    for g in seqs:
        p = p_ref[g]
        prev = jnp.where(row == 0, prev_ref[g], pltpu.roll(p, 1, 0))
        prev_ref[g] = p[L - 1:L, :]
        ps = p + (prev - p) * mu_ref[...]
        k = ps[:, BRANCH_W:2 * BRANCH_W]
        d = w0_ref[...] + _dot(jnp.tanh(ps[:, 1536:1600]), wup_ref[...])
        log_w = -math.exp(-0.5) * _sigmoid(d)
        a = _sigmoid(a0_ref[...] + _dot(ps[:, 1600:1664], aup_ref[...]))
        gate.append(_dot(_sigmoid(ps[:, 1664:1792]), gup_ref[...]))
        r.append(ps[:, 0:BRANCH_W])
        v.append(ps[:, 2 * BRANCH_W:3 * BRANCH_W])
        kappa = k * kk_ref[...]
        kh = kappa / jnp.maximum(jnp.sqrt(head_sum(kappa * kappa)), 1e-12)
        k2.append(k * (1.0 + (a - 1.0) * ka_ref[...]))
        cum = _dot_hi(tril_f, log_w)
        g_incl = jnp.exp(cum)
        g_inv = jnp.exp(-cum)
        g_last.append(g_incl[L - 1:L, :])
        r_rows.append((r[g] * g_incl).astype(BF16))
        a_rows.append((-kh * jnp.exp(cum - log_w)).astype(BF16))
        b_rows.append((a * kh * g_inv).astype(BF16))
        k_rows.append((k2[g] * g_inv).astype(BF16))
        v_b.append(v[g].astype(BF16))

    units = [(g, h) for g in seqs for h in range(RW_HEADS)]
    idx = range(len(units))
    sl = [slice(h * RW_HEAD, (h + 1) * RW_HEAD) for _, h in units]
    lhs = [jnp.concatenate([r_rows[g][:, sl[i]], a_rows[g][:, sl[i]]], axis=0) for i, (g, h) in enumerate(units)]
    b_u = [b_rows[g][:, sl[i]] for i, (g, h) in enumerate(units)]
    k_u = [k_rows[g][:, sl[i]] for i, (g, h) in enumerate(units)]
    v_u = [v_b[g][:, sl[i]] for i, (g, h) in enumerate(units)]
    s0 = [s_ref[g * RW_HEADS + h] for g, h in units]
    gram = [_dot_nt(lhs[i], jnp.concatenate([b_u[i], k_u[i]], axis=0)) for i in idx]
    ls0 = [_dot_nt(lhs[i], s0[i]) for i in idx]
    u = [ls0[i][L:] + _dot(jnp.where(strict, gram[i][L:, L:], 0.0), v_u[i]) for i in idx]
    n_pow = [jnp.where(strict, gram[i][L:, :L], 0.0) for i in idx]
    for step in range(steps):
        u = [u[i] + _dot(n_pow[i], u[i]) for i in idx]
        if step + 1 < steps:
            n_pow = [_dot(n_pow[i], n_pow[i]) for i in idx]

    y_u = [ls0[i][:L] + _dot(jnp.where(lower, gram[i][:L, :L], 0.0), u[i])
           + _dot(jnp.where(lower, gram[i][:L, L:], 0.0), v_u[i]) for i in idx]
    for i, (g, h) in enumerate(units):
        s_ref[g * RW_HEADS + h] = (s0[i] + _dot_tn(u[i], b_u[i]) + _dot_tn(v_u[i], k_u[i])) * g_last[g][:, sl[i]]

    for g in seqs:
        y = jnp.concatenate(y_u[g * RW_HEADS:(g + 1) * RW_HEADS], axis=1)
        dev = y - head_sum(y) * (1.0 / RW_HEAD)
        var = head_sum(dev * dev) * (1.0 / RW_HEAD)
        y = dev * lax.rsqrt(var + RW_GN_EPS) * gng_ref[...] + gnb_ref[...]
        bonus = head_sum(r[g] * k2[g] * rk_ref[...])
        o_ref[g] = (y + bonus * v[g]) * gate[g]


def _rwkv(p, mu, w0, w_up, a0, a_up, g_up, k_k, k_a, r_k, gn_gain, gn_bias):
    b, t, c = p.shape
    group = math.gcd(b, RW_SEQ_GROUP)
    head_of = jnp.arange(BRANCH_W) // RW_HEAD
    same_head = (head_of[:, None] == head_of[None, :]).astype(BF16)
    row = lambda x: x.reshape(1, -1)
    vec = lambda n: pl.BlockSpec((1, n), lambda i, j: (0, 0))
    mat = lambda shp: pl.BlockSpec(shp, lambda i, j: (0, 0))
    return pl.pallas_call(
        _rwkv_kernel,
        out_shape=jax.ShapeDtypeStruct((b, t, BRANCH_W), F32),
        grid=(b // group, t // RW_CHUNK),
        in_specs=[
            pl.BlockSpec((group, RW_CHUNK, c), lambda i, j: (i, j, 0)),
            vec(c), vec(BRANCH_W), mat(w_up.shape), vec(BRANCH_W), mat(a_up.shape), mat(g_up.shape),
            vec(BRANCH_W), vec(BRANCH_W), vec(BRANCH_W), vec(BRANCH_W), vec(BRANCH_W),
            mat((BRANCH_W, BRANCH_W)),
        ],
        out_specs=pl.BlockSpec((group, RW_CHUNK, BRANCH_W), lambda i, j: (i, j, 0)),
        scratch_shapes=[pltpu.VMEM((group, 1, c), F32),
                        pltpu.VMEM((group * RW_HEADS, RW_HEAD, RW_HEAD), F32)],
        compiler_params=_params("parallel", "arbitrary"),
        name="rwkv7",
    )(p, row(mu), row(w0), w_up, row(a0), a_up, g_up, row(k_k), row(k_a), row(r_k), row(gn_gain),
      row(gn_bias), same_head)


def _log_sigmoid(x):
    return jnp.minimum(x, 0.0) - jnp.log1p(jnp.exp(-jnp.abs(x)))


def _mlstm_kernel(p_ref, cw_ref, cb_ref, gb_ref, ng_ref, o_ref, tail_ref, c_ref, n_ref, m_ref):
    @pl.when(pl.program_id(1) == 0)
    def _():
        tail_ref[...] = jnp.zeros_like(tail_ref)
        c_ref[...] = jnp.zeros_like(c_ref)
        n_ref[...] = jnp.zeros_like(n_ref)
        m_ref[...] = jnp.zeros_like(m_ref)

    L = ML_CHUNK
    qk_w = 2 * BRANCH_W
    lower = _tril(L)
    tril_f = lower.astype(F32)
    seqs = range(p_ref.shape[0])

    x = [p_ref[g] for g in seqs]
    q, k, gates, b_all, b_t, g_t = [], [], [], [], [], []
    for g in seqs:
        qk_in = x[g][:, :qk_w]
        ext = jnp.concatenate([tail_ref[g], qk_in], axis=0)
        tail_ref[g] = qk_in[L - 8:L, :]
        conv = cb_ref[...] + cw_ref[ML_CONV - 1:ML_CONV, :] * qk_in
        for j in range(ML_CONV - 1):
            conv = conv + cw_ref[j:j + 1, :] * pltpu.roll(ext, ML_CONV - 1 - j, 0)[8:8 + L]
        qk = conv * _sigmoid(conv)
        q.append(qk[:, :BRANCH_W])
        k.append(qk[:, BRANCH_W:] * ML_HEAD ** -0.5)
        gates.append(x[g][:, qk_w + 2 * BRANCH_W:] + gb_ref[...])
        b_all.append(_dot_hi(tril_f, _log_sigmoid(gates[g])))
        b_t.append(b_all[g].T)
        g_t.append(gates[g].T)

    units = [(g, h) for g in seqs for h in range(ML_HEADS)]
    sl = [slice(h * ML_HEAD, (h + 1) * ML_HEAD) for _, h in units]
    q_u = [q[g][:, sl[i]] for i, (g, h) in enumerate(units)]
    k_u = [k[g][:, sl[i]] for i, (g, h) in enumerate(units)]
    v_u = [x[g][:, qk_w + h * ML_HEAD:qk_w + (h + 1) * ML_HEAD] for g, h in units]
    b_col = [b_all[g][:, ML_HEADS + h:ML_HEADS + h + 1] for g, h in units]
    b_row = [b_t[g][ML_HEADS + h:ML_HEADS + h + 1, :] for g, h in units]
    i_col = [gates[g][:, h:h + 1] for g, h in units]
    i_row = [g_t[g][h:h + 1, :] for g, h in units]
    m_st = [m_ref[g, h:h + 1, 0:1] for g, h in units]
    n_st = [n_ref[g, h:h + 1, :] for g, h in units]
    c_st = [c_ref[g * ML_HEADS + h] for g, h in units]
    idx = range(len(units))

    log_d = [jnp.where(lower, b_col[i] - b_row[i] + i_row[i], -jnp.inf) for i in idx]
    m_j = [jnp.maximum(b_col[i] + m_st[i], jnp.max(log_d[i], axis=-1, keepdims=True)) for i in idx]
    s = [_dot_nt(q_u[i], k_u[i]) * jnp.exp(log_d[i] - m_j[i]) for i in idx]
    inter = [jnp.exp(b_col[i] + m_st[i] - m_j[i]) for i in idx]
    num = [_dot(s[i], v_u[i]) + inter[i] * _dot_nt(q_u[i], c_st[i]) for i in idx]
    den = [jnp.sum(s[i], axis=-1, keepdims=True)
           + inter[i] * jnp.sum(q_u[i] * n_st[i], axis=-1, keepdims=True) for i in idx]
    hh = [num[i] / jnp.maximum(jnp.abs(den[i]), jnp.exp(-m_j[i])) for i in idx]

    g_tot = [b_col[i][L - 1:L, :] for i in idx]
    w_log = [g_tot[i] - b_col[i] + i_col[i] for i in idx]
    m_new = [jnp.maximum(g_tot[i] + m_st[i], jnp.max(w_log[i], axis=0, keepdims=True)) for i in idx]
    wgt = [jnp.exp(w_log[i] - m_new[i]) for i in idx]
    dec = [jnp.exp(g_tot[i] + m_st[i] - m_new[i]) for i in idx]
    c_new = [dec[i] * c_st[i] + _dot_tn(wgt[i] * v_u[i], k_u[i]) for i in idx]
    n_new = [dec[i] * n_st[i] + jnp.sum(wgt[i] * k_u[i], axis=0, keepdims=True) for i in idx]
    hn = [hh[i] * lax.rsqrt(jnp.mean(hh[i] * hh[i], axis=-1, keepdims=True) + NORM_EPS) * ng_ref[:, sl[i]]
          for i in idx]

    for i, (g, h) in enumerate(units):
        c_ref[g * ML_HEADS + h] = c_new[i]
        n_ref[g, h:h + 1, :] = n_new[i]
        m_ref[g, h:h + 1, :] = jnp.broadcast_to(m_new[i], (1, ML_HEAD))
        o_gate = x[g][:, qk_w + BRANCH_W + h * ML_HEAD:qk_w + BRANCH_W + (h + 1) * ML_HEAD]
        o_ref[g, :, sl[i]] = _sigmoid(o_gate) * hn[i]


def _mlstm(p, conv_w, conv_b, gate_bias, norm_gain):
    b, t, c = p.shape
    group = math.gcd(b, ML_SEQ_GROUP)
    return pl.pallas_call(
        _mlstm_kernel,
        out_shape=jax.ShapeDtypeStruct((b, t, BRANCH_W), F32),
        grid=(b // group, t // ML_CHUNK),
        in_specs=[
            pl.BlockSpec((group, ML_CHUNK, c), lambda i, j: (i, j, 0)),
            pl.BlockSpec(conv_w.shape, lambda i, j: (0, 0)),
            pl.BlockSpec((1, 2 * BRANCH_W), lambda i, j: (0, 0)),
            pl.BlockSpec((1, 128), lambda i, j: (0, 0)),
            pl.BlockSpec((1, BRANCH_W), lambda i, j: (0, 0)),
        ],
        out_specs=pl.BlockSpec((group, ML_CHUNK, BRANCH_W), lambda i, j: (i, j, 0)),
        scratch_shapes=[
            pltpu.VMEM((group, 8, 2 * BRANCH_W), F32),
            pltpu.VMEM((group * ML_HEADS, ML_HEAD, ML_HEAD), F32),
            pltpu.VMEM((group, 8, ML_HEAD), F32),
            pltpu.VMEM((group, 8, ML_HEAD), F32),
        ],
        compiler_params=_params("parallel", "arbitrary"),
        name="mlstm",
    )(p, conv_w, conv_b.reshape(1, -1), gate_bias, norm_gain.reshape(1, -1))


def _sum_sublane_groups(x):
    n = x.shape[0] // 32
    g = x.reshape(4 * n, 8, x.shape[1])
    parts = [g[i * n:(i + 1) * n] for i in range(4)]
    return (jnp.sum(parts[0], axis=0) + jnp.sum(parts[1], axis=0)) + (jnp.sum(parts[2], axis=0) + jnp.sum(parts[3], axis=0))


def _dsa_kernel(pq_ref, pk_ref, kvn_ref, wukt_ref, wuv_ref, o_ref, ckv_ref, ckvt_ref, kidx_ref, sc_ref,
                *, top_k):
    qb = pl.program_id(1)
    tq = DSA_QBLOCK
    kc = DSA_KEY_CHUNK
    tk = pk_ref.shape[1]
    n_chunks_max = ckvt_ref.shape[0]
    hq = DSA_HEADS * tq

    @pl.when(qb == 0)
    def _():
        keys = pk_ref[0]
        ckv = _rms(keys[:, :DSA_LATENT], kvn_ref[...])
        pad_rows = n_chunks_max * kc - tk
        ckv_ref[0:tk, :] = ckv.astype(BF16)
        ckv_ref[tk:, :] = jnp.zeros((pad_rows, DSA_LATENT), BF16)
        kidx_ref[0:tk, :] = keys[:, DSA_LATENT:DSA_LATENT + IDX_HEAD].astype(BF16)
        kidx_ref[tk:, :] = jnp.zeros((pad_rows, IDX_HEAD), BF16)
        ckv_t = jnp.concatenate([ckv.T, jnp.zeros((DSA_LATENT, pad_rows), F32)], axis=1)
        extra = jnp.where(lax.broadcasted_iota(jnp.int32, (DSA_SUM_ROWS, kc), 0) == 0, 1.0, 0.0)
        for c in range(n_chunks_max):
            ckvt_ref[c] = jnp.concatenate([ckv_t[:, c * kc:(c + 1) * kc], extra], axis=0).astype(BF16)

    n_chunks = jnp.minimum(lax.shift_right_logical(qb * tq + tq + N_META + kc - 1, int(math.log2(kc))),
                           n_chunks_max)
    q_pos = qb * tq + lax.broadcasted_iota(jnp.int32, (1, tq), 1)
    q_chunk = jnp.where(q_pos < N_META, 0,
                        1 + lax.shift_right_arithmetic(q_pos - N_META, int(math.log2(STREAM_CHUNK))))
    n_allowed = jnp.minimum(N_META + STREAM_CHUNK * q_chunk, tk)

    def rows_of(c):
        return pl.ds(pl.multiple_of(c * kc, kc), kc)

    pq = pq_ref[0]
    q = pq[:, :BRANCH_W]
    qi_t = jnp.concatenate([pq[:, BRANCH_W + j * 128:BRANCH_W + (j + 1) * 128].T
                            for j in range(IDX_HEADS * IDX_HEAD // 128)], axis=0)
    qi_t = jnp.concatenate([qi_t[h * IDX_HEAD:(h + 1) * IDX_HEAD] for h in range(IDX_HEADS)], axis=1)
    tail_t = pq[:, DSA_COLS_PAD - 128:].T
    w_t = tail_t[IDX_HEAD:IDX_HEAD + IDX_HEADS] * (IDX_HEADS * IDX_HEAD) ** -0.5
    w_row = jnp.concatenate([w_t[h:h + 1] for h in range(IDX_HEADS)], axis=1)
    qi_t = qi_t.astype(BF16)

    def score_chunk(c, bounds):
        lo, hi = bounds
        s = jnp.maximum(jnp.dot(kidx_ref[rows_of(c), :], qi_t, preferred_element_type=F32), 0.0) * w_row
        score = s[:, 0:tq]
        for h in range(1, IDX_HEADS):
            score = score + s[:, h * tq:(h + 1) * tq]
        k_pos = c * kc + lax.broadcasted_iota(jnp.int32, (kc, 1), 0)
        allowed = k_pos < n_allowed
        sc_ref[rows_of(c), :] = jnp.where(allowed, score, -jnp.inf)
        lo = jnp.minimum(lo, jnp.min(jnp.where(allowed, score, jnp.inf), axis=0, keepdims=True))
        hi = jnp.maximum(hi, jnp.max(jnp.where(allowed, score, -jnp.inf), axis=0, keepdims=True))
        return lo, hi

    lo, row_max = lax.fori_loop(0, n_chunks, score_chunk,
                                (jnp.full((1, tq), jnp.inf, F32), jnp.full((1, tq), -jnp.inf, F32)))
    hi = row_max + jnp.maximum(jnp.abs(row_max) * 2.0 ** -20, 1e-30)

    def count(pred, level):
        def body(c, acc):
            return acc + _sum_sublane_groups(jnp.where(pred(sc_ref[rows_of(c), :], level), 1.0, 0.0))
        return jnp.sum(lax.fori_loop(0, n_chunks, body, jnp.zeros((8, tq), F32)), axis=0, keepdims=True)

    def bisect(_, bracket):
        lo, hi, n_lo = bracket
        mid = 0.5 * lo + 0.5 * hi
        n_mid = count(jnp.greater_equal, mid)
        ge = n_mid >= top_k
        return jnp.where(ge, mid, lo), jnp.where(ge, hi, mid), jnp.where(ge, n_mid, n_lo)

    def snap(lo):
        def body(c, val):
            s = sc_ref[rows_of(c), :]
            return jnp.minimum(val, jnp.min(jnp.where(s >= lo, s, jnp.inf), axis=0, keepdims=True))
        val = lax.fori_loop(0, n_chunks, body, jnp.full((1, tq), jnp.inf, F32))
        return val, count(jnp.greater, val)

    bracket = (lo, hi, n_allowed.astype(F32))
    bracket = lax.fori_loop(0, DSA_BISECT_STEPS, bisect, bracket)

    def keeps_too_many(state):
        return jnp.logical_and(jnp.max(state[0][2]) > top_k, state[1] < DSA_BISECT_EXTRA_ROUNDS)

    def halve_more(state):
        return lax.fori_loop(0, DSA_BISECT_REFINE, bisect, state[0]), state[1] + 1

    bracket, _ = lax.while_loop(keeps_too_many, halve_more, (bracket, jnp.int32(0)))
    thr, n_above = snap(bracket[0])

    def unsettled(state):
        return jnp.logical_and(jnp.max(state[2]) >= top_k, state[3] < DSA_BISECT_MAX_ROUNDS)

    def refine(state):
        bracket = lax.fori_loop(0, DSA_BISECT_REFINE, bisect, state[0])
        thr, n_above = snap(bracket[0])
        return bracket, thr, n_above, state[3] + 1

    bracket, thr, n_above, _ = lax.while_loop(unsettled, refine, (bracket, thr, n_above, jnp.int32(0)))

    @pl.when(jnp.max(bracket[2]) > top_k)
    def _():
        need = top_k - n_above
        earlier = (lax.broadcasted_iota(jnp.int32, (kc, kc), 1)
                   < lax.broadcasted_iota(jnp.int32, (kc, kc), 0)).astype(BF16)

        def body(c, seen):
            s = sc_ref[rows_of(c), :]
            tie = jnp.where(s == thr, 1.0, 0.0)
            rank = seen + jnp.dot(earlier, tie.astype(BF16), preferred_element_type=F32)
            sc_ref[rows_of(c), :] = jnp.where((tie > 0.0) & (rank >= need), -jnp.inf, s)
            return seen + jnp.sum(tie, axis=0, keepdims=True)

        lax.fori_loop(0, n_chunks, body, jnp.zeros((1, tq), F32))

    q_t = jnp.concatenate([_dot_nt(wukt_ref[h], q[:, h * DSA_HEAD:(h + 1) * DSA_HEAD])
                           for h in range(DSA_HEADS)], axis=1)
    q_t = (q_t * (DSA_HEAD ** -0.5 * math.log2(math.e))).astype(BF16)

    def finite_or_zero(m):
        return jnp.where(m == -jnp.inf, 0.0, m)

    def attend(chunks, m, acc):
        n = range(len(chunks))
        logits = [jnp.dot(ckv_ref[rows_of(c), :], q_t, preferred_element_type=F32)
                  + jnp.concatenate([jnp.where(sc_ref[rows_of(c), :] >= thr, 0.0, -jnp.inf)] * DSA_HEADS, axis=1)
                  for c in chunks]
        m_new = [jnp.maximum(m[s], jnp.max(logits[s], axis=0, keepdims=True)) for s in n]
        m_safe = [finite_or_zero(m_new[s]) for s in n]
        e = [jnp.exp2(logits[s] - m_safe[s]).astype(BF16) for s in n]
        acc = [acc[s] * jnp.exp2(m[s] - m_safe[s])
               + jnp.dot(ckvt_ref[chunks[s]], e[s], preferred_element_type=F32) for s in n]
        return m_new, acc

    def attend_pair(i, carry):
        m, acc = attend([2 * i, 2 * i + 1], carry[:2], carry[2:])
        return m[0], m[1], acc[0], acc[1]

    def attend_last(_, carry):
        m, acc = attend([n_chunks - 1], carry[:1], carry[2:3])
        return m[0], carry[1], acc[0], carry[3]

    m_init = jnp.full((1, hq), -jnp.inf, F32)
    acc_init = jnp.zeros((DSA_LATENT + DSA_SUM_ROWS, hq), F32)
    streams = lax.fori_loop(0, lax.shift_right_logical(n_chunks, 1), attend_pair,
                            (m_init, m_init, acc_init, acc_init))
    m0, m1, acc0, acc1 = lax.fori_loop(0, n_chunks & 1, attend_last, streams)
    m_safe = finite_or_zero(jnp.maximum(m0, m1))
    acc = acc0 * jnp.exp2(m0 - m_safe) + acc1 * jnp.exp2(m1 - m_safe)
    out_t = acc[:DSA_LATENT] / acc[DSA_LATENT:DSA_LATENT + 1]
    for h in range(DSA_HEADS):
        o_ref[0, :, h * DSA_HEAD:(h + 1) * DSA_HEAD] = _dot_tn(out_t[:, h * tq:(h + 1) * tq], wuv_ref[h])


def _dsa(p, kv_norm, w_uk_t, w_uv, top_k):
    b, t, c = p.shape
    n_chunks = -(-t // DSA_KEY_CHUNK)
    return pl.pallas_call(
        functools.partial(_dsa_kernel, top_k=top_k),
        out_shape=jax.ShapeDtypeStruct((b, t, BRANCH_W), F32),
        grid=(b, t // DSA_QBLOCK),
        in_specs=[
            pl.BlockSpec((1, DSA_QBLOCK, c), lambda i, j: (i, j, 0)),
            pl.BlockSpec((1, t, DSA_KEY_COL_BLOCK), lambda i, j: (i, 0, c // DSA_KEY_COL_BLOCK - 1)),
            pl.BlockSpec((1, DSA_LATENT), lambda i, j: (0, 0)),
            pl.BlockSpec(w_uk_t.shape, lambda i, j: (0, 0, 0)),
            pl.BlockSpec(w_uv.shape, lambda i, j: (0, 0, 0)),
        ],
        out_specs=pl.BlockSpec((1, DSA_QBLOCK, BRANCH_W), lambda i, j: (i, j, 0)),
        scratch_shapes=[
            pltpu.VMEM((n_chunks * DSA_KEY_CHUNK, DSA_LATENT), BF16),
            pltpu.VMEM((n_chunks, DSA_LATENT + DSA_SUM_ROWS, DSA_KEY_CHUNK), BF16),
            pltpu.VMEM((n_chunks * DSA_KEY_CHUNK, IDX_HEAD), BF16),
            pltpu.VMEM((n_chunks * DSA_KEY_CHUNK, DSA_QBLOCK), F32),
        ],
        compiler_params=_params("parallel", "arbitrary"),
        name="dsa",
    )(p, p, kv_norm.reshape(1, -1), w_uk_t, w_uv)


def _merge_kernel(h_ref, yr_ref, ym_ref, yd_ref, wg_ref, wb_ref, wo_ref, gpre_ref, gpost_ref, o_ref):
    h = h_ref[...]
    xn = _rms(h, gpre_ref[...]).astype(BF16)
    mixed = None
    for i, y_ref in enumerate((yr_ref, ym_ref, yd_ref)):
        gate = jnp.dot(xn, wg_ref[:, i * D_MODEL:(i + 1) * D_MODEL], preferred_element_type=F32)
        proj = jnp.dot(y_ref[...].astype(BF16), wb_ref[i], preferred_element_type=F32)
        term = _sigmoid(gate) * proj
        mixed = term if mixed is None else mixed + term
    out = jnp.dot(mixed.astype(BF16), wo_ref[...], preferred_element_type=F32)
    o_ref[...] = h + _rms(out, gpost_ref[...])


def _merge(h, y_rw, y_ml, y_dsa, w_gate, w_branch, w_out, gain_pre, gain_post):
    m, d = h.shape
    rows = lambda n: pl.BlockSpec((ROW_TILE, n), lambda i: (i, 0))
    return pl.pallas_call(
        _merge_kernel,
        out_shape=jax.ShapeDtypeStruct((m, d), F32),
        grid=(m // ROW_TILE,),
        in_specs=[
            rows(d), rows(BRANCH_W), rows(BRANCH_W), rows(BRANCH_W),
            pl.BlockSpec(w_gate.shape, lambda i: (0, 0)),
            pl.BlockSpec(w_branch.shape, lambda i: (0, 0, 0)),
            pl.BlockSpec(w_out.shape, lambda i: (0, 0)),
            pl.BlockSpec((1, d), lambda i: (0, 0)),
            pl.BlockSpec((1, d), lambda i: (0, 0)),
        ],
        out_specs=rows(d),
        compiler_params=_params("parallel"),
        name="merge",
    )(h, y_rw, y_ml, y_dsa, w_gate, w_branch, w_out, gain_pre.reshape(1, d), gain_post.reshape(1, d))


def _split_w_in(w):
    rw_end = RW_COLS
    ml_end = rw_end + 4 * BRANCH_W + 2 * ML_HEADS
    dsa_end = ml_end + BRANCH_W + DSA_LATENT + IDX_HEADS * IDX_HEAD + IDX_HEAD + IDX_HEADS
    w_rw, w_ml, w_dsa, w_gate = w[:, :rw_end], w[:, rw_end:ml_end], w[:, ml_end:dsa_end], w[:, dsa_end:]
    w_ml = jnp.pad(w_ml, ((0, 0), (0, ML_COLS_PAD - w_ml.shape[1])))
    q, c_kv, rest = w_dsa[:, :BRANCH_W], w_dsa[:, BRANCH_W:BRANCH_W + DSA_LATENT], w_dsa[:, BRANCH_W + DSA_LATENT:]
    q_idx, tail = rest[:, :IDX_HEADS * IDX_HEAD], rest[:, IDX_HEADS * IDX_HEAD:]
    w_dsa = jnp.concatenate([q, q_idx, c_kv, tail], axis=1)
    w_dsa = jnp.pad(w_dsa, ((0, 0), (0, DSA_COLS_PAD - w_dsa.shape[1])))
    return tuple(x.astype(BF16) for x in (w_rw, w_ml, w_dsa, w_gate))


def kernel(x, meta_tokens, norm_gain, ffn_w_in, ffn_w_out, w_in, rw_mu, rw_w0, rw_w_up, rw_a0, rw_a_up, rw_g_up, rw_k_k, rw_k_a, rw_r_k, rw_gn_gain, rw_gn_bias, ml_conv_w, ml_conv_b, ml_i_bias, ml_f_bias, ml_norm_gain, dsa_kv_norm, dsa_w_uk, dsa_w_uv, w_branch, w_out):
    bsz, seq, d = x.shape
    depth = norm_gain.shape[0]
    top_k = min(TOPK_MAX, seq // 4)
    t_len = seq + N_META
    t_pad = -(-t_len // SEQ_PAD_MULTIPLE) * SEQ_PAD_MULTIPLE
    h = jnp.concatenate([
        jnp.broadcast_to(meta_tokens.astype(x.dtype)[None], (bsz, N_META, d)), x,
        jnp.zeros((bsz, t_pad - t_len, d), x.dtype)], axis=1).reshape(bsz * t_pad, d)

    for l in range(depth):
        g = norm_gain[l]
        h = _ffn(h, g[0], g[1], ffn_w_in[l, 0].astype(BF16), ffn_w_out[l, 0].astype(BF16))

        w_rw, w_ml, w_dsa, w_gate = _split_w_in(w_in[l])
        p_rw, p_ml, p_dsa = (p.reshape(bsz, t_pad, -1) for p in _proj(h, g[2], (w_rw, w_ml, w_dsa)))

        y_rw = _rwkv(p_rw, rw_mu[l], rw_w0[l], rw_w_up[l], rw_a0[l], rw_a_up[l], rw_g_up[l], rw_k_k[l],
                     rw_k_a[l], rw_r_k[l], rw_gn_gain[l], rw_gn_bias[l])
        gate_bias = jnp.pad(jnp.concatenate([ml_i_bias[l], ml_f_bias[l]]), (0, 128 - 2 * ML_HEADS))
        y_ml = _mlstm(p_ml, ml_conv_w[l], ml_conv_b[l], gate_bias.reshape(1, 128), ml_norm_gain[l])
        y_dsa = _dsa(p_dsa, dsa_kv_norm[l], jnp.swapaxes(dsa_w_uk[l], 1, 2).astype(BF16), dsa_w_uv[l].astype(BF16), top_k)

        flat = lambda y: y.reshape(bsz * t_pad, BRANCH_W)
        h = _merge(h, flat(y_rw), flat(y_ml), flat(y_dsa), w_gate, w_branch[l].astype(BF16),
                   w_out[l].astype(BF16), g[2], g[3])
        h = _ffn(h, g[4], g[5], ffn_w_in[l, 1].astype(BF16), ffn_w_out[l, 1].astype(BF16))

    return h.reshape(bsz, t_pad, d)[:, N_META:t_len]
```

```python
import functools
import math

import jax
import jax.numpy as jnp
from jax import lax
from jax.experimental import pallas as pl
from jax.experimental.pallas import tpu as pltpu

F32 = jnp.float32
BF16 = jnp.bfloat16
HI = lax.Precision.HIGHEST

D_MODEL = 1024
D_FF = 2816
N_META = 16
STREAM_CHUNK = 64
BRANCH_W = 512
NORM_EPS = 1e-6

RW_HEADS, RW_HEAD = 8, 64
RW_COLS = 1792
RW_GN_EPS = 64e-5
RW_CHUNK = 64

ML_HEADS, ML_HEAD = 4, 128
ML_CHUNK = 64
ML_CONV = 4
ML_COLS_PAD = 2176

DSA_HEADS, DSA_HEAD, DSA_LATENT = 8, 64, 128
IDX_HEADS, IDX_HEAD = 8, 64
TOPK_MAX = 256
DSA_QBLOCK = 128
DSA_COLS_PAD = 1280
DSA_KEY_COL_BLOCK = 256

ROW_TILE = 512
MXU_WIDTH = 256
FF_SLICES = ((0, 6 * MXU_WIDTH), (6 * MXU_WIDTH, D_FF))
SEQ_PAD_MULTIPLE = 128
RW_SEQ_GROUP = 2
ML_SEQ_GROUP = 4
VMEM_LIMIT = 56 * 1024 * 1024
DSA_KEY_CHUNK = 512
DSA_SUM_ROWS = 16
DSA_BISECT_STEPS = 16
DSA_BISECT_REFINE = 4
DSA_BISECT_EXTRA_ROUNDS = 2
DSA_BISECT_MAX_ROUNDS = 80


def _dot(a, b):
    return jnp.dot(a.astype(BF16), b.astype(BF16), preferred_element_type=F32)


def _dot_nt(a, b):
    return lax.dot_general(a.astype(BF16), b.astype(BF16), (((1,), (1,)), ((), ())),
                           preferred_element_type=F32)


def _dot_tn(a, b):
    return lax.dot_general(a.astype(BF16), b.astype(BF16), (((0,), (0,)), ((), ())),
                           preferred_element_type=F32)


def _dot_hi(a, b):
    return jnp.dot(a, b, preferred_element_type=F32, precision=HI)


def _rms(x, gain):
    return x * lax.rsqrt(jnp.mean(x * x, axis=-1, keepdims=True) + NORM_EPS) * gain


def _sigmoid(x):
    return jax.nn.sigmoid(x)


def _tril(n, strict=False):
    r = lax.broadcasted_iota(jnp.int32, (n, n), 0)
    c = lax.broadcasted_iota(jnp.int32, (n, n), 1)
    return (c < r) if strict else (c <= r)


def _params(*sem):
    return pltpu.CompilerParams(dimension_semantics=sem, vmem_limit_bytes=VMEM_LIMIT)


def _ffn_kernel(h_ref, gpre_ref, gpost_ref, wi_ref, wo_ref, o_ref):
    h = h_ref[...]
    xn = _rms(h, gpre_ref[...]).astype(BF16)
    out = None
    for lo, hi in FF_SLICES:
        gate = jnp.dot(xn, wi_ref[:, lo:hi], preferred_element_type=F32)
        up = jnp.dot(xn, wi_ref[:, D_FF + lo:D_FF + hi], preferred_element_type=F32)
        act = (gate * _sigmoid(gate) * up).astype(BF16)
        part = jnp.dot(act, wo_ref[lo:hi, :], preferred_element_type=F32)
        out = part if out is None else out + part
    o_ref[...] = h + 0.5 * _rms(out, gpost_ref[...])


def _ffn(h, g_pre, g_post, w_in, w_out):
    m, d = h.shape
    resident = lambda shape: pl.BlockSpec(shape, lambda i: (0, 0), pipeline_mode=pl.Buffered(1))
    return pl.pallas_call(
        _ffn_kernel,
        out_shape=jax.ShapeDtypeStruct((m, d), F32),
        grid=(m // ROW_TILE,),
        in_specs=[
            pl.BlockSpec((ROW_TILE, d), lambda i: (i, 0)),
            pl.BlockSpec((1, d), lambda i: (0, 0)),
            pl.BlockSpec((1, d), lambda i: (0, 0)),
            resident(w_in.shape),
            resident(w_out.shape),
        ],
        out_specs=pl.BlockSpec((ROW_TILE, d), lambda i: (i, 0)),
        compiler_params=_params("parallel"),
        name="ffn",
    )(h, g_pre.reshape(1, d), g_post.reshape(1, d), w_in, w_out)


def _proj_kernel(h_ref, g_ref, *refs):
    w_refs, o_refs = refs[:len(refs) // 2], refs[len(refs) // 2:]
    xn = _rms(h_ref[...], g_ref[...]).astype(BF16)
    for w_ref, o_ref in zip(w_refs, o_refs):
        o_ref[...] = jnp.dot(xn, w_ref[...], preferred_element_type=F32)


def _proj(h, gain, weights):
    m, d = h.shape
    return pl.pallas_call(
        _proj_kernel,
        out_shape=[jax.ShapeDtypeStruct((m, w.shape[1]), F32) for w in weights],
        grid=(m // ROW_TILE,),
        in_specs=[
            pl.BlockSpec((ROW_TILE, d), lambda i: (i, 0)),
            pl.BlockSpec((1, d), lambda i: (0, 0)),
        ] + [pl.BlockSpec(w.shape, lambda i: (0, 0)) for w in weights],
        out_specs=[pl.BlockSpec((ROW_TILE, w.shape[1]), lambda i: (i, 0)) for w in weights],
        compiler_params=_params("parallel"),
        name="in_proj",
    )(h, gain.reshape(1, d), *weights)


def _rwkv_kernel(p_ref, mu_ref, w0_ref, wup_ref, a0_ref, aup_ref, gup_ref, kk_ref, ka_ref, rk_ref,
                 gng_ref, gnb_ref, hs_ref, o_ref, prev_ref, s_ref):
    @pl.when(pl.program_id(1) == 0)
    def _():
        prev_ref[...] = jnp.zeros_like(prev_ref)
        s_ref[...] = jnp.zeros_like(s_ref)

    L = RW_CHUNK
    lower = _tril(L)
    strict = _tril(L, strict=True)
    tril_f = lower.astype(F32)
    row = lax.broadcasted_iota(jnp.int32, (L, 1), 0)
    steps = int(math.log2(L))
    seqs = range(p_ref.shape[0])

    def head_sum(x):
        return _dot(x, hs_ref[...])

    r, k2, v, gate, g_last, r_rows, a_rows, b_rows, k_rows, v_b = ([] for _ in range(10))
    for g in seqs:
        p = p_ref[g]
        prev = jnp.where(row == 0, prev_ref[g], pltpu.roll(p, 1, 0))
        prev_ref[g] = p[L - 1:L, :]
        ps = p + (prev - p) * mu_ref[...]
        k = ps[:, BRANCH_W:2 * BRANCH_W]
        d = w0_ref[...] + _dot(jnp.tanh(ps[:, 1536:1600]), wup_ref[...])
        log_w = -math.exp(-0.5) * _sigmoid(d)
        a = _sigmoid(a0_ref[...] + _dot(ps[:, 1600:1664], aup_ref[...]))
        gate.append(_dot(_sigmoid(ps[:, 1664:1792]), gup_ref[...]))
        r.append(ps[:, 0:BRANCH_W])
        v.append(ps[:, 2 * BRANCH_W:3 * BRANCH_W])
        kappa = k * kk_ref[...]
        kh = kappa / jnp.maximum(jnp.sqrt(head_sum(kappa * kappa)), 1e-12)
        k2.append(k * (1.0 + (a - 1.0) * ka_ref[...]))
        cum = _dot_hi(tril_f, log_w)
        g_incl = jnp.exp(cum)
        g_inv = jnp.exp(-cum)
        g_last.append(g_incl[L - 1:L, :])
        r_rows.append((r[g] * g_incl).astype(BF16))
        a_rows.append((-kh * jnp.exp(cum - log_w)).astype(BF16))
        b_rows.append((a * kh * g_inv).astype(BF16))
        k_rows.append((k2[g] * g_inv).astype(BF16))
        v_b.append(v[g].astype(BF16))

    units = [(g, h) for g in seqs for h in range(RW_HEADS)]
    idx = range(len(units))
    sl = [slice(h * RW_HEAD, (h + 1) * RW_HEAD) for _, h in units]
    lhs = [jnp.concatenate([r_rows[g][:, sl[i]], a_rows[g][:, sl[i]]], axis=0) for i, (g, h) in enumerate(units)]
    b_u = [b_rows[g][:, sl[i]] for i, (g, h) in enumerate(units)]
    k_u = [k_rows[g][:, sl[i]] for i, (g, h) in enumerate(units)]
    v_u = [v_b[g][:, sl[i]] for i, (g, h) in enumerate(units)]
    s0 = [s_ref[g * RW_HEADS + h] for g, h in units]
    gram = [_dot_nt(lhs[i], jnp.concatenate([b_u[i], k_u[i]], axis=0)) for i in idx]
    ls0 = [_dot_nt(lhs[i], s0[i]) for i in idx]
    u = [ls0[i][L:] + _dot(jnp.where(strict, gram[i][L:, L:], 0.0), v_u[i]) for i in idx]
    n_pow = [jnp.where(strict, gram[i][L:, :L], 0.0) for i in idx]
    for step in range(steps):
        u = [u[i] + _dot(n_pow[i], u[i]) for i in idx]
        if step + 1 < steps:
            n_pow = [_dot(n_pow[i], n_pow[i]) for i in idx]

    y_u = [ls0[i][:L] + _dot(jnp.where(lower, gram[i][:L, :L], 0.0), u[i])
           + _dot(jnp.where(lower, gram[i][:L, L:], 0.0), v_u[i]) for i in idx]
    for i, (g, h) in enumerate(units):
        s_ref[g * RW_HEADS + h] = (s0[i] + _dot_tn(u[i], b_u[i]) + _dot_tn(v_u[i], k_u[i])) * g_last[g][:, sl[i]]

    for g in seqs:
        y = jnp.concatenate(y_u[g * RW_HEADS:(g + 1) * RW_HEADS], axis=1)
        dev = y - head_sum(y) * (1.0 / RW_HEAD)
        var = head_sum(dev * dev) * (1.0 / RW_HEAD)
        y = dev * lax.rsqrt(var + RW_GN_EPS) * gng_ref[...] + gnb_ref[...]
        bonus = head_sum(r[g] * k2[g] * rk_ref[...])
        o_ref[g] = (y + bonus * v[g]) * gate[g]


def _rwkv(p, mu, w0, w_up, a0, a_up, g_up, k_k, k_a, r_k, gn_gain, gn_bias):
    b, t, c = p.shape
    group = math.gcd(b, RW_SEQ_GROUP)
    head_of = jnp.arange(BRANCH_W) // RW_HEAD
    same_head = (head_of[:, None] == head_of[None, :]).astype(BF16)
    row = lambda x: x.reshape(1, -1)
    vec = lambda n: pl.BlockSpec((1, n), lambda i, j: (0, 0))
    mat = lambda shp: pl.BlockSpec(shp, lambda i, j: (0, 0))
    return pl.pallas_call(
        _rwkv_kernel,
        out_shape=jax.ShapeDtypeStruct((b, t, BRANCH_W), F32),
        grid=(b // group, t // RW_CHUNK),
        in_specs=[
            pl.BlockSpec((group, RW_CHUNK, c), lambda i, j: (i, j, 0)),
            vec(c), vec(BRANCH_W), mat(w_up.shape), vec(BRANCH_W), mat(a_up.shape), mat(g_up.shape),
            vec(BRANCH_W), vec(BRANCH_W), vec(BRANCH_W), vec(BRANCH_W), vec(BRANCH_W),
            mat((BRANCH_W, BRANCH_W)),
        ],
        out_specs=pl.BlockSpec((group, RW_CHUNK, BRANCH_W), lambda i, j: (i, j, 0)),
        scratch_shapes=[pltpu.VMEM((group, 1, c), F32),
                        pltpu.VMEM((group * RW_HEADS, RW_HEAD, RW_HEAD), F32)],
        compiler_params=_params("parallel", "arbitrary"),
        name="rwkv7",
    )(p, row(mu), row(w0), w_up, row(a0), a_up, g_up, row(k_k), row(k_a), row(r_k), row(gn_gain),
      row(gn_bias), same_head)


def _log_sigmoid(x):
    return jnp.minimum(x, 0.0) - jnp.log1p(jnp.exp(-jnp.abs(x)))


def _mlstm_kernel(p_ref, cw_ref, cb_ref, gb_ref, ng_ref, o_ref, tail_ref, c_ref, n_ref, m_ref):
    @pl.when(pl.program_id(1) == 0)
    def _():
        tail_ref[...] = jnp.zeros_like(tail_ref)
        c_ref[...] = jnp.zeros_like(c_ref)
        n_ref[...] = jnp.zeros_like(n_ref)
        m_ref[...] = jnp.zeros_like(m_ref)

    L = ML_CHUNK
    qk_w = 2 * BRANCH_W
    lower = _tril(L)
    tril_f = lower.astype(F32)
    seqs = range(p_ref.shape[0])

    x = [p_ref[g] for g in seqs]
    q, k, gates, b_all, b_t, g_t = [], [], [], [], [], []
    for g in seqs:
        qk_in = x[g][:, :qk_w]
        ext = jnp.concatenate([tail_ref[g], qk_in], axis=0)
        tail_ref[g] = qk_in[L - 8:L, :]
        conv = cb_ref[...] + cw_ref[ML_CONV - 1:ML_CONV, :] * qk_in
        for j in range(ML_CONV - 1):
            conv = conv + cw_ref[j:j + 1, :] * pltpu.roll(ext, ML_CONV - 1 - j, 0)[8:8 + L]
        qk = conv * _sigmoid(conv)
        q.append(qk[:, :BRANCH_W])
        k.append(qk[:, BRANCH_W:] * ML_HEAD ** -0.5)
        gates.append(x[g][:, qk_w + 2 * BRANCH_W:] + gb_ref[...])
        b_all.append(_dot_hi(tril_f, _log_sigmoid(gates[g])))
        b_t.append(b_all[g].T)
        g_t.append(gates[g].T)

    units = [(g, h) for g in seqs for h in range(ML_HEADS)]
    sl = [slice(h * ML_HEAD, (h + 1) * ML_HEAD) for _, h in units]
    q_u = [q[g][:, sl[i]] for i, (g, h) in enumerate(units)]
    k_u = [k[g][:, sl[i]] for i, (g, h) in enumerate(units)]
    v_u = [x[g][:, qk_w + h * ML_HEAD:qk_w + (h + 1) * ML_HEAD] for g, h in units]
    b_col = [b_all[g][:, ML_HEADS + h:ML_HEADS + h + 1] for g, h in units]
    b_row = [b_t[g][ML_HEADS + h:ML_HEADS + h + 1, :] for g, h in units]
    i_col = [gates[g][:, h:h + 1] for g, h in units]
    i_row = [g_t[g][h:h + 1, :] for g, h in units]
    m_st = [m_ref[g, h:h + 1, 0:1] for g, h in units]
    n_st = [n_ref[g, h:h + 1, :] for g, h in units]
    c_st = [c_ref[g * ML_HEADS + h] for g, h in units]
    idx = range(len(units))

    log_d = [jnp.where(lower, b_col[i] - b_row[i] + i_row[i], -jnp.inf) for i in idx]
    m_j = [jnp.maximum(b_col[i] + m_st[i], jnp.max(log_d[i], axis=-1, keepdims=True)) for i in idx]
    s = [_dot_nt(q_u[i], k_u[i]) * jnp.exp(log_d[i] - m_j[i]) for i in idx]
    inter = [jnp.exp(b_col[i] + m_st[i] - m_j[i]) for i in idx]
    num = [_dot(s[i], v_u[i]) + inter[i] * _dot_nt(q_u[i], c_st[i]) for i in idx]
    den = [jnp.sum(s[i], axis=-1, keepdims=True)
           + inter[i] * jnp.sum(q_u[i] * n_st[i], axis=-1, keepdims=True) for i in idx]
    hh = [num[i] / jnp.maximum(jnp.abs(den[i]), jnp.exp(-m_j[i])) for i in idx]

    g_tot = [b_col[i][L - 1:L, :] for i in idx]
    w_log = [g_tot[i] - b_col[i] + i_col[i] for i in idx]
    m_new = [jnp.maximum(g_tot[i] + m_st[i], jnp.max(w_log[i], axis=0, keepdims=True)) for i in idx]
    wgt = [jnp.exp(w_log[i] - m_new[i]) for i in idx]
    dec = [jnp.exp(g_tot[i] + m_st[i] - m_new[i]) for i in idx]
    c_new = [dec[i] * c_st[i] + _dot_tn(wgt[i] * v_u[i], k_u[i]) for i in idx]
    n_new = [dec[i] * n_st[i] + jnp.sum(wgt[i] * k_u[i], axis=0, keepdims=True) for i in idx]
    hn = [hh[i] * lax.rsqrt(jnp.mean(hh[i] * hh[i], axis=-1, keepdims=True) + NORM_EPS) * ng_ref[:, sl[i]]
          for i in idx]

    for i, (g, h) in enumerate(units):
        c_ref[g * ML_HEADS + h] = c_new[i]
        n_ref[g, h:h + 1, :] = n_new[i]
        m_ref[g, h:h + 1, :] = jnp.broadcast_to(m_new[i], (1, ML_HEAD))
        o_gate = x[g][:, qk_w + BRANCH_W + h * ML_HEAD:qk_w + BRANCH_W + (h + 1) * ML_HEAD]
        o_ref[g, :, sl[i]] = _sigmoid(o_gate) * hn[i]


def _mlstm(p, conv_w, conv_b, gate_bias, norm_gain):
    b, t, c = p.shape
    group = math.gcd(b, ML_SEQ_GROUP)
    return pl.pallas_call(
        _mlstm_kernel,
        out_shape=jax.ShapeDtypeStruct((b, t, BRANCH_W), F32),
        grid=(b // group, t // ML_CHUNK),
        in_specs=[
            pl.BlockSpec((group, ML_CHUNK, c), lambda i, j: (i, j, 0)),
            pl.BlockSpec(conv_w.shape, lambda i, j: (0, 0)),
            pl.BlockSpec((1, 2 * BRANCH_W), lambda i, j: (0, 0)),
            pl.BlockSpec((1, 128), lambda i, j: (0, 0)),
            pl.BlockSpec((1, BRANCH_W), lambda i, j: (0, 0)),
        ],
        out_specs=pl.BlockSpec((group, ML_CHUNK, BRANCH_W), lambda i, j: (i, j, 0)),
        scratch_shapes=[
            pltpu.VMEM((group, 8, 2 * BRANCH_W), F32),
            pltpu.VMEM((group * ML_HEADS, ML_HEAD, ML_HEAD), F32),
            pltpu.VMEM((group, 8, ML_HEAD), F32),
            pltpu.VMEM((group, 8, ML_HEAD), F32),
        ],
        compiler_params=_params("parallel", "arbitrary"),
        name="mlstm",
    )(p, conv_w, conv_b.reshape(1, -1), gate_bias, norm_gain.reshape(1, -1))


def _sum_sublane_groups(x):
    n = x.shape[0] // 32
    g = x.reshape(4 * n, 8, x.shape[1])
    parts = [g[i * n:(i + 1) * n] for i in range(4)]
    return (jnp.sum(parts[0], axis=0) + jnp.sum(parts[1], axis=0)) + (jnp.sum(parts[2], axis=0) + jnp.sum(parts[3], axis=0))


def _dsa_kernel(pq_ref, pk_ref, kvn_ref, wukt_ref, wuv_ref, o_ref, ckv_ref, ckvt_ref, kidx_ref, sc_ref,
                *, top_k):
    qb = pl.program_id(1)
    tq = DSA_QBLOCK
    kc = DSA_KEY_CHUNK
    tk = pk_ref.shape[1]
    n_chunks_max = ckvt_ref.shape[0]
    hq = DSA_HEADS * tq

    @pl.when(qb == 0)
    def _():
        keys = pk_ref[0]
        ckv = _rms(keys[:, :DSA_LATENT], kvn_ref[...])
        pad_rows = n_chunks_max * kc - tk
        ckv_ref[0:tk, :] = ckv.astype(BF16)
        ckv_ref[tk:, :] = jnp.zeros((pad_rows, DSA_LATENT), BF16)
        kidx_ref[0:tk, :] = keys[:, DSA_LATENT:DSA_LATENT + IDX_HEAD].astype(BF16)
        kidx_ref[tk:, :] = jnp.zeros((pad_rows, IDX_HEAD), BF16)
        ckv_t = jnp.concatenate([ckv.T, jnp.zeros((DSA_LATENT, pad_rows), F32)], axis=1)
        extra = jnp.where(lax.broadcasted_iota(jnp.int32, (DSA_SUM_ROWS, kc), 0) == 0, 1.0, 0.0)
        for c in range(n_chunks_max):
            ckvt_ref[c] = jnp.concatenate([ckv_t[:, c * kc:(c + 1) * kc], extra], axis=0).astype(BF16)

    n_chunks = jnp.minimum(lax.shift_right_logical(qb * tq + tq + N_META + kc - 1, int(math.log2(kc))),
                           n_chunks_max)
    q_pos = qb * tq + lax.broadcasted_iota(jnp.int32, (1, tq), 1)
    q_chunk = jnp.where(q_pos < N_META, 0,
                        1 + lax.shift_right_arithmetic(q_pos - N_META, int(math.log2(STREAM_CHUNK))))
    n_allowed = jnp.minimum(N_META + STREAM_CHUNK * q_chunk, tk)

    def rows_of(c):
        return pl.ds(pl.multiple_of(c * kc, kc), kc)

    pq = pq_ref[0]
    q = pq[:, :BRANCH_W]
    qi_t = jnp.concatenate([pq[:, BRANCH_W + j * 128:BRANCH_W + (j + 1) * 128].T
                            for j in range(IDX_HEADS * IDX_HEAD // 128)], axis=0)
    qi_t = jnp.concatenate([qi_t[h * IDX_HEAD:(h + 1) * IDX_HEAD] for h in range(IDX_HEADS)], axis=1)
    tail_t = pq[:, DSA_COLS_PAD - 128:].T
    w_t = tail_t[IDX_HEAD:IDX_HEAD + IDX_HEADS] * (IDX_HEADS * IDX_HEAD) ** -0.5
    w_row = jnp.concatenate([w_t[h:h + 1] for h in range(IDX_HEADS)], axis=1)
    qi_t = qi_t.astype(BF16)

    def score_chunk(c, bounds):
        lo, hi = bounds
        s = jnp.maximum(jnp.dot(kidx_ref[rows_of(c), :], qi_t, preferred_element_type=F32), 0.0) * w_row
        score = s[:, 0:tq]
        for h in range(1, IDX_HEADS):
            score = score + s[:, h * tq:(h + 1) * tq]
        k_pos = c * kc + lax.broadcasted_iota(jnp.int32, (kc, 1), 0)
        allowed = k_pos < n_allowed
        sc_ref[rows_of(c), :] = jnp.where(allowed, score, -jnp.inf)
        lo = jnp.minimum(lo, jnp.min(jnp.where(allowed, score, jnp.inf), axis=0, keepdims=True))
        hi = jnp.maximum(hi, jnp.max(jnp.where(allowed, score, -jnp.inf), axis=0, keepdims=True))
        return lo, hi

    lo, row_max = lax.fori_loop(0, n_chunks, score_chunk,
                                (jnp.full((1, tq), jnp.inf, F32), jnp.full((1, tq), -jnp.inf, F32)))
    hi = row_max + jnp.maximum(jnp.abs(row_max) * 2.0 ** -20, 1e-30)

    def count(pred, level):
        def body(c, acc):
            return acc + _sum_sublane_groups(jnp.where(pred(sc_ref[rows_of(c), :], level), 1.0, 0.0))
        return jnp.sum(lax.fori_loop(0, n_chunks, body, jnp.zeros((8, tq), F32)), axis=0, keepdims=True)

    def bisect(_, bracket):
        lo, hi, n_lo = bracket
        mid = 0.5 * lo + 0.5 * hi
        n_mid = count(jnp.greater_equal, mid)
        ge = n_mid >= top_k
        return jnp.where(ge, mid, lo), jnp.where(ge, hi, mid), jnp.where(ge, n_mid, n_lo)

    def snap(lo):
        def body(c, val):
            s = sc_ref[rows_of(c), :]
            return jnp.minimum(val, jnp.min(jnp.where(s >= lo, s, jnp.inf), axis=0, keepdims=True))
        val = lax.fori_loop(0, n_chunks, body, jnp.full((1, tq), jnp.inf, F32))
        return val, count(jnp.greater, val)

    bracket = (lo, hi, n_allowed.astype(F32))
    bracket = lax.fori_loop(0, DSA_BISECT_STEPS, bisect, bracket)

    def keeps_too_many(state):
        return jnp.logical_and(jnp.max(state[0][2]) > top_k, state[1] < DSA_BISECT_EXTRA_ROUNDS)

    def halve_more(state):
        return lax.fori_loop(0, DSA_BISECT_REFINE, bisect, state[0]), state[1] + 1

    bracket, _ = lax.while_loop(keeps_too_many, halve_more, (bracket, jnp.int32(0)))
    thr, n_above = snap(bracket[0])

    def unsettled(state):
        return jnp.logical_and(jnp.max(state[2]) >= top_k, state[3] < DSA_BISECT_MAX_ROUNDS)

    def refine(state):
        bracket = lax.fori_loop(0, DSA_BISECT_REFINE, bisect, state[0])
        thr, n_above = snap(bracket[0])
        return bracket, thr, n_above, state[3] + 1

    bracket, thr, n_above, _ = lax.while_loop(unsettled, refine, (bracket, thr, n_above, jnp.int32(0)))

    @pl.when(jnp.max(bracket[2]) > top_k)
    def _():
        need = top_k - n_above
        earlier = (lax.broadcasted_iota(jnp.int32, (kc, kc), 1)
                   < lax.broadcasted_iota(jnp.int32, (kc, kc), 0)).astype(BF16)

        def body(c, seen):
            s = sc_ref[rows_of(c), :]
            tie = jnp.where(s == thr, 1.0, 0.0)
            rank = seen + jnp.dot(earlier, tie.astype(BF16), preferred_element_type=F32)
            sc_ref[rows_of(c), :] = jnp.where((tie > 0.0) & (rank >= need), -jnp.inf, s)
            return seen + jnp.sum(tie, axis=0, keepdims=True)

        lax.fori_loop(0, n_chunks, body, jnp.zeros((1, tq), F32))

    q_t = jnp.concatenate([_dot_nt(wukt_ref[h], q[:, h * DSA_HEAD:(h + 1) * DSA_HEAD])
                           for h in range(DSA_HEADS)], axis=1)
    q_t = (q_t * (DSA_HEAD ** -0.5 * math.log2(math.e))).astype(BF16)

    def finite_or_zero(m):
        return jnp.where(m == -jnp.inf, 0.0, m)

    def attend(chunks, m, acc):
        n = range(len(chunks))
        logits = [jnp.dot(ckv_ref[rows_of(c), :], q_t, preferred_element_type=F32)
                  + jnp.concatenate([jnp.where(sc_ref[rows_of(c), :] >= thr, 0.0, -jnp.inf)] * DSA_HEADS, axis=1)
                  for c in chunks]
        m_new = [jnp.maximum(m[s], jnp.max(logits[s], axis=0, keepdims=True)) for s in n]
        m_safe = [finite_or_zero(m_new[s]) for s in n]
        e = [jnp.exp2(logits[s] - m_safe[s]).astype(BF16) for s in n]
        acc = [acc[s] * jnp.exp2(m[s] - m_safe[s])
               + jnp.dot(ckvt_ref[chunks[s]], e[s], preferred_element_type=F32) for s in n]
        return m_new, acc

    def attend_pair(i, carry):
        m, acc = attend([2 * i, 2 * i + 1], carry[:2], carry[2:])
        return m[0], m[1], acc[0], acc[1]

    def attend_last(_, carry):
        m, acc = attend([n_chunks - 1], carry[:1], carry[2:3])
        return m[0], carry[1], acc[0], carry[3]

    m_init = jnp.full((1, hq), -jnp.inf, F32)
    acc_init = jnp.zeros((DSA_LATENT + DSA_SUM_ROWS, hq), F32)
    streams = lax.fori_loop(0, lax.shift_right_logical(n_chunks, 1), attend_pair,
                            (m_init, m_init, acc_init, acc_init))
    m0, m1, acc0, acc1 = lax.fori_loop(0, n_chunks & 1, attend_last, streams)
    m_safe = finite_or_zero(jnp.maximum(m0, m1))
    acc = acc0 * jnp.exp2(m0 - m_safe) + acc1 * jnp.exp2(m1 - m_safe)
    out_t = acc[:DSA_LATENT] / acc[DSA_LATENT:DSA_LATENT + 1]
    for h in range(DSA_HEADS):
        o_ref[0, :, h * DSA_HEAD:(h + 1) * DSA_HEAD] = _dot_tn(out_t[:, h * tq:(h + 1) * tq], wuv_ref[h])


def _dsa(p, kv_norm, w_uk_t, w_uv, top_k):
    b, t, c = p.shape
    n_chunks = -(-t // DSA_KEY_CHUNK)
    return pl.pallas_call(
        functools.partial(_dsa_kernel, top_k=top_k),
        out_shape=jax.ShapeDtypeStruct((b, t, BRANCH_W), F32),
        grid=(b, t // DSA_QBLOCK),
        in_specs=[
            pl.BlockSpec((1, DSA_QBLOCK, c), lambda i, j: (i, j, 0)),
            pl.BlockSpec((1, t, DSA_KEY_COL_BLOCK), lambda i, j: (i, 0, c // DSA_KEY_COL_BLOCK - 1)),
            pl.BlockSpec((1, DSA_LATENT), lambda i, j: (0, 0)),
            pl.BlockSpec(w_uk_t.shape, lambda i, j: (0, 0, 0)),
            pl.BlockSpec(w_uv.shape, lambda i, j: (0, 0, 0)),
        ],
        out_specs=pl.BlockSpec((1, DSA_QBLOCK, BRANCH_W), lambda i, j: (i, j, 0)),
        scratch_shapes=[
            pltpu.VMEM((n_chunks * DSA_KEY_CHUNK, DSA_LATENT), BF16),
            pltpu.VMEM((n_chunks, DSA_LATENT + DSA_SUM_ROWS, DSA_KEY_CHUNK), BF16),
            pltpu.VMEM((n_chunks * DSA_KEY_CHUNK, IDX_HEAD), BF16),
            pltpu.VMEM((n_chunks * DSA_KEY_CHUNK, DSA_QBLOCK), F32),
        ],
        compiler_params=_params("parallel", "arbitrary"),
        name="dsa",
    )(p, p, kv_norm.reshape(1, -1), w_uk_t, w_uv)


def _merge_kernel(h_ref, yr_ref, ym_ref, yd_ref, wg_ref, wb_ref, wo_ref, gpre_ref, gpost_ref, o_ref):
    h = h_ref[...]
    xn = _rms(h, gpre_ref[...]).astype(BF16)
    mixed = None
    for i, y_ref in enumerate((yr_ref, ym_ref, yd_ref)):
        gate = jnp.dot(xn, wg_ref[:, i * D_MODEL:(i + 1) * D_MODEL], preferred_element_type=F32)
        proj = jnp.dot(y_ref[...].astype(BF16), wb_ref[i], preferred_element_type=F32)
        term = _sigmoid(gate) * proj
        mixed = term if mixed is None else mixed + term
    out = jnp.dot(mixed.astype(BF16), wo_ref[...], preferred_element_type=F32)
    o_ref[...] = h + _rms(out, gpost_ref[...])


def _merge(h, y_rw, y_ml, y_dsa, w_gate, w_branch, w_out, gain_pre, gain_post):
    m, d = h.shape
    rows = lambda n: pl.BlockSpec((ROW_TILE, n), lambda i: (i, 0))
    return pl.pallas_call(
        _merge_kernel,
        out_shape=jax.ShapeDtypeStruct((m, d), F32),
        grid=(m // ROW_TILE,),
        in_specs=[
            rows(d), rows(BRANCH_W), rows(BRANCH_W), rows(BRANCH_W),
            pl.BlockSpec(w_gate.shape, lambda i: (0, 0)),
            pl.BlockSpec(w_branch.shape, lambda i: (0, 0, 0)),
            pl.BlockSpec(w_out.shape, lambda i: (0, 0)),
            pl.BlockSpec((1, d), lambda i: (0, 0)),
            pl.BlockSpec((1, d), lambda i: (0, 0)),
        ],
        out_specs=rows(d),
        compiler_params=_params("parallel"),
        name="merge",
    )(h, y_rw, y_ml, y_dsa, w_gate, w_branch, w_out, gain_pre.reshape(1, d), gain_post.reshape(1, d))


def _split_w_in(w):
    rw_end = RW_COLS
    ml_end = rw_end + 4 * BRANCH_W + 2 * ML_HEADS
    dsa_end = ml_end + BRANCH_W + DSA_LATENT + IDX_HEADS * IDX_HEAD + IDX_HEAD + IDX_HEADS
    w_rw, w_ml, w_dsa, w_gate = w[:, :rw_end], w[:, rw_end:ml_end], w[:, ml_end:dsa_end], w[:, dsa_end:]
    w_ml = jnp.pad(w_ml, ((0, 0), (0, ML_COLS_PAD - w_ml.shape[1])))
    q, c_kv, rest = w_dsa[:, :BRANCH_W], w_dsa[:, BRANCH_W:BRANCH_W + DSA_LATENT], w_dsa[:, BRANCH_W + DSA_LATENT:]
    q_idx, tail = rest[:, :IDX_HEADS * IDX_HEAD], rest[:, IDX_HEADS * IDX_HEAD:]
    w_dsa = jnp.concatenate([q, q_idx, c_kv, tail], axis=1)
    w_dsa = jnp.pad(w_dsa, ((0, 0), (0, DSA_COLS_PAD - w_dsa.shape[1])))
    return tuple(x.astype(BF16) for x in (w_rw, w_ml, w_dsa, w_gate))


def kernel(x, meta_tokens, norm_gain, ffn_w_in, ffn_w_out, w_in, rw_mu, rw_w0, rw_w_up, rw_a0, rw_a_up, rw_g_up, rw_k_k, rw_k_a, rw_r_k, rw_gn_gain, rw_gn_bias, ml_conv_w, ml_conv_b, ml_i_bias, ml_f_bias, ml_norm_gain, dsa_kv_norm, dsa_w_uk, dsa_w_uv, w_branch, w_out):
    bsz, seq, d = x.shape
    depth = norm_gain.shape[0]
    top_k = min(TOPK_MAX, seq // 4)
    t_len = seq + N_META
    t_pad = -(-t_len // SEQ_PAD_MULTIPLE) * SEQ_PAD_MULTIPLE
    h = jnp.concatenate([
        jnp.broadcast_to(meta_tokens.astype(x.dtype)[None], (bsz, N_META, d)), x,
        jnp.zeros((bsz, t_pad - t_len, d), x.dtype)], axis=1).reshape(bsz * t_pad, d)

    for l in range(depth):
        g = norm_gain[l]
        h = _ffn(h, g[0], g[1], ffn_w_in[l, 0].astype(BF16), ffn_w_out[l, 0].astype(BF16))

        w_rw, w_ml, w_dsa, w_gate = _split_w_in(w_in[l])
        p_rw, p_ml, p_dsa = (p.reshape(bsz, t_pad, -1) for p in _proj(h, g[2], (w_rw, w_ml, w_dsa)))

        y_rw = _rwkv(p_rw, rw_mu[l], rw_w0[l], rw_w_up[l], rw_a0[l], rw_a_up[l], rw_g_up[l], rw_k_k[l],
                     rw_k_a[l], rw_r_k[l], rw_gn_gain[l], rw_gn_bias[l])
        gate_bias = jnp.pad(jnp.concatenate([ml_i_bias[l], ml_f_bias[l]]), (0, 128 - 2 * ML_HEADS))
        y_ml = _mlstm(p_ml, ml_conv_w[l], ml_conv_b[l], gate_bias.reshape(1, 128), ml_norm_gain[l])
        y_dsa = _dsa(p_dsa, dsa_kv_norm[l], jnp.swapaxes(dsa_w_uk[l], 1, 2).astype(BF16), dsa_w_uv[l].astype(BF16), top_k)

        flat = lambda y: y.reshape(bsz * t_pad, BRANCH_W)
        h = _merge(h, flat(y_rw), flat(y_ml), flat(y_dsa), w_gate, w_branch[l].astype(BF16),
                   w_out[l].astype(BF16), g[2], g[3])
        h = _ffn(h, g[4], g[5], ffn_w_in[l, 1].astype(BF16), ffn_w_out[l, 1].astype(BF16))

    return h.reshape(bsz, t_pad, d)[:, N_META:t_len]
```

```python
import functools
import math

import jax
import jax.numpy as jnp
from jax import lax
from jax.experimental import pallas as pl
from jax.experimental.pallas import tpu as pltpu

F32 = jnp.float32
BF16 = jnp.bfloat16
HI = lax.Precision.HIGHEST

D_MODEL = 1024
D_FF = 2816
N_META = 16
STREAM_CHUNK = 64
BRANCH_W = 512
NORM_EPS = 1e-6

RW_HEADS, RW_HEAD = 8, 64
RW_COLS = 1792
RW_GN_EPS = 64e-5
RW_CHUNK = 64

ML_HEADS, ML_HEAD = 4, 128
ML_CHUNK = 64
ML_CONV = 4
ML_COLS_PAD = 2176

DSA_HEADS, DSA_HEAD, DSA_LATENT = 8, 64, 128
IDX_HEADS, IDX_HEAD = 8, 64
TOPK_MAX = 256
DSA_QBLOCK = 128
DSA_COLS_PAD = 1280
DSA_KEY_COL_BLOCK = 256

ROW_TILE = 512
MXU_WIDTH = 256
FF_SLICES = ((0, 6 * MXU_WIDTH), (6 * MXU_WIDTH, D_FF))
SEQ_PAD_MULTIPLE = 128
RW_SEQ_GROUP = 2
ML_SEQ_GROUP = 4
VMEM_LIMIT = 56 * 1024 * 1024
DSA_KEY_CHUNK = 512
DSA_SUM_ROWS = 16
DSA_BISECT_STEPS = 16
DSA_BISECT_REFINE = 4
DSA_BISECT_EXTRA_ROUNDS = 2
DSA_BISECT_MAX_ROUNDS = 80


def _dot(a, b):
    return jnp.dot(a.astype(BF16), b.astype(BF16), preferred_element_type=F32)


def _dot_nt(a, b):
    return lax.dot_general(a.astype(BF16), b.astype(BF16), (((1,), (1,)), ((), ())),
                           preferred_element_type=F32)


def _dot_tn(a, b):
    return lax.dot_general(a.astype(BF16), b.astype(BF16), (((0,), (0,)), ((), ())),
                           preferred_element_type=F32)


def _dot_hi(a, b):
    return jnp.dot(a, b, preferred_element_type=F32, precision=HI)


def _rms(x, gain):
    return x * lax.rsqrt(jnp.mean(x * x, axis=-1, keepdims=True) + NORM_EPS) * gain


def _sigmoid(x):
    return jax.nn.sigmoid(x)


def _tril(n, strict=False):
    r = lax.broadcasted_iota(jnp.int32, (n, n), 0)
    c = lax.broadcasted_iota(jnp.int32, (n, n), 1)
    return (c < r) if strict else (c <= r)


def _params(*sem):
    return pltpu.CompilerParams(dimension_semantics=sem, vmem_limit_bytes=VMEM_LIMIT)


def _ffn_kernel(h_ref, gpre_ref, gpost_ref, wi_ref, wo_ref, o_ref):
    h = h_ref[...]
    xn = _rms(h, gpre_ref[...]).astype(BF16)
    out = None
    for lo, hi in FF_SLICES:
        gate = jnp.dot(xn, wi_ref[:, lo:hi], preferred_element_type=F32)
        up = jnp.dot(xn, wi_ref[:, D_FF + lo:D_FF + hi], preferred_element_type=F32)
        act = (gate * _sigmoid(gate) * up).astype(BF16)
        part = jnp.dot(act, wo_ref[lo:hi, :], preferred_element_type=F32)
        out = part if out is None else out + part
    o_ref[...] = h + 0.5 * _rms(out, gpost_ref[...])


def _ffn(h, g_pre, g_post, w_in, w_out):
    m, d = h.shape
    resident = lambda shape: pl.BlockSpec(shape, lambda i: (0, 0), pipeline_mode=pl.Buffered(1))
    return pl.pallas_call(
        _ffn_kernel,
        out_shape=jax.ShapeDtypeStruct((m, d), F32),
        grid=(m // ROW_TILE,),
        in_specs=[
            pl.BlockSpec((ROW_TILE, d), lambda i: (i, 0)),
            pl.BlockSpec((1, d), lambda i: (0, 0)),
            pl.BlockSpec((1, d), lambda i: (0, 0)),
            resident(w_in.shape),
            resident(w_out.shape),
        ],
        out_specs=pl.BlockSpec((ROW_TILE, d), lambda i: (i, 0)),
        compiler_params=_params("parallel"),
        name="ffn",
    )(h, g_pre.reshape(1, d), g_post.reshape(1, d), w_in, w_out)


def _proj_kernel(h_ref, g_ref, *refs):
    w_refs, o_refs = refs[:len(refs) // 2], refs[len(refs) // 2:]
    xn = _rms(h_ref[...], g_ref[...]).astype(BF16)
    for w_ref, o_ref in zip(w_refs, o_refs):
        o_ref[...] = jnp.dot(xn, w_ref[...], preferred_element_type=F32)


def _proj(h, gain, weights):
    m, d = h.shape
    return pl.pallas_call(
        _proj_kernel,
        out_shape=[jax.ShapeDtypeStruct((m, w.shape[1]), F32) for w in weights],
        grid=(m // ROW_TILE,),
        in_specs=[
            pl.BlockSpec((ROW_TILE, d), lambda i: (i, 0)),
            pl.BlockSpec((1, d), lambda i: (0, 0)),
        ] + [pl.BlockSpec(w.shape, lambda i: (0, 0)) for w in weights],
        out_specs=[pl.BlockSpec((ROW_TILE, w.shape[1]), lambda i: (i, 0)) for w in weights],
        compiler_params=_params("parallel"),
        name="in_proj",
    )(h, gain.reshape(1, d), *weights)


def _rwkv_kernel(p_ref, mu_ref, w0_ref, wup_ref, a0_ref, aup_ref, gup_ref, kk_ref, ka_ref, rk_ref,
                 gng_ref, gnb_ref, hs_ref, o_ref, prev_ref, s_ref):
    @pl.when(pl.program_id(1) == 0)
    def _():
        prev_ref[...] = jnp.zeros_like(prev_ref)
        s_ref[...] = jnp.zeros_like(s_ref)

    L = RW_CHUNK
    lower = _tril(L)
    strict = _tril(L, strict=True)
    tril_f = lower.astype(F32)
    row = lax.broadcasted_iota(jnp.int32, (L, 1), 0)
    steps = int(math.log2(L))
    seqs = range(p_ref.shape[0])

    def head_sum(x):
        return _dot(x, hs_ref[...])

    r, k2, v, gate, g_last, r_rows, a_rows, b_rows, k_rows, v_b = ([] for _ in range(10))
    for g in seqs:
        p = p_ref[g]
        prev = jnp.where(row == 0, prev_ref[g], pltpu.roll(p, 1, 0))
        prev_ref[g] = p[L - 1:L, :]
        ps = p + (prev - p) * mu_ref[...]
        k = ps[:, BRANCH_W:2 * BRANCH_W]
        d = w0_ref[...] + _dot(jnp.tanh(ps[:, 1536:1600]), wup_ref[...])
        log_w = -math.exp(-0.5) * _sigmoid(d)
        a = _sigmoid(a0_ref[...] + _dot(ps[:, 1600:1664], aup_ref[...]))
        gate.append(_dot(_sigmoid(ps[:, 1664:1792]), gup_ref[...]))
        r.append(ps[:, 0:BRANCH_W])
        v.append(ps[:, 2 * BRANCH_W:3 * BRANCH_W])
        kappa = k * kk_ref[...]
        kh = kappa / jnp.maximum(jnp.sqrt(head_sum(kappa * kappa)), 1e-12)
        k2.append(k * (1.0 + (a - 1.0) * ka_ref[...]))
        cum = _dot_hi(tril_f, log_w)
        g_incl = jnp.exp(cum)
        g_inv = jnp.exp(-cum)
        g_last.append(g_incl[L - 1:L, :])
        r_rows.append((r[g] * g_incl).astype(BF16))
        a_rows.append((-kh * jnp.exp(cum - log_w)).astype(BF16))
        b_rows.append((a * kh * g_inv).astype(BF16))
        k_rows.append((k2[g] * g_inv).astype(BF16))
        v_b.append(v[g].astype(BF16))

    units = [(g, h) for g in seqs for h in range(RW_HEADS)]
    idx = range(len(units))
    sl = [slice(h * RW_HEAD, (h + 1) * RW_HEAD) for _, h in units]
    lhs = [jnp.concatenate([r_rows[g][:, sl[i]], a_rows[g][:, sl[i]]], axis=0) for i, (g, h) in enumerate(units)]
    b_u = [b_rows[g][:, sl[i]] for i, (g, h) in enumerate(units)]
    k_u = [k_rows[g][:, sl[i]] for i, (g, h) in enumerate(units)]
    v_u = [v_b[g][:, sl[i]] for i, (g, h) in enumerate(units)]
    s0 = [s_ref[g * RW_HEADS + h] for g, h in units]
    gram = [_dot_nt(lhs[i], jnp.concatenate([b_u[i], k_u[i]], axis=0)) for i in idx]
    ls0 = [_dot_nt(lhs[i], s0[i]) for i in idx]
    u = [ls0[i][L:] + _dot(jnp.where(strict, gram[i][L:, L:], 0.0), v_u[i]) for i in idx]
    n_pow = [jnp.where(strict, gram[i][L:, :L], 0.0) for i in idx]
    for step in range(steps):
        u = [u[i] + _dot(n_pow[i], u[i]) for i in idx]
        if step + 1 < steps:
            n_pow = [_dot(n_pow[i], n_pow[i]) for i in idx]

    y_u = [ls0[i][:L] + _dot(jnp.where(lower, gram[i][:L, :L], 0.0), u[i])
           + _dot(jnp.where(lower, gram[i][:L, L:], 0.0), v_u[i]) for i in idx]
    for i, (g, h) in enumerate(units):
        s_ref[g * RW_HEADS + h] = (s0[i] + _dot_tn(u[i], b_u[i]) + _dot_tn(v_u[i], k_u[i])) * g_last[g][:, sl[i]]

    for g in seqs:
        y = jnp.concatenate(y_u[g * RW_HEADS:(g + 1) * RW_HEADS], axis=1)
        dev = y - head_sum(y) * (1.0 / RW_HEAD)
        var = head_sum(dev * dev) * (1.0 / RW_HEAD)
        y = dev * lax.rsqrt(var + RW_GN_EPS) * gng_ref[...] + gnb_ref[...]
        bonus = head_sum(r[g] * k2[g] * rk_ref[...])
        o_ref[g] = (y + bonus * v[g]) * gate[g]


def _rwkv(p, mu, w0, w_up, a0, a_up, g_up, k_k, k_a, r_k, gn_gain, gn_bias):
    b, t, c = p.shape
    group = math.gcd(b, RW_SEQ_GROUP)
    head_of = jnp.arange(BRANCH_W) // RW_HEAD
    same_head = (head_of[:, None] == head_of[None, :]).astype(BF16)
    row = lambda x: x.reshape(1, -1)
    vec = lambda n: pl.BlockSpec((1, n), lambda i, j: (0, 0))
    mat = lambda shp: pl.BlockSpec(shp, lambda i, j: (0, 0))
    return pl.pallas_call(
        _rwkv_kernel,
        out_shape=jax.ShapeDtypeStruct((b, t, BRANCH_W), F32),
        grid=(b // group, t // RW_CHUNK),
        in_specs=[
            pl.BlockSpec((group, RW_CHUNK, c), lambda i, j: (i, j, 0)),
            vec(c), vec(BRANCH_W), mat(w_up.shape), vec(BRANCH_W), mat(a_up.shape), mat(g_up.shape),
            vec(BRANCH_W), vec(BRANCH_W), vec(BRANCH_W), vec(BRANCH_W), vec(BRANCH_W),
            mat((BRANCH_W, BRANCH_W)),
        ],
        out_specs=pl.BlockSpec((group, RW_CHUNK, BRANCH_W), lambda i, j: (i, j, 0)),
        scratch_shapes=[pltpu.VMEM((group, 1, c), F32),
                        pltpu.VMEM((group * RW_HEADS, RW_HEAD, RW_HEAD), F32)],
        compiler_params=_params("parallel", "arbitrary"),
        name="rwkv7",
    )(p, row(mu), row(w0), w_up, row(a0), a_up, g_up, row(k_k), row(k_a), row(r_k), row(gn_gain),
      row(gn_bias), same_head)


def _log_sigmoid(x):
    return jnp.minimum(x, 0.0) - jnp.log1p(jnp.exp(-jnp.abs(x)))


def _mlstm_kernel(p_ref, cw_ref, cb_ref, gb_ref, ng_ref, o_ref, tail_ref, c_ref, n_ref, m_ref):
    @pl.when(pl.program_id(1) == 0)
    def _():
        tail_ref[...] = jnp.zeros_like(tail_ref)
        c_ref[...] = jnp.zeros_like(c_ref)
        n_ref[...] = jnp.zeros_like(n_ref)
        m_ref[...] = jnp.zeros_like(m_ref)

    L = ML_CHUNK
    qk_w = 2 * BRANCH_W
    lower = _tril(L)
    tril_f = lower.astype(F32)
    seqs = range(p_ref.shape[0])

    x = [p_ref[g] for g in seqs]
    q, k, gates, b_all, b_t, g_t = [], [], [], [], [], []
    for g in seqs:
        qk_in = x[g][:, :qk_w]
        ext = jnp.concatenate([tail_ref[g], qk_in], axis=0)
        tail_ref[g] = qk_in[L - 8:L, :]
        conv = cb_ref[...] + cw_ref[ML_CONV - 1:ML_CONV, :] * qk_in
        for j in range(ML_CONV - 1):
            conv = conv + cw_ref[j:j + 1, :] * pltpu.roll(ext, ML_CONV - 1 - j, 0)[8:8 + L]
        qk = conv * _sigmoid(conv)
        q.append(qk[:, :BRANCH_W])
        k.append(qk[:, BRANCH_W:] * ML_HEAD ** -0.5)
        gates.append(x[g][:, qk_w + 2 * BRANCH_W:] + gb_ref[...])
        b_all.append(_dot_hi(tril_f, _log_sigmoid(gates[g])))
        b_t.append(b_all[g].T)
        g_t.append(gates[g].T)

    units = [(g, h) for g in seqs for h in range(ML_HEADS)]
    sl = [slice(h * ML_HEAD, (h + 1) * ML_HEAD) for _, h in units]
    q_u = [q[g][:, sl[i]] for i, (g, h) in enumerate(units)]
    k_u = [k[g][:, sl[i]] for i, (g, h) in enumerate(units)]
    v_u = [x[g][:, qk_w + h * ML_HEAD:qk_w + (h + 1) * ML_HEAD] for g, h in units]
    b_col = [b_all[g][:, ML_HEADS + h:ML_HEADS + h + 1] for g, h in units]
    b_row = [b_t[g][ML_HEADS + h:ML_HEADS + h + 1, :] for g, h in units]
    i_col = [gates[g][:, h:h + 1] for g, h in units]
    i_row = [g_t[g][h:h + 1, :] for g, h in units]
    m_st = [m_ref[g, h:h + 1, 0:1] for g, h in units]
    n_st = [n_ref[g, h:h + 1, :] for g, h in units]
    c_st = [c_ref[g * ML_HEADS + h] for g, h in units]
    idx = range(len(units))

    log_d = [jnp.where(lower, b_col[i] - b_row[i] + i_row[i], -jnp.inf) for i in idx]
    m_j = [jnp.maximum(b_col[i] + m_st[i], jnp.max(log_d[i], axis=-1, keepdims=True)) for i in idx]
    s = [_dot_nt(q_u[i], k_u[i]) * jnp.exp(log_d[i] - m_j[i]) for i in idx]
    inter = [jnp.exp(b_col[i] + m_st[i] - m_j[i]) for i in idx]
    num = [_dot(s[i], v_u[i]) + inter[i] * _dot_nt(q_u[i], c_st[i]) for i in idx]
    den = [jnp.sum(s[i], axis=-1, keepdims=True)
           + inter[i] * jnp.sum(q_u[i] * n_st[i], axis=-1, keepdims=True) for i in idx]
    hh = [num[i] / jnp.maximum(jnp.abs(den[i]), jnp.exp(-m_j[i])) for i in idx]

    g_tot = [b_col[i][L - 1:L, :] for i in idx]
    w_log = [g_tot[i] - b_col[i] + i_col[i] for i in idx]
    m_new = [jnp.maximum(g_tot[i] + m_st[i], jnp.max(w_log[i], axis=0, keepdims=True)) for i in idx]
    wgt = [jnp.exp(w_log[i] - m_new[i]) for i in idx]
    dec = [jnp.exp(g_tot[i] + m_st[i] - m_new[i]) for i in idx]
    c_new = [dec[i] * c_st[i] + _dot_tn(wgt[i] * v_u[i], k_u[i]) for i in idx]
    n_new = [dec[i] * n_st[i] + jnp.sum(wgt[i] * k_u[i], axis=0, keepdims=True) for i in idx]
    hn = [hh[i] * lax.rsqrt(jnp.mean(hh[i] * hh[i], axis=-1, keepdims=True) + NORM_EPS) * ng_ref[:, sl[i]]
          for i in idx]

    for i, (g, h) in enumerate(units):
        c_ref[g * ML_HEADS + h] = c_new[i]
        n_ref[g, h:h + 1, :] = n_new[i]
        m_ref[g, h:h + 1, :] = jnp.broadcast_to(m_new[i], (1, ML_HEAD))
        o_gate = x[g][:, qk_w + BRANCH_W + h * ML_HEAD:qk_w + BRANCH_W + (h + 1) * ML_HEAD]
        o_ref[g, :, sl[i]] = _sigmoid(o_gate) * hn[i]


def _mlstm(p, conv_w, conv_b, gate_bias, norm_gain):
    b, t, c = p.shape
    group = math.gcd(b, ML_SEQ_GROUP)
    return pl.pallas_call(
        _mlstm_kernel,
        out_shape=jax.ShapeDtypeStruct((b, t, BRANCH_W), F32),
        grid=(b // group, t // ML_CHUNK),
        in_specs=[
            pl.BlockSpec((group, ML_CHUNK, c), lambda i, j: (i, j, 0)),
            pl.BlockSpec(conv_w.shape, lambda i, j: (0, 0)),
            pl.BlockSpec((1, 2 * BRANCH_W), lambda i, j: (0, 0)),
            pl.BlockSpec((1, 128), lambda i, j: (0, 0)),
            pl.BlockSpec((1, BRANCH_W), lambda i, j: (0, 0)),
        ],
        out_specs=pl.BlockSpec((group, ML_CHUNK, BRANCH_W), lambda i, j: (i, j, 0)),
        scratch_shapes=[
            pltpu.VMEM((group, 8, 2 * BRANCH_W), F32),
            pltpu.VMEM((group * ML_HEADS, ML_HEAD, ML_HEAD), F32),
            pltpu.VMEM((group, 8, ML_HEAD), F32),
            pltpu.VMEM((group, 8, ML_HEAD), F32),
        ],
        compiler_params=_params("parallel", "arbitrary"),
        name="mlstm",
    )(p, conv_w, conv_b.reshape(1, -1), gate_bias, norm_gain.reshape(1, -1))


def _sum_sublane_groups(x):
    n = x.shape[0] // 32
    g = x.reshape(4 * n, 8, x.shape[1])
    parts = [g[i * n:(i + 1) * n] for i in range(4)]
    return (jnp.sum(parts[0], axis=0) + jnp.sum(parts[1], axis=0)) + (jnp.sum(parts[2], axis=0) + jnp.sum(parts[3], axis=0))


def _dsa_kernel(pq_ref, pk_ref, kvn_ref, wukt_ref, wuv_ref, o_ref, ckv_ref, ckvt_ref, kidx_ref, sc_ref,
                *, top_k):
    qb = pl.program_id(1)
    tq = DSA_QBLOCK
    kc = DSA_KEY_CHUNK
    tk = pk_ref.shape[1]
    n_chunks_max = ckvt_ref.shape[0]
    hq = DSA_HEADS * tq

    @pl.when(qb == 0)
    def _():
        keys = pk_ref[0]
        ckv = _rms(keys[:, :DSA_LATENT], kvn_ref[...])
        pad_rows = n_chunks_max * kc - tk
        ckv_ref[0:tk, :] = ckv.astype(BF16)
        ckv_ref[tk:, :] = jnp.zeros((pad_rows, DSA_LATENT), BF16)
        kidx_ref[0:tk, :] = keys[:, DSA_LATENT:DSA_LATENT + IDX_HEAD].astype(BF16)
        kidx_ref[tk:, :] = jnp.zeros((pad_rows, IDX_HEAD), BF16)
        ckv_t = jnp.concatenate([ckv.T, jnp.zeros((DSA_LATENT, pad_rows), F32)], axis=1)
        extra = jnp.where(lax.broadcasted_iota(jnp.int32, (DSA_SUM_ROWS, kc), 0) == 0, 1.0, 0.0)
        for c in range(n_chunks_max):
            ckvt_ref[c] = jnp.concatenate([ckv_t[:, c * kc:(c + 1) * kc], extra], axis=0).astype(BF16)

    n_chunks = jnp.minimum(lax.shift_right_logical(qb * tq + tq + N_META + kc - 1, int(math.log2(kc))),
                           n_chunks_max)
    q_pos = qb * tq + lax.broadcasted_iota(jnp.int32, (1, tq), 1)
    q_chunk = jnp.where(q_pos < N_META, 0,
                        1 + lax.shift_right_arithmetic(q_pos - N_META, int(math.log2(STREAM_CHUNK))))
    n_allowed = jnp.minimum(N_META + STREAM_CHUNK * q_chunk, tk)

    def rows_of(c):
        return pl.ds(pl.multiple_of(c * kc, kc), kc)

    pq = pq_ref[0]
    q = pq[:, :BRANCH_W]
    qi_t = jnp.concatenate([pq[:, BRANCH_W + j * 128:BRANCH_W + (j + 1) * 128].T
                            for j in range(IDX_HEADS * IDX_HEAD // 128)], axis=0)
    qi_t = jnp.concatenate([qi_t[h * IDX_HEAD:(h + 1) * IDX_HEAD] for h in range(IDX_HEADS)], axis=1)
    tail_t = pq[:, DSA_COLS_PAD - 128:].T
    w_t = tail_t[IDX_HEAD:IDX_HEAD + IDX_HEADS] * (IDX_HEADS * IDX_HEAD) ** -0.5
    w_row = jnp.concatenate([w_t[h:h + 1] for h in range(IDX_HEADS)], axis=1)
    qi_t = qi_t.astype(BF16)

    def score_chunk(c, bounds):
        lo, hi = bounds
        s = jnp.maximum(jnp.dot(kidx_ref[rows_of(c), :], qi_t, preferred_element_type=F32), 0.0) * w_row
        score = s[:, 0:tq]
        for h in range(1, IDX_HEADS):
            score = score + s[:, h * tq:(h + 1) * tq]
        k_pos = c * kc + lax.broadcasted_iota(jnp.int32, (kc, 1), 0)
        allowed = k_pos < n_allowed
        sc_ref[rows_of(c), :] = jnp.where(allowed, score, -jnp.inf)
        lo = jnp.minimum(lo, jnp.min(jnp.where(allowed, score, jnp.inf), axis=0, keepdims=True))
        hi = jnp.maximum(hi, jnp.max(jnp.where(allowed, score, -jnp.inf), axis=0, keepdims=True))
        return lo, hi

    lo, row_max = lax.fori_loop(0, n_chunks, score_chunk,
                                (jnp.full((1, tq), jnp.inf, F32), jnp.full((1, tq), -jnp.inf, F32)))
    hi = row_max + jnp.maximum(jnp.abs(row_max) * 2.0 ** -20, 1e-30)

    def count(pred, level):
        def body(c, acc):
            return acc + _sum_sublane_groups(jnp.where(pred(sc_ref[rows_of(c), :], level), 1.0, 0.0))
        return jnp.sum(lax.fori_loop(0, n_chunks, body, jnp.zeros((8, tq), F32)), axis=0, keepdims=True)

    def bisect(_, bracket):
        lo, hi, n_lo = bracket
        mid = 0.5 * lo + 0.5 * hi
        n_mid = count(jnp.greater_equal, mid)
        ge = n_mid >= top_k
        return jnp.where(ge, mid, lo), jnp.where(ge, hi, mid), jnp.where(ge, n_mid, n_lo)

    def snap(lo):
        def body(c, val):
            s = sc_ref[rows_of(c), :]
            return jnp.minimum(val, jnp.min(jnp.where(s >= lo, s, jnp.inf), axis=0, keepdims=True))
        val = lax.fori_loop(0, n_chunks, body, jnp.full((1, tq), jnp.inf, F32))
        return val, count(jnp.greater, val)

    bracket = (lo, hi, n_allowed.astype(F32))
    bracket = lax.fori_loop(0, DSA_BISECT_STEPS, bisect, bracket)

    def keeps_too_many(state):
        return jnp.logical_and(jnp.max(state[0][2]) > top_k, state[1] < DSA_BISECT_EXTRA_ROUNDS)

    def halve_more(state):
        return lax.fori_loop(0, DSA_BISECT_REFINE, bisect, state[0]), state[1] + 1

    bracket, _ = lax.while_loop(keeps_too_many, halve_more, (bracket, jnp.int32(0)))

    some_row_keeps_too_many = (jnp.max(bracket[2]) > top_k).astype(jnp.int32)
    thr, n_above = lax.fori_loop(0, some_row_keeps_too_many, lambda _, carry: snap(bracket[0]),
                                 (bracket[0], jnp.zeros((1, tq), F32)))

    def unsettled(state):
        return jnp.logical_and(jnp.max(state[2]) >= top_k, state[3] < DSA_BISECT_MAX_ROUNDS)

    def refine(state):
        bracket = lax.fori_loop(0, DSA_BISECT_REFINE, bisect, state[0])
        thr, n_above = snap(bracket[0])
        return bracket, thr, n_above, state[3] + 1

    bracket, thr, n_above, _ = lax.while_loop(unsettled, refine, (bracket, thr, n_above, jnp.int32(0)))

    @pl.when(jnp.max(bracket[2]) > top_k)
    def _():
        need = top_k - n_above
        earlier = (lax.broadcasted_iota(jnp.int32, (kc, kc), 1)
                   < lax.broadcasted_iota(jnp.int32, (kc, kc), 0)).astype(BF16)

        def body(c, seen):
            s = sc_ref[rows_of(c), :]
            tie = jnp.where(s == thr, 1.0, 0.0)
            rank = seen + jnp.dot(earlier, tie.astype(BF16), preferred_element_type=F32)
            sc_ref[rows_of(c), :] = jnp.where((tie > 0.0) & (rank >= need), -jnp.inf, s)
            return seen + jnp.sum(tie, axis=0, keepdims=True)

        lax.fori_loop(0, n_chunks, body, jnp.zeros((1, tq), F32))

    q_t = jnp.concatenate([_dot_nt(wukt_ref[h], q[:, h * DSA_HEAD:(h + 1) * DSA_HEAD])
                           for h in range(DSA_HEADS)], axis=1)
    q_t = (q_t * (DSA_HEAD ** -0.5 * math.log2(math.e))).astype(BF16)

    def finite_or_zero(m):
        return jnp.where(m == -jnp.inf, 0.0, m)

    def attend(chunks, m, acc):
        n = range(len(chunks))
        logits = [jnp.dot(ckv_ref[rows_of(c), :], q_t, preferred_element_type=F32)
                  + jnp.concatenate([jnp.where(sc_ref[rows_of(c), :] >= thr, 0.0, -jnp.inf)] * DSA_HEADS, axis=1)
                  for c in chunks]
        m_new = [jnp.maximum(m[s], jnp.max(logits[s], axis=0, keepdims=True)) for s in n]
        m_safe = [finite_or_zero(m_new[s]) for s in n]
        e = [jnp.exp2(logits[s] - m_safe[s]).astype(BF16) for s in n]
        acc = [acc[s] * jnp.exp2(m[s] - m_safe[s])
               + jnp.dot(ckvt_ref[chunks[s]], e[s], preferred_element_type=F32) for s in n]
        return m_new, acc

    def attend_pair(i, carry):
        m, acc = attend([2 * i, 2 * i + 1], carry[:2], carry[2:])
        return m[0], m[1], acc[0], acc[1]

    def attend_last(_, carry):
        m, acc = attend([n_chunks - 1], carry[:1], carry[2:3])
        return m[0], carry[1], acc[0], carry[3]

    m_init = jnp.full((1, hq), -jnp.inf, F32)
    acc_init = jnp.zeros((DSA_LATENT + DSA_SUM_ROWS, hq), F32)
    streams = lax.fori_loop(0, lax.shift_right_logical(n_chunks, 1), attend_pair,
                            (m_init, m_init, acc_init, acc_init))
    m0, m1, acc0, acc1 = lax.fori_loop(0, n_chunks & 1, attend_last, streams)
    m_safe = finite_or_zero(jnp.maximum(m0, m1))
    acc = acc0 * jnp.exp2(m0 - m_safe) + acc1 * jnp.exp2(m1 - m_safe)
    out_t = acc[:DSA_LATENT] / acc[DSA_LATENT:DSA_LATENT + 1]
    for h in range(DSA_HEADS):
        o_ref[0, :, h * DSA_HEAD:(h + 1) * DSA_HEAD] = _dot_tn(out_t[:, h * tq:(h + 1) * tq], wuv_ref[h])


def _dsa(p, kv_norm, w_uk_t, w_uv, top_k):
    b, t, c = p.shape
    n_chunks = -(-t // DSA_KEY_CHUNK)
    return pl.pallas_call(
        functools.partial(_dsa_kernel, top_k=top_k),
        out_shape=jax.ShapeDtypeStruct((b, t, BRANCH_W), F32),
        grid=(b, t // DSA_QBLOCK),
        in_specs=[
            pl.BlockSpec((1, DSA_QBLOCK, c), lambda i, j: (i, j, 0)),
            pl.BlockSpec((1, t, DSA_KEY_COL_BLOCK), lambda i, j: (i, 0, c // DSA_KEY_COL_BLOCK - 1)),
            pl.BlockSpec((1, DSA_LATENT), lambda i, j: (0, 0)),
            pl.BlockSpec(w_uk_t.shape, lambda i, j: (0, 0, 0)),
            pl.BlockSpec(w_uv.shape, lambda i, j: (0, 0, 0)),
        ],
        out_specs=pl.BlockSpec((1, DSA_QBLOCK, BRANCH_W), lambda i, j: (i, j, 0)),
        scratch_shapes=[
            pltpu.VMEM((n_chunks * DSA_KEY_CHUNK, DSA_LATENT), BF16),
            pltpu.VMEM((n_chunks, DSA_LATENT + DSA_SUM_ROWS, DSA_KEY_CHUNK), BF16),
            pltpu.VMEM((n_chunks * DSA_KEY_CHUNK, IDX_HEAD), BF16),
            pltpu.VMEM((n_chunks * DSA_KEY_CHUNK, DSA_QBLOCK), F32),
        ],
        compiler_params=_params("parallel", "arbitrary"),
        name="dsa",
    )(p, p, kv_norm.reshape(1, -1), w_uk_t, w_uv)


def _merge_kernel(h_ref, yr_ref, ym_ref, yd_ref, wg_ref, wb_ref, wo_ref, gpre_ref, gpost_ref, o_ref):
    h = h_ref[...]
    xn = _rms(h, gpre_ref[...]).astype(BF16)
    mixed = None
    for i, y_ref in enumerate((yr_ref, ym_ref, yd_ref)):
        gate = jnp.dot(xn, wg_ref[:, i * D_MODEL:(i + 1) * D_MODEL], preferred_element_type=F32)
        proj = jnp.dot(y_ref[...].astype(BF16), wb_ref[i], preferred_element_type=F32)
        term = _sigmoid(gate) * proj
        mixed = term if mixed is None else mixed + term
    out = jnp.dot(mixed.astype(BF16), wo_ref[...], preferred_element_type=F32)
    o_ref[...] = h + _rms(out, gpost_ref[...])


def _merge(h, y_rw, y_ml, y_dsa, w_gate, w_branch, w_out, gain_pre, gain_post):
    m, d = h.shape
    rows = lambda n: pl.BlockSpec((ROW_TILE, n), lambda i: (i, 0))
    return pl.pallas_call(
        _merge_kernel,
        out_shape=jax.ShapeDtypeStruct((m, d), F32),
        grid=(m // ROW_TILE,),
        in_specs=[
            rows(d), rows(BRANCH_W), rows(BRANCH_W), rows(BRANCH_W),
            pl.BlockSpec(w_gate.shape, lambda i: (0, 0)),
            pl.BlockSpec(w_branch.shape, lambda i: (0, 0, 0)),
            pl.BlockSpec(w_out.shape, lambda i: (0, 0)),
            pl.BlockSpec((1, d), lambda i: (0, 0)),
            pl.BlockSpec((1, d), lambda i: (0, 0)),
        ],
        out_specs=rows(d),
        compiler_params=_params("parallel"),
        name="merge",
    )(h, y_rw, y_ml, y_dsa, w_gate, w_branch, w_out, gain_pre.reshape(1, d), gain_post.reshape(1, d))


def _split_w_in(w):
    rw_end = RW_COLS
    ml_end = rw_end + 4 * BRANCH_W + 2 * ML_HEADS
    dsa_end = ml_end + BRANCH_W + DSA_LATENT + IDX_HEADS * IDX_HEAD + IDX_HEAD + IDX_HEADS
    w_rw, w_ml, w_dsa, w_gate = w[:, :rw_end], w[:, rw_end:ml_end], w[:, ml_end:dsa_end], w[:, dsa_end:]
    w_ml = jnp.pad(w_ml, ((0, 0), (0, ML_COLS_PAD - w_ml.shape[1])))
    q, c_kv, rest = w_dsa[:, :BRANCH_W], w_dsa[:, BRANCH_W:BRANCH_W + DSA_LATENT], w_dsa[:, BRANCH_W + DSA_LATENT:]
    q_idx, tail = rest[:, :IDX_HEADS * IDX_HEAD], rest[:, IDX_HEADS * IDX_HEAD:]
    w_dsa = jnp.concatenate([q, q_idx, c_kv, tail], axis=1)
    w_dsa = jnp.pad(w_dsa, ((0, 0), (0, DSA_COLS_PAD - w_dsa.shape[1])))
    return tuple(x.astype(BF16) for x in (w_rw, w_ml, w_dsa, w_gate))


def kernel(x, meta_tokens, norm_gain, ffn_w_in, ffn_w_out, w_in, rw_mu, rw_w0, rw_w_up, rw_a0, rw_a_up, rw_g_up, rw_k_k, rw_k_a, rw_r_k, rw_gn_gain, rw_gn_bias, ml_conv_w, ml_conv_b, ml_i_bias, ml_f_bias, ml_norm_gain, dsa_kv_norm, dsa_w_uk, dsa_w_uv, w_branch, w_out):
    bsz, seq, d = x.shape
    depth = norm_gain.shape[0]
    top_k = min(TOPK_MAX, seq // 4)
    t_len = seq + N_META
    t_pad = -(-t_len // SEQ_PAD_MULTIPLE) * SEQ_PAD_MULTIPLE
    h = jnp.concatenate([
        jnp.broadcast_to(meta_tokens.astype(x.dtype)[None], (bsz, N_META, d)), x,
        jnp.zeros((bsz, t_pad - t_len, d), x.dtype)], axis=1).reshape(bsz * t_pad, d)

    for l in range(depth):
        g = norm_gain[l]
        h = _ffn(h, g[0], g[1], ffn_w_in[l, 0].astype(BF16), ffn_w_out[l, 0].astype(BF16))

        w_rw, w_ml, w_dsa, w_gate = _split_w_in(w_in[l])
        p_rw, p_ml, p_dsa = (p.reshape(bsz, t_pad, -1) for p in _proj(h, g[2], (w_rw, w_ml, w_dsa)))

        y_rw = _rwkv(p_rw, rw_mu[l], rw_w0[l], rw_w_up[l], rw_a0[l], rw_a_up[l], rw_g_up[l], rw_k_k[l],
                     rw_k_a[l], rw_r_k[l], rw_gn_gain[l], rw_gn_bias[l])
        gate_bias = jnp.pad(jnp.concatenate([ml_i_bias[l], ml_f_bias[l]]), (0, 128 - 2 * ML_HEADS))
        y_ml = _mlstm(p_ml, ml_conv_w[l], ml_conv_b[l], gate_bias.reshape(1, 128), ml_norm_gain[l])
        y_dsa = _dsa(p_dsa, dsa_kv_norm[l], jnp.swapaxes(dsa_w_uk[l], 1, 2).astype(BF16), dsa_w_uv[l].astype(BF16), top_k)

        flat = lambda y: y.reshape(bsz * t_pad, BRANCH_W)
        h = _merge(h, flat(y_rw), flat(y_ml), flat(y_dsa), w_gate, w_branch[l].astype(BF16),
                   w_out[l].astype(BF16), g[2], g[3])
        h = _ffn(h, g[4], g[5], ffn_w_in[l, 1].astype(BF16), ffn_w_out[l, 1].astype(BF16))

    return h.reshape(bsz, t_pad, d)[:, N_META:t_len]
```

```python
import functools
import math

import jax
import jax.numpy as jnp
from jax import lax
from jax.experimental import pallas as pl
from jax.experimental.pallas import tpu as pltpu

F32 = jnp.float32
BF16 = jnp.bfloat16
HI = lax.Precision.HIGHEST

D_MODEL = 1024
D_FF = 2816
N_META = 16
STREAM_CHUNK = 64
BRANCH_W = 512
NORM_EPS = 1e-6

RW_HEADS, RW_HEAD = 8, 64
RW_COLS = 1792
RW_GN_EPS = 64e-5
RW_CHUNK = 64

ML_HEADS, ML_HEAD = 4, 128
ML_CHUNK = 64
ML_CONV = 4
ML_COLS_PAD = 2176

DSA_HEADS, DSA_HEAD, DSA_LATENT = 8, 64, 128
IDX_HEADS, IDX_HEAD = 8, 64
TOPK_MAX = 256
DSA_QBLOCK = 128
DSA_COLS_PAD = 1280
DSA_CKV_OFF = BRANCH_W
DSA_QIDX_OFF = DSA_CKV_OFF + DSA_LATENT
DSA_TAIL_OFF = DSA_QIDX_OFF + IDX_HEADS * IDX_HEAD

ROW_TILE = 512
FFN_ROW_TILE = 1024
MXU_WIDTH = 256
FF_SLICES = ((0, 6 * MXU_WIDTH), (6 * MXU_WIDTH, D_FF))
SEQ_PAD_MULTIPLE = 128
RW_SEQ_GROUP = 2
ML_SEQ_GROUP = 4
VMEM_LIMIT = 56 * 1024 * 1024
DSA_KEY_CHUNK = 512
DSA_SUM_ROWS = 16
DSA_BISECT_STEPS = 16
DSA_BISECT_REFINE = 4
DSA_BISECT_EXTRA_ROUNDS = 2
DSA_BISECT_MAX_ROUNDS = 80


def _dot(a, b):
    return jnp.dot(a.astype(BF16), b.astype(BF16), preferred_element_type=F32)


def _dot_nt(a, b):
    return lax.dot_general(a.astype(BF16), b.astype(BF16), (((1,), (1,)), ((), ())),
                           preferred_element_type=F32)


def _dot_tn(a, b):
    return lax.dot_general(a.astype(BF16), b.astype(BF16), (((0,), (0,)), ((), ())),
                           preferred_element_type=F32)


def _dot_hi(a, b):
    return jnp.dot(a, b, preferred_element_type=F32, precision=HI)


def _rms(x, gain):
    return x * lax.rsqrt(jnp.mean(x * x, axis=-1, keepdims=True) + NORM_EPS) * gain


def _sigmoid(x):
    return jax.nn.sigmoid(x)


def _tril(n, strict=False):
    r = lax.broadcasted_iota(jnp.int32, (n, n), 0)
    c = lax.broadcasted_iota(jnp.int32, (n, n), 1)
    return (c < r) if strict else (c <= r)


def _params(*sem):
    return pltpu.CompilerParams(dimension_semantics=sem, vmem_limit_bytes=VMEM_LIMIT)


def _ffn_rows(h, gpre_ref, gpost_ref, wi_ref, wo_ref):
    xn = _rms(h, gpre_ref[...]).astype(BF16)
    out = None
    for lo, hi in FF_SLICES:
        gate = jnp.dot(xn, wi_ref[:, lo:hi], preferred_element_type=F32)
        up = jnp.dot(xn, wi_ref[:, D_FF + lo:D_FF + hi], preferred_element_type=F32)
        act = (gate * _sigmoid(gate) * up).astype(BF16)
        part = jnp.dot(act, wo_ref[lo:hi, :], preferred_element_type=F32)
        out = part if out is None else out + part
    return h + 0.5 * _rms(out, gpost_ref[...])


def _ffn_kernel(h_ref, gpre_ref, gpost_ref, wi_ref, wo_ref, o_ref):
    o_ref[...] = _ffn_rows(h_ref[...], gpre_ref, gpost_ref, wi_ref, wo_ref)


def _ffn_tail_kernel(h_ref, next_ref, gpre_ref, gpost_ref, wi_ref, wo_ref, o_ref):
    h = jnp.concatenate([h_ref[0, N_META:, :], next_ref[0]], axis=0)
    o_ref[0] = _ffn_rows(h, gpre_ref, gpost_ref, wi_ref, wo_ref)


def _ffn_specs(d, w_in, w_out, index):
    resident = lambda shape: pl.BlockSpec(shape, lambda *_: (0, 0), pipeline_mode=pl.Buffered(1))
    return [pl.BlockSpec((1, d), index), pl.BlockSpec((1, d), index), resident(w_in.shape), resident(w_out.shape)]


def _ffn(h, g_pre, g_post, w_in, w_out):
    m, d = h.shape
    return pl.pallas_call(
        _ffn_kernel,
        out_shape=jax.ShapeDtypeStruct((m, d), F32),
        grid=(m // FFN_ROW_TILE,),
        in_specs=[pl.BlockSpec((FFN_ROW_TILE, d), lambda i: (i, 0))] + _ffn_specs(d, w_in, w_out, lambda i: (0, 0)),
        out_specs=pl.BlockSpec((FFN_ROW_TILE, d), lambda i: (i, 0)),
        compiler_params=_params("parallel"),
        name="ffn",
    )(h, g_pre.reshape(1, d), g_post.reshape(1, d), w_in, w_out)


def _ffn_tail(h, seq, g_pre, g_post, w_in, w_out):
    bsz, t_pad, d = h.shape
    tile = FFN_ROW_TILE
    assert seq % tile == 0 and tile % N_META == 0 and seq + N_META <= t_pad
    return pl.pallas_call(
        _ffn_tail_kernel,
        out_shape=jax.ShapeDtypeStruct((bsz, seq, d), F32),
        grid=(bsz, seq // tile),
        in_specs=[
            pl.BlockSpec((1, tile, d), lambda b, i: (b, i, 0)),
            pl.BlockSpec((1, N_META, d), lambda b, i: (b, (i + 1) * (tile // N_META), 0)),
        ] + _ffn_specs(d, w_in, w_out, lambda b, i: (0, 0)),
        out_specs=pl.BlockSpec((1, tile, d), lambda b, i: (b, i, 0)),
        compiler_params=_params("parallel", "parallel"),
        name="ffn_tail",
    )(h, h, g_pre.reshape(1, d), g_post.reshape(1, d), w_in, w_out)


def _proj_kernel(h_ref, g_ref, *refs):
    w_refs, o_refs = refs[:len(refs) // 2], refs[len(refs) // 2:]
    xn = _rms(h_ref[...], g_ref[...]).astype(BF16)
    for w_ref, o_ref in zip(w_refs, o_refs):
        o_ref[...] = jnp.dot(xn, w_ref[...], preferred_element_type=F32)


def _proj(h, gain, weights):
    m, d = h.shape
    return pl.pallas_call(
        _proj_kernel,
        out_shape=[jax.ShapeDtypeStruct((m, w.shape[1]), F32) for w in weights],
        grid=(m // ROW_TILE,),
        in_specs=[
            pl.BlockSpec((ROW_TILE, d), lambda i: (i, 0)),
            pl.BlockSpec((1, d), lambda i: (0, 0)),
        ] + [pl.BlockSpec(w.shape, lambda i: (0, 0)) for w in weights],
        out_specs=[pl.BlockSpec((ROW_TILE, w.shape[1]), lambda i: (i, 0)) for w in weights],
        compiler_params=_params("parallel"),
        name="in_proj",
    )(h, gain.reshape(1, d), *weights)


def _rwkv_kernel(p_ref, mu_ref, w0_ref, wup_ref, a0_ref, aup_ref, gup_ref, kk_ref, ka_ref, rk_ref,
                 gng_ref, gnb_ref, hs_ref, o_ref, prev_ref, s_ref):
    @pl.when(pl.program_id(1) == 0)
    def _():
        prev_ref[...] = jnp.zeros_like(prev_ref)
        s_ref[...] = jnp.zeros_like(s_ref)

    L = RW_CHUNK
    lower = _tril(L)
    strict = _tril(L, strict=True)
    tril_f = lower.astype(F32)
    row = lax.broadcasted_iota(jnp.int32, (L, 1), 0)
    steps = int(math.log2(L))
    seqs = range(p_ref.shape[0])

    def head_sum(x):
        return _dot(x, hs_ref[...])

    r, k2, v, gate, g_last, r_rows, a_rows, b_rows, k_rows, v_b = ([] for _ in range(10))
    for g in seqs:
        p = p_ref[g]
        prev = jnp.where(row == 0, prev_ref[g], pltpu.roll(p, 1, 0))
        prev_ref[g] = p[L - 1:L, :]
        ps = p + (prev - p) * mu_ref[...]
        k = ps[:, BRANCH_W:2 * BRANCH_W]
        d = w0_ref[...] + _dot(jnp.tanh(ps[:, 1536:1600]), wup_ref[...])
        log_w = -math.exp(-0.5) * _sigmoid(d)
        a = _sigmoid(a0_ref[...] + _dot(ps[:, 1600:1664], aup_ref[...]))
        gate.append(_dot(_sigmoid(ps[:, 1664:1792]), gup_ref[...]))
        r.append(ps[:, 0:BRANCH_W])
        v.append(ps[:, 2 * BRANCH_W:3 * BRANCH_W])
        kappa = k * kk_ref[...]
        kh = kappa / jnp.maximum(jnp.sqrt(head_sum(kappa * kappa)), 1e-12)
        k2.append(k * (1.0 + (a - 1.0) * ka_ref[...]))
        cum = _dot_hi(tril_f, log_w)
        g_incl = jnp.exp(cum)
        g_inv = jnp.exp(-cum)
        g_last.append(g_incl[L - 1:L, :])
        r_rows.append((r[g] * g_incl).astype(BF16))
        a_rows.append((-kh * jnp.exp(cum - log_w)).astype(BF16))
        b_rows.append((a * kh * g_inv).astype(BF16))
        k_rows.append((k2[g] * g_inv).astype(BF16))
        v_b.append(v[g].astype(BF16))

    units = [(g, h) for g in seqs for h in range(RW_HEADS)]
    idx = range(len(units))
    sl = [slice(h * RW_HEAD, (h + 1) * RW_HEAD) for _, h in units]
    lhs = [jnp.concatenate([r_rows[g][:, sl[i]], a_rows[g][:, sl[i]]], axis=0) for i, (g, h) in enumerate(units)]
    b_u = [b_rows[g][:, sl[i]] for i, (g, h) in enumerate(units)]
    k_u = [k_rows[g][:, sl[i]] for i, (g, h) in enumerate(units)]
    v_u = [v_b[g][:, sl[i]] for i, (g, h) in enumerate(units)]
    s0 = [s_ref[g * RW_HEADS + h] for g, h in units]
    gram = [_dot_nt(lhs[i], jnp.concatenate([b_u[i], k_u[i]], axis=0)) for i in idx]
    ls0 = [_dot_nt(lhs[i], s0[i]) for i in idx]
    u = [ls0[i][L:] + _dot(jnp.where(strict, gram[i][L:, L:], 0.0), v_u[i]) for i in idx]
    n_pow = [jnp.where(strict, gram[i][L:, :L], 0.0) for i in idx]
    for step in range(steps):
        u = [u[i] + _dot(n_pow[i], u[i]) for i in idx]
        if step + 1 < steps:
            n_pow = [_dot(n_pow[i], n_pow[i]) for i in idx]

    y_u = [ls0[i][:L] + _dot(jnp.where(lower, gram[i][:L, :L], 0.0), u[i])
           + _dot(jnp.where(lower, gram[i][:L, L:], 0.0), v_u[i]) for i in idx]
    for i, (g, h) in enumerate(units):
        s_ref[g * RW_HEADS + h] = (s0[i] + _dot_tn(u[i], b_u[i]) + _dot_tn(v_u[i], k_u[i])) * g_last[g][:, sl[i]]

    for g in seqs:
        y = jnp.concatenate(y_u[g * RW_HEADS:(g + 1) * RW_HEADS], axis=1)
        dev = y - head_sum(y) * (1.0 / RW_HEAD)
        var = head_sum(dev * dev) * (1.0 / RW_HEAD)
        y = dev * lax.rsqrt(var + RW_GN_EPS) * gng_ref[...] + gnb_ref[...]
        bonus = head_sum(r[g] * k2[g] * rk_ref[...])
        o_ref[g] = (y + bonus * v[g]) * gate[g]


def _rwkv(p, mu, w0, w_up, a0, a_up, g_up, k_k, k_a, r_k, gn_gain, gn_bias):
    b, t, c = p.shape
    group = math.gcd(b, RW_SEQ_GROUP)
    head_of = jnp.arange(BRANCH_W) // RW_HEAD
    same_head = (head_of[:, None] == head_of[None, :]).astype(BF16)
    row = lambda x: x.reshape(1, -1)
    vec = lambda n: pl.BlockSpec((1, n), lambda i, j: (0, 0))
    mat = lambda shp: pl.BlockSpec(shp, lambda i, j: (0, 0))
    return pl.pallas_call(
        _rwkv_kernel,
        out_shape=jax.ShapeDtypeStruct((b, t, BRANCH_W), F32),
        grid=(b // group, t // RW_CHUNK),
        in_specs=[
            pl.BlockSpec((group, RW_CHUNK, c), lambda i, j: (i, j, 0)),
            vec(c), vec(BRANCH_W), mat(w_up.shape), vec(BRANCH_W), mat(a_up.shape), mat(g_up.shape),
            vec(BRANCH_W), vec(BRANCH_W), vec(BRANCH_W), vec(BRANCH_W), vec(BRANCH_W),
            mat((BRANCH_W, BRANCH_W)),
        ],
        out_specs=pl.BlockSpec((group, RW_CHUNK, BRANCH_W), lambda i, j: (i, j, 0)),
        scratch_shapes=[pltpu.VMEM((group, 1, c), F32),
                        pltpu.VMEM((group * RW_HEADS, RW_HEAD, RW_HEAD), F32)],
        compiler_params=_params("parallel", "arbitrary"),
        name="rwkv7",
    )(p, row(mu), row(w0), w_up, row(a0), a_up, g_up, row(k_k), row(k_a), row(r_k), row(gn_gain),
      row(gn_bias), same_head)


def _log_sigmoid(x):
    return jnp.minimum(x, 0.0) - jnp.log1p(jnp.exp(-jnp.abs(x)))


def _mlstm_kernel(p_ref, cw_ref, cb_ref, gb_ref, ng_ref, o_ref, tail_ref, c_ref, n_ref, m_ref):
    @pl.when(pl.program_id(1) == 0)
    def _():
        tail_ref[...] = jnp.zeros_like(tail_ref)
        c_ref[...] = jnp.zeros_like(c_ref)
        n_ref[...] = jnp.zeros_like(n_ref)
        m_ref[...] = jnp.zeros_like(m_ref)

    L = ML_CHUNK
    qk_w = 2 * BRANCH_W
    lower = _tril(L)
    tril_f = lower.astype(F32)
    seqs = range(p_ref.shape[0])

    x = [p_ref[g] for g in seqs]
    q, k, gates, b_all, b_t, g_t = [], [], [], [], [], []
    for g in seqs:
        qk_in = x[g][:, :qk_w]
        ext = jnp.concatenate([tail_ref[g], qk_in], axis=0)
        tail_ref[g] = qk_in[L - 8:L, :]
        conv = cb_ref[...] + cw_ref[ML_CONV - 1:ML_CONV, :] * qk_in
        for j in range(ML_CONV - 1):
            conv = conv + cw_ref[j:j + 1, :] * pltpu.roll(ext, ML_CONV - 1 - j, 0)[8:8 + L]
        qk = conv * _sigmoid(conv)
        q.append(qk[:, :BRANCH_W])
        k.append(qk[:, BRANCH_W:] * ML_HEAD ** -0.5)
        gates.append(x[g][:, qk_w + 2 * BRANCH_W:] + gb_ref[...])
        b_all.append(_dot_hi(tril_f, _log_sigmoid(gates[g])))
        b_t.append(b_all[g].T)
        g_t.append(gates[g].T)

    units = [(g, h) for g in seqs for h in range(ML_HEADS)]
    sl = [slice(h * ML_HEAD, (h + 1) * ML_HEAD) for _, h in units]
    q_u = [q[g][:, sl[i]] for i, (g, h) in enumerate(units)]
    k_u = [k[g][:, sl[i]] for i, (g, h) in enumerate(units)]
    v_u = [x[g][:, qk_w + h * ML_HEAD:qk_w + (h + 1) * ML_HEAD] for g, h in units]
    b_col = [b_all[g][:, ML_HEADS + h:ML_HEADS + h + 1] for g, h in units]
    b_row = [b_t[g][ML_HEADS + h:ML_HEADS + h + 1, :] for g, h in units]
    i_col = [gates[g][:, h:h + 1] for g, h in units]
    i_row = [g_t[g][h:h + 1, :] for g, h in units]
    m_st = [m_ref[g, h:h + 1, 0:1] for g, h in units]
    n_st = [n_ref[g, h:h + 1, :] for g, h in units]
    c_st = [c_ref[g * ML_HEADS + h] for g, h in units]
    idx = range(len(units))

    log_d = [jnp.where(lower, b_col[i] - b_row[i] + i_row[i], -jnp.inf) for i in idx]
    m_j = [jnp.maximum(b_col[i] + m_st[i], jnp.max(log_d[i], axis=-1, keepdims=True)) for i in idx]
    s = [_dot_nt(q_u[i], k_u[i]) * jnp.exp(log_d[i] - m_j[i]) for i in idx]
    inter = [jnp.exp(b_col[i] + m_st[i] - m_j[i]) for i in idx]
    num = [_dot(s[i], v_u[i]) + inter[i] * _dot_nt(q_u[i], c_st[i]) for i in idx]
    den = [jnp.sum(s[i], axis=-1, keepdims=True)
           + inter[i] * jnp.sum(q_u[i] * n_st[i], axis=-1, keepdims=True) for i in idx]
    hh = [num[i] / jnp.maximum(jnp.abs(den[i]), jnp.exp(-m_j[i])) for i in idx]

    g_tot = [b_col[i][L - 1:L, :] for i in idx]
    w_log = [g_tot[i] - b_col[i] + i_col[i] for i in idx]
    m_new = [jnp.maximum(g_tot[i] + m_st[i], jnp.max(w_log[i], axis=0, keepdims=True)) for i in idx]
    wgt = [jnp.exp(w_log[i] - m_new[i]) for i in idx]
    dec = [jnp.exp(g_tot[i] + m_st[i] - m_new[i]) for i in idx]
    c_new = [dec[i] * c_st[i] + _dot_tn(wgt[i] * v_u[i], k_u[i]) for i in idx]
    n_new = [dec[i] * n_st[i] + jnp.sum(wgt[i] * k_u[i], axis=0, keepdims=True) for i in idx]
    hn = [hh[i] * lax.rsqrt(jnp.mean(hh[i] * hh[i], axis=-1, keepdims=True) + NORM_EPS) * ng_ref[:, sl[i]]
          for i in idx]

    for i, (g, h) in enumerate(units):
        c_ref[g * ML_HEADS + h] = c_new[i]
        n_ref[g, h:h + 1, :] = n_new[i]
        m_ref[g, h:h + 1, :] = jnp.broadcast_to(m_new[i], (1, ML_HEAD))
        o_gate = x[g][:, qk_w + BRANCH_W + h * ML_HEAD:qk_w + BRANCH_W + (h + 1) * ML_HEAD]
        o_ref[g, :, sl[i]] = _sigmoid(o_gate) * hn[i]


def _mlstm(p, conv_w, conv_b, gate_bias, norm_gain):
    b, t, c = p.shape
    group = math.gcd(b, ML_SEQ_GROUP)
    return pl.pallas_call(
        _mlstm_kernel,
        out_shape=jax.ShapeDtypeStruct((b, t, BRANCH_W), F32),
        grid=(b // group, t // ML_CHUNK),
        in_specs=[
            pl.BlockSpec((group, ML_CHUNK, c), lambda i, j: (i, j, 0)),
            pl.BlockSpec(conv_w.shape, lambda i, j: (0, 0)),
            pl.BlockSpec((1, 2 * BRANCH_W), lambda i, j: (0, 0)),
            pl.BlockSpec((1, 128), lambda i, j: (0, 0)),
            pl.BlockSpec((1, BRANCH_W), lambda i, j: (0, 0)),
        ],
        out_specs=pl.BlockSpec((group, ML_CHUNK, BRANCH_W), lambda i, j: (i, j, 0)),
        scratch_shapes=[
            pltpu.VMEM((group, 8, 2 * BRANCH_W), F32),
            pltpu.VMEM((group * ML_HEADS, ML_HEAD, ML_HEAD), F32),
            pltpu.VMEM((group, 8, ML_HEAD), F32),
            pltpu.VMEM((group, 8, ML_HEAD), F32),
        ],
        compiler_params=_params("parallel", "arbitrary"),
        name="mlstm",
    )(p, conv_w, conv_b.reshape(1, -1), gate_bias, norm_gain.reshape(1, -1))


def _sum_sublane_groups(x):
    n = x.shape[0] // 32
    g = x.reshape(4 * n, 8, x.shape[1])
    parts = [g[i * n:(i + 1) * n] for i in range(4)]
    return (jnp.sum(parts[0], axis=0) + jnp.sum(parts[1], axis=0)) + (jnp.sum(parts[2], axis=0) + jnp.sum(parts[3], axis=0))


def _dsa_kernel(pq_ref, pkv_ref, pki_ref, kvn_ref, wukt_ref, wuv_ref, o_ref, ckv_ref, ckvt_ref, kidx_ref, sc_ref,
                *, top_k):
    qb = pl.program_id(1)
    tq = DSA_QBLOCK
    kc = DSA_KEY_CHUNK
    tk = pkv_ref.shape[1]
    n_chunks_max = ckvt_ref.shape[0]
    hq = DSA_HEADS * tq

    @pl.when(qb == 0)
    def _():
        ckv = _rms(pkv_ref[0], kvn_ref[...])
        pad_rows = n_chunks_max * kc - tk
        ckv_ref[0:tk, :] = ckv.astype(BF16)
        ckv_ref[tk:, :] = jnp.zeros((pad_rows, DSA_LATENT), BF16)
        kidx_ref[0:tk, :] = pki_ref[0][:, :IDX_HEAD].astype(BF16)
        kidx_ref[tk:, :] = jnp.zeros((pad_rows, IDX_HEAD), BF16)
        ckv_t = jnp.concatenate([ckv.T, jnp.zeros((DSA_LATENT, pad_rows), F32)], axis=1)
        extra = jnp.where(lax.broadcasted_iota(jnp.int32, (DSA_SUM_ROWS, kc), 0) == 0, 1.0, 0.0)
        for c in range(n_chunks_max):
            ckvt_ref[c] = jnp.concatenate([ckv_t[:, c * kc:(c + 1) * kc], extra], axis=0).astype(BF16)

    n_chunks = jnp.minimum(lax.shift_right_logical(qb * tq + tq + N_META + kc - 1, int(math.log2(kc))),
                           n_chunks_max)
    q_pos = qb * tq + lax.broadcasted_iota(jnp.int32, (1, tq), 1)
    q_chunk = jnp.where(q_pos < N_META, 0,
                        1 + lax.shift_right_arithmetic(q_pos - N_META, int(math.log2(STREAM_CHUNK))))
    n_allowed = jnp.minimum(N_META + STREAM_CHUNK * q_chunk, tk)

    def rows_of(c):
        return pl.ds(pl.multiple_of(c * kc, kc), kc)

    pq = pq_ref[0]
    q = pq[:, :BRANCH_W]
    qi_t = jnp.concatenate([pq[:, DSA_QIDX_OFF + j * 128:DSA_QIDX_OFF + (j + 1) * 128].T
                            for j in range(IDX_HEADS * IDX_HEAD // 128)], axis=0)
    qi_t = jnp.concatenate([qi_t[h * IDX_HEAD:(h + 1) * IDX_HEAD] for h in range(IDX_HEADS)], axis=1)
    tail_t = pq[:, DSA_TAIL_OFF:].T
    w_t = tail_t[IDX_HEAD:IDX_HEAD + IDX_HEADS] * (IDX_HEADS * IDX_HEAD) ** -0.5
    w_row = jnp.concatenate([w_t[h:h + 1] for h in range(IDX_HEADS)], axis=1)
    qi_t = qi_t.astype(BF16)

    def score_chunk(c, bounds):
        lo, hi = bounds
        s = jnp.maximum(jnp.dot(kidx_ref[rows_of(c), :], qi_t, preferred_element_type=F32), 0.0) * w_row
        score = s[:, 0:tq]
        for h in range(1, IDX_HEADS):
            score = score + s[:, h * tq:(h + 1) * tq]
        k_pos = c * kc + lax.broadcasted_iota(jnp.int32, (kc, 1), 0)
        allowed = k_pos < n_allowed
        sc_ref[rows_of(c), :] = jnp.where(allowed, score, -jnp.inf)
        lo = jnp.minimum(lo, jnp.min(jnp.where(allowed, score, jnp.inf), axis=0, keepdims=True))
        hi = jnp.maximum(hi, jnp.max(jnp.where(allowed, score, -jnp.inf), axis=0, keepdims=True))
        return lo, hi

    lo, row_max = lax.fori_loop(0, n_chunks, score_chunk,
                                (jnp.full((1, tq), jnp.inf, F32), jnp.full((1, tq), -jnp.inf, F32)))
    hi = row_max + jnp.maximum(jnp.abs(row_max) * 2.0 ** -20, 1e-30)

    def count(pred, level):
        def body(c, acc):
            return acc + _sum_sublane_groups(jnp.where(pred(sc_ref[rows_of(c), :], level), 1.0, 0.0))
        return jnp.sum(lax.fori_loop(0, n_chunks, body, jnp.zeros((8, tq), F32)), axis=0, keepdims=True)

    def bisect(_, bracket):
        lo, hi, n_lo = bracket
        mid = 0.5 * lo + 0.5 * hi
        n_mid = count(jnp.greater_equal, mid)
        ge = n_mid >= top_k
        return jnp.where(ge, mid, lo), jnp.where(ge, hi, mid), jnp.where(ge, n_mid, n_lo)

    def snap(lo):
        def body(c, val):
            s = sc_ref[rows_of(c), :]
            return jnp.minimum(val, jnp.min(jnp.where(s >= lo, s, jnp.inf), axis=0, keepdims=True))
        val = lax.fori_loop(0, n_chunks, body, jnp.full((1, tq), jnp.inf, F32))
        return val, count(jnp.greater, val)

    bracket = (lo, hi, n_allowed.astype(F32))
    bracket = lax.fori_loop(0, DSA_BISECT_STEPS, bisect, bracket)

    def keeps_too_many(state):
        return jnp.logical_and(jnp.max(state[0][2]) > top_k, state[1] < DSA_BISECT_EXTRA_ROUNDS)

    def halve_more(state):
        return lax.fori_loop(0, DSA_BISECT_REFINE, bisect, state[0]), state[1] + 1

    bracket, _ = lax.while_loop(keeps_too_many, halve_more, (bracket, jnp.int32(0)))

    some_row_keeps_too_many = (jnp.max(bracket[2]) > top_k).astype(jnp.int32)
    thr, n_above = lax.fori_loop(0, some_row_keeps_too_many, lambda _, carry: snap(bracket[0]),
                                 (bracket[0], jnp.zeros((1, tq), F32)))

    def unsettled(state):
        return jnp.logical_and(jnp.max(state[2]) >= top_k, state[3] < DSA_BISECT_MAX_ROUNDS)

    def refine(state):
        bracket = lax.fori_loop(0, DSA_BISECT_REFINE, bisect, state[0])
        thr, n_above = snap(bracket[0])
        return bracket, thr, n_above, state[3] + 1

    bracket, thr, n_above, _ = lax.while_loop(unsettled, refine, (bracket, thr, n_above, jnp.int32(0)))

    @pl.when(jnp.max(bracket[2]) > top_k)
    def _():
        need = top_k - n_above
        earlier = (lax.broadcasted_iota(jnp.int32, (kc, kc), 1)
                   < lax.broadcasted_iota(jnp.int32, (kc, kc), 0)).astype(BF16)

        def body(c, seen):
            s = sc_ref[rows_of(c), :]
            tie = jnp.where(s == thr, 1.0, 0.0)
            rank = seen + jnp.dot(earlier, tie.astype(BF16), preferred_element_type=F32)
            sc_ref[rows_of(c), :] = jnp.where((tie > 0.0) & (rank >= need), -jnp.inf, s)
            return seen + jnp.sum(tie, axis=0, keepdims=True)

        lax.fori_loop(0, n_chunks, body, jnp.zeros((1, tq), F32))

    q_t = jnp.concatenate([_dot_nt(wukt_ref[h], q[:, h * DSA_HEAD:(h + 1) * DSA_HEAD])
                           for h in range(DSA_HEADS)], axis=1)
    q_t = (q_t * (DSA_HEAD ** -0.5 * math.log2(math.e))).astype(BF16)

    def finite_or_zero(m):
        return jnp.where(m == -jnp.inf, 0.0, m)

    def attend(chunks, m, acc):
        n = range(len(chunks))
        logits = [jnp.dot(ckv_ref[rows_of(c), :], q_t, preferred_element_type=F32)
                  + jnp.concatenate([jnp.where(sc_ref[rows_of(c), :] >= thr, 0.0, -jnp.inf)] * DSA_HEADS, axis=1)
                  for c in chunks]
        m_new = [jnp.maximum(m[s], jnp.max(logits[s], axis=0, keepdims=True)) for s in n]
        m_safe = [finite_or_zero(m_new[s]) for s in n]
        e = [jnp.exp2(logits[s] - m_safe[s]).astype(BF16) for s in n]
        acc = [acc[s] * jnp.exp2(m[s] - m_safe[s])
               + jnp.dot(ckvt_ref[chunks[s]], e[s], preferred_element_type=F32) for s in n]
        return m_new, acc

    def attend_pair(i, carry):
        m, acc = attend([2 * i, 2 * i + 1], carry[:2], carry[2:])
        return m[0], m[1], acc[0], acc[1]

    def attend_last(_, carry):
        m, acc = attend([n_chunks - 1], carry[:1], carry[2:3])
        return m[0], carry[1], acc[0], carry[3]

    m_init = jnp.full((1, hq), -jnp.inf, F32)
    acc_init = jnp.zeros((DSA_LATENT + DSA_SUM_ROWS, hq), F32)
    streams = lax.fori_loop(0, lax.shift_right_logical(n_chunks, 1), attend_pair,
                            (m_init, m_init, acc_init, acc_init))
    m0, m1, acc0, acc1 = lax.fori_loop(0, n_chunks & 1, attend_last, streams)
    m_safe = finite_or_zero(jnp.maximum(m0, m1))
    acc = acc0 * jnp.exp2(m0 - m_safe) + acc1 * jnp.exp2(m1 - m_safe)
    out_t = acc[:DSA_LATENT] / acc[DSA_LATENT:DSA_LATENT + 1]
    for h in range(DSA_HEADS):
        o_ref[0, :, h * DSA_HEAD:(h + 1) * DSA_HEAD] = _dot_tn(out_t[:, h * tq:(h + 1) * tq], wuv_ref[h])


def _dsa(p, kv_norm, w_uk_t, w_uv, top_k):
    b, t, c = p.shape
    n_chunks = -(-t // DSA_KEY_CHUNK)
    return pl.pallas_call(
        functools.partial(_dsa_kernel, top_k=top_k),
        out_shape=jax.ShapeDtypeStruct((b, t, BRANCH_W), F32),
        grid=(b, t // DSA_QBLOCK),
        in_specs=[
            pl.BlockSpec((1, DSA_QBLOCK, c), lambda i, j: (i, j, 0)),
            pl.BlockSpec((1, t, DSA_LATENT), lambda i, j: (i, 0, DSA_CKV_OFF // DSA_LATENT)),
            pl.BlockSpec((1, t, 128), lambda i, j: (i, 0, DSA_TAIL_OFF // 128)),
            pl.BlockSpec((1, DSA_LATENT), lambda i, j: (0, 0)),
            pl.BlockSpec(w_uk_t.shape, lambda i, j: (0, 0, 0)),
            pl.BlockSpec(w_uv.shape, lambda i, j: (0, 0, 0)),
        ],
        out_specs=pl.BlockSpec((1, DSA_QBLOCK, BRANCH_W), lambda i, j: (i, j, 0)),
        scratch_shapes=[
            pltpu.VMEM((n_chunks * DSA_KEY_CHUNK, DSA_LATENT), BF16),
            pltpu.VMEM((n_chunks, DSA_LATENT + DSA_SUM_ROWS, DSA_KEY_CHUNK), BF16),
            pltpu.VMEM((n_chunks * DSA_KEY_CHUNK, IDX_HEAD), BF16),
            pltpu.VMEM((n_chunks * DSA_KEY_CHUNK, DSA_QBLOCK), F32),
        ],
        compiler_params=_params("parallel", "arbitrary"),
        name="dsa",
    )(p, p, p, kv_norm.reshape(1, -1), w_uk_t, w_uv)


def _merge_kernel(h_ref, yr_ref, ym_ref, yd_ref, wg_ref, wb_ref, wo_ref, gpre_ref, gpost_ref, o_ref):
    h = h_ref[...]
    xn = _rms(h, gpre_ref[...]).astype(BF16)
    mixed = None
    for i, y_ref in enumerate((yr_ref, ym_ref, yd_ref)):
        gate = jnp.dot(xn, wg_ref[:, i * D_MODEL:(i + 1) * D_MODEL], preferred_element_type=F32)
        proj = jnp.dot(y_ref[...].astype(BF16), wb_ref[i], preferred_element_type=F32)
        term = _sigmoid(gate) * proj
        mixed = term if mixed is None else mixed + term
    out = jnp.dot(mixed.astype(BF16), wo_ref[...], preferred_element_type=F32)
    o_ref[...] = h + _rms(out, gpost_ref[...])


def _merge(h, y_rw, y_ml, y_dsa, w_gate, w_branch, w_out, gain_pre, gain_post):
    m, d = h.shape
    rows = lambda n: pl.BlockSpec((ROW_TILE, n), lambda i: (i, 0))
    return pl.pallas_call(
        _merge_kernel,
        out_shape=jax.ShapeDtypeStruct((m, d), F32),
        grid=(m // ROW_TILE,),
        in_specs=[
            rows(d), rows(BRANCH_W), rows(BRANCH_W), rows(BRANCH_W),
            pl.BlockSpec(w_gate.shape, lambda i: (0, 0)),
            pl.BlockSpec(w_branch.shape, lambda i: (0, 0, 0)),
            pl.BlockSpec(w_out.shape, lambda i: (0, 0)),
            pl.BlockSpec((1, d), lambda i: (0, 0)),
            pl.BlockSpec((1, d), lambda i: (0, 0)),
        ],
        out_specs=rows(d),
        compiler_params=_params("parallel"),
        name="merge",
    )(h, y_rw, y_ml, y_dsa, w_gate, w_branch, w_out, gain_pre.reshape(1, d), gain_post.reshape(1, d))


def _split_w_in(w):
    w = w.astype(BF16)
    rw_end = RW_COLS
    ml_end = rw_end + 4 * BRANCH_W + 2 * ML_HEADS
    dsa_end = ml_end + DSA_TAIL_OFF + IDX_HEAD + IDX_HEADS
    w_ml = jnp.pad(w[:, rw_end:ml_end], ((0, 0), (0, ML_COLS_PAD - (ml_end - rw_end))))
    w_dsa = jnp.pad(w[:, ml_end:dsa_end], ((0, 0), (0, DSA_COLS_PAD - (dsa_end - ml_end))))
    return w[:, :rw_end], w_ml, w_dsa, w[:, dsa_end:]


def kernel(x, meta_tokens, norm_gain, ffn_w_in, ffn_w_out, w_in, rw_mu, rw_w0, rw_w_up, rw_a0, rw_a_up, rw_g_up, rw_k_k, rw_k_a, rw_r_k, rw_gn_gain, rw_gn_bias, ml_conv_w, ml_conv_b, ml_i_bias, ml_f_bias, ml_norm_gain, dsa_kv_norm, dsa_w_uk, dsa_w_uv, w_branch, w_out):
    bsz, seq, d = x.shape
    depth = norm_gain.shape[0]
    top_k = min(TOPK_MAX, seq // 4)
    t_len = seq + N_META
    t_pad = -(-t_len // SEQ_PAD_MULTIPLE) * SEQ_PAD_MULTIPLE
    h = jnp.concatenate([
        jnp.broadcast_to(meta_tokens.astype(x.dtype)[None], (bsz, N_META, d)), x,
        jnp.zeros((bsz, t_pad - t_len, d), x.dtype)], axis=1).reshape(bsz * t_pad, d)

    for l in range(depth):
        g = norm_gain[l]
        h = _ffn(h, g[0], g[1], ffn_w_in[l, 0].astype(BF16), ffn_w_out[l, 0].astype(BF16))

        w_rw, w_ml, w_dsa, w_gate = _split_w_in(w_in[l])
        p_rw, p_ml, p_dsa = (p.reshape(bsz, t_pad, -1) for p in _proj(h, g[2], (w_rw, w_ml, w_dsa)))

        y_rw = _rwkv(p_rw, rw_mu[l], rw_w0[l], rw_w_up[l], rw_a0[l], rw_a_up[l], rw_g_up[l], rw_k_k[l],
                     rw_k_a[l], rw_r_k[l], rw_gn_gain[l], rw_gn_bias[l])
        gate_bias = jnp.pad(jnp.concatenate([ml_i_bias[l], ml_f_bias[l]]), (0, 128 - 2 * ML_HEADS))
        y_ml = _mlstm(p_ml, ml_conv_w[l], ml_conv_b[l], gate_bias.reshape(1, 128), ml_norm_gain[l])
        y_dsa = _dsa(p_dsa, dsa_kv_norm[l], jnp.swapaxes(dsa_w_uk[l], 1, 2).astype(BF16), dsa_w_uv[l].astype(BF16), top_k)

        flat = lambda y: y.reshape(bsz * t_pad, BRANCH_W)
        h = _merge(h, flat(y_rw), flat(y_ml), flat(y_dsa), w_gate, w_branch[l].astype(BF16),
                   w_out[l].astype(BF16), g[2], g[3])
        ffn2 = (g[4], g[5], ffn_w_in[l, 1].astype(BF16), ffn_w_out[l, 1].astype(BF16))
        if l + 1 < depth:
            h = _ffn(h, *ffn2)
    return _ffn_tail(h.reshape(bsz, t_pad, d), seq, *ffn2)
```

```python
import functools
import math

import jax
import jax.numpy as jnp
from jax import lax
from jax.experimental import pallas as pl
from jax.experimental.pallas import tpu as pltpu

F32 = jnp.float32
BF16 = jnp.bfloat16
HI = lax.Precision.HIGHEST

D_MODEL = 1024
D_FF = 2816
N_META = 16
STREAM_CHUNK = 64
BRANCH_W = 512
NORM_EPS = 1e-6

RW_HEADS, RW_HEAD = 8, 64
RW_COLS = 1792
RW_GN_EPS = 64e-5
RW_CHUNK = 64

ML_HEADS, ML_HEAD = 4, 128
ML_CHUNK = 64
ML_CONV = 4
ML_COLS_PAD = 2176

DSA_HEADS, DSA_HEAD, DSA_LATENT = 8, 64, 128
IDX_HEADS, IDX_HEAD = 8, 64
TOPK_MAX = 256
DSA_QBLOCK = 128
DSA_COLS_PAD = 1280
DSA_CKV_OFF = BRANCH_W
DSA_QIDX_OFF = DSA_CKV_OFF + DSA_LATENT
DSA_TAIL_OFF = DSA_QIDX_OFF + IDX_HEADS * IDX_HEAD

ROW_TILE = 512
FFN_ROW_TILE = 1024
MXU_WIDTH = 256
FF_SLICES = ((0, 6 * MXU_WIDTH), (6 * MXU_WIDTH, D_FF))
SEQ_PAD_MULTIPLE = 128
RW_SEQ_GROUP = 4
ML_SEQ_GROUP = 4
VMEM_LIMIT = 56 * 1024 * 1024
DSA_KEY_CHUNK = 512
DSA_SUM_ROWS = 16
DSA_BISECT_STEPS = 16
DSA_BISECT_REFINE = 4
DSA_BISECT_EXTRA_ROUNDS = 2
DSA_BISECT_MAX_ROUNDS = 80


def _dot(a, b):
    return jnp.dot(a.astype(BF16), b.astype(BF16), preferred_element_type=F32)


def _dot_nt(a, b):
    return lax.dot_general(a.astype(BF16), b.astype(BF16), (((1,), (1,)), ((), ())),
                           preferred_element_type=F32)


def _dot_tn(a, b):
    return lax.dot_general(a.astype(BF16), b.astype(BF16), (((0,), (0,)), ((), ())),
                           preferred_element_type=F32)


def _dot_hi(a, b):
    return jnp.dot(a, b, preferred_element_type=F32, precision=HI)


def _rms(x, gain):
    return x * lax.rsqrt(jnp.mean(x * x, axis=-1, keepdims=True) + NORM_EPS) * gain


def _sigmoid(x):
    return jax.nn.sigmoid(x)


def _tril(n, strict=False):
    r = lax.broadcasted_iota(jnp.int32, (n, n), 0)
    c = lax.broadcasted_iota(jnp.int32, (n, n), 1)
    return (c < r) if strict else (c <= r)


def _params(*sem):
    return pltpu.CompilerParams(dimension_semantics=sem, vmem_limit_bytes=VMEM_LIMIT)


def _ffn_rows(h, gpre_ref, gpost_ref, wi_ref, wo_ref):
    xn = _rms(h, gpre_ref[...]).astype(BF16)
    out = None
    for lo, hi in FF_SLICES:
        gate = jnp.dot(xn, wi_ref[:, lo:hi], preferred_element_type=F32)
        up = jnp.dot(xn, wi_ref[:, D_FF + lo:D_FF + hi], preferred_element_type=F32)
        act = (gate * _sigmoid(gate) * up).astype(BF16)
        part = jnp.dot(act, wo_ref[lo:hi, :], preferred_element_type=F32)
        out = part if out is None else out + part
    return h + 0.5 * _rms(out, gpost_ref[...])


def _ffn_kernel(h_ref, gpre_ref, gpost_ref, wi_ref, wo_ref, o_ref):
    o_ref[...] = _ffn_rows(h_ref[...], gpre_ref, gpost_ref, wi_ref, wo_ref)


def _ffn_tail_kernel(h_ref, next_ref, gpre_ref, gpost_ref, wi_ref, wo_ref, o_ref):
    h = jnp.concatenate([h_ref[0, N_META:, :], next_ref[0]], axis=0)
    o_ref[0] = _ffn_rows(h, gpre_ref, gpost_ref, wi_ref, wo_ref)


def _ffn_specs(d, w_in, w_out, index):
    resident = lambda shape: pl.BlockSpec(shape, lambda *_: (0, 0), pipeline_mode=pl.Buffered(1))
    return [pl.BlockSpec((1, d), index), pl.BlockSpec((1, d), index), resident(w_in.shape), resident(w_out.shape)]


def _ffn(h, g_pre, g_post, w_in, w_out):
    m, d = h.shape
    return pl.pallas_call(
        _ffn_kernel,
        out_shape=jax.ShapeDtypeStruct((m, d), F32),
        grid=(m // FFN_ROW_TILE,),
        in_specs=[pl.BlockSpec((FFN_ROW_TILE, d), lambda i: (i, 0))] + _ffn_specs(d, w_in, w_out, lambda i: (0, 0)),
        out_specs=pl.BlockSpec((FFN_ROW_TILE, d), lambda i: (i, 0)),
        compiler_params=_params("parallel"),
        name="ffn",
    )(h, g_pre.reshape(1, d), g_post.reshape(1, d), w_in, w_out)


def _ffn_tail(h, seq, g_pre, g_post, w_in, w_out):
    bsz, t_pad, d = h.shape
    tile = FFN_ROW_TILE
    assert seq % tile == 0 and tile % N_META == 0 and seq + N_META <= t_pad
    return pl.pallas_call(
        _ffn_tail_kernel,
        out_shape=jax.ShapeDtypeStruct((bsz, seq, d), F32),
        grid=(bsz, seq // tile),
        in_specs=[
            pl.BlockSpec((1, tile, d), lambda b, i: (b, i, 0)),
            pl.BlockSpec((1, N_META, d), lambda b, i: (b, (i + 1) * (tile // N_META), 0)),
        ] + _ffn_specs(d, w_in, w_out, lambda b, i: (0, 0)),
        out_specs=pl.BlockSpec((1, tile, d), lambda b, i: (b, i, 0)),
        compiler_params=_params("parallel", "parallel"),
        name="ffn_tail",
    )(h, h, g_pre.reshape(1, d), g_post.reshape(1, d), w_in, w_out)


def _proj_kernel(h_ref, g_ref, *refs):
    w_refs, o_refs = refs[:len(refs) // 2], refs[len(refs) // 2:]
    xn = _rms(h_ref[...], g_ref[...]).astype(BF16)
    for w_ref, o_ref in zip(w_refs, o_refs):
        o_ref[...] = jnp.dot(xn, w_ref[...], preferred_element_type=F32)


def _proj(h, gain, weights):
    m, d = h.shape
    return pl.pallas_call(
        _proj_kernel,
        out_shape=[jax.ShapeDtypeStruct((m, w.shape[1]), F32) for w in weights],
        grid=(m // ROW_TILE,),
        in_specs=[
            pl.BlockSpec((ROW_TILE, d), lambda i: (i, 0)),
            pl.BlockSpec((1, d), lambda i: (0, 0)),
        ] + [pl.BlockSpec(w.shape, lambda i: (0, 0)) for w in weights],
        out_specs=[pl.BlockSpec((ROW_TILE, w.shape[1]), lambda i: (i, 0)) for w in weights],
        compiler_params=_params("parallel"),
        name="in_proj",
    )(h, gain.reshape(1, d), *weights)


def _rwkv_kernel(p_ref, mu_ref, w0_ref, wup_ref, a0_ref, aup_ref, gup_ref, kk_ref, ka_ref, rk_ref,
                 gng_ref, gnb_ref, hs_ref, o_ref, prev_ref, s_ref):
    @pl.when(pl.program_id(1) == 0)
    def _():
        prev_ref[...] = jnp.zeros_like(prev_ref)
        s_ref[...] = jnp.zeros_like(s_ref)

    L = RW_CHUNK
    lower = _tril(L)
    strict = _tril(L, strict=True)
    tril_f = lower.astype(F32)
    row = lax.broadcasted_iota(jnp.int32, (L, 1), 0)
    steps = int(math.log2(L))
    seqs = range(p_ref.shape[0])

    def head_sum(x):
        return _dot(x, hs_ref[...])

    r, k2, v, gate, g_last, r_rows, a_rows, b_rows, k_rows, v_b = ([] for _ in range(10))
    for g in seqs:
        p = p_ref[g]
        prev = jnp.where(row == 0, prev_ref[g], pltpu.roll(p, 1, 0))
        prev_ref[g] = p[L - 1:L, :]
        ps = p + (prev - p) * mu_ref[...]
        k = ps[:, BRANCH_W:2 * BRANCH_W]
        d = w0_ref[...] + _dot(jnp.tanh(ps[:, 1536:1600]), wup_ref[...])
        log_w = -math.exp(-0.5) * _sigmoid(d)
        a = _sigmoid(a0_ref[...] + _dot(ps[:, 1600:1664], aup_ref[...]))
        gate.append(_dot(_sigmoid(ps[:, 1664:1792]), gup_ref[...]))
        r.append(ps[:, 0:BRANCH_W])
        v.append(ps[:, 2 * BRANCH_W:3 * BRANCH_W])
        kappa = k * kk_ref[...]
        kh = kappa / jnp.maximum(jnp.sqrt(head_sum(kappa * kappa)), 1e-12)
        k2.append(k * (1.0 + (a - 1.0) * ka_ref[...]))
        cum = _dot_hi(tril_f, log_w)
        g_incl = jnp.exp(cum)
        g_inv = jnp.exp(-cum)
        g_last.append(g_incl[L - 1:L, :])
        r_rows.append((r[g] * g_incl).astype(BF16))
        a_rows.append((-kh * jnp.exp(cum - log_w)).astype(BF16))
        b_rows.append((a * kh * g_inv).astype(BF16))
        k_rows.append((k2[g] * g_inv).astype(BF16))
        v_b.append(v[g].astype(BF16))

    units = [(g, h) for g in seqs for h in range(RW_HEADS)]
    idx = range(len(units))
    sl = [slice(h * RW_HEAD, (h + 1) * RW_HEAD) for _, h in units]
    lhs = [jnp.concatenate([r_rows[g][:, sl[i]], a_rows[g][:, sl[i]]], axis=0) for i, (g, h) in enumerate(units)]
    b_u = [b_rows[g][:, sl[i]] for i, (g, h) in enumerate(units)]
    k_u = [k_rows[g][:, sl[i]] for i, (g, h) in enumerate(units)]
    v_u = [v_b[g][:, sl[i]] for i, (g, h) in enumerate(units)]
    s0 = [s_ref[g * RW_HEADS + h] for g, h in units]
    gram = [_dot_nt(lhs[i], jnp.concatenate([b_u[i], k_u[i]], axis=0)) for i in idx]
    ls0 = [_dot_nt(lhs[i], s0[i]) for i in idx]
    u = [ls0[i][L:] + _dot(jnp.where(strict, gram[i][L:, L:], 0.0), v_u[i]) for i in idx]
    n_pow = [jnp.where(strict, gram[i][L:, :L], 0.0) for i in idx]
    for step in range(steps):
        u = [u[i] + _dot(n_pow[i], u[i]) for i in idx]
        if step + 1 < steps:
            n_pow = [_dot(n_pow[i], n_pow[i]) for i in idx]

    y_u = [ls0[i][:L] + _dot(jnp.where(lower, gram[i][:L, :L], 0.0), u[i])
           + _dot(jnp.where(lower, gram[i][:L, L:], 0.0), v_u[i]) for i in idx]
    for i, (g, h) in enumerate(units):
        s_ref[g * RW_HEADS + h] = (s0[i] + _dot_tn(u[i], b_u[i]) + _dot_tn(v_u[i], k_u[i])) * g_last[g][:, sl[i]]

    for g in seqs:
        y = jnp.concatenate(y_u[g * RW_HEADS:(g + 1) * RW_HEADS], axis=1)
        dev = y - head_sum(y) * (1.0 / RW_HEAD)
        var = head_sum(dev * dev) * (1.0 / RW_HEAD)
        y = dev * lax.rsqrt(var + RW_GN_EPS) * gng_ref[...] + gnb_ref[...]
        bonus = head_sum(r[g] * k2[g] * rk_ref[...])
        o_ref[g] = (y + bonus * v[g]) * gate[g]


def _rwkv(p, mu, w0, w_up, a0, a_up, g_up, k_k, k_a, r_k, gn_gain, gn_bias):
    b, t, c = p.shape
    group = math.gcd(b, RW_SEQ_GROUP)
    head_of = jnp.arange(BRANCH_W) // RW_HEAD
    same_head = (head_of[:, None] == head_of[None, :]).astype(BF16)
    row = lambda x: x.reshape(1, -1)
    vec = lambda n: pl.BlockSpec((1, n), lambda i, j: (0, 0))
    mat = lambda shp: pl.BlockSpec(shp, lambda i, j: (0, 0))
    return pl.pallas_call(
        _rwkv_kernel,
        out_shape=jax.ShapeDtypeStruct((b, t, BRANCH_W), F32),
        grid=(b // group, t // RW_CHUNK),
        in_specs=[
            pl.BlockSpec((group, RW_CHUNK, c), lambda i, j: (i, j, 0)),
            vec(c), vec(BRANCH_W), mat(w_up.shape), vec(BRANCH_W), mat(a_up.shape), mat(g_up.shape),
            vec(BRANCH_W), vec(BRANCH_W), vec(BRANCH_W), vec(BRANCH_W), vec(BRANCH_W),
            mat((BRANCH_W, BRANCH_W)),
        ],
        out_specs=pl.BlockSpec((group, RW_CHUNK, BRANCH_W), lambda i, j: (i, j, 0)),
        scratch_shapes=[pltpu.VMEM((group, 1, c), F32),
                        pltpu.VMEM((group * RW_HEADS, RW_HEAD, RW_HEAD), F32)],
        compiler_params=_params("parallel", "arbitrary"),
        name="rwkv7",
    )(p, row(mu), row(w0), w_up, row(a0), a_up, g_up, row(k_k), row(k_a), row(r_k), row(gn_gain),
      row(gn_bias), same_head)


def _log_sigmoid(x):
    return jnp.minimum(x, 0.0) - jnp.log1p(jnp.exp(-jnp.abs(x)))


def _mlstm_kernel(p_ref, cw_ref, cb_ref, gb_ref, ng_ref, o_ref, tail_ref, c_ref, n_ref, m_ref):
    @pl.when(pl.program_id(1) == 0)
    def _():
        tail_ref[...] = jnp.zeros_like(tail_ref)
        c_ref[...] = jnp.zeros_like(c_ref)
        n_ref[...] = jnp.zeros_like(n_ref)
        m_ref[...] = jnp.zeros_like(m_ref)

    L = ML_CHUNK
    qk_w = 2 * BRANCH_W
    lower = _tril(L)
    tril_f = lower.astype(F32)
    seqs = range(p_ref.shape[0])

    x = [p_ref[g] for g in seqs]
    q, k, gates, b_all, b_t, g_t = [], [], [], [], [], []
    for g in seqs:
        qk_in = x[g][:, :qk_w]
        ext = jnp.concatenate([tail_ref[g], qk_in], axis=0)
        tail_ref[g] = qk_in[L - 8:L, :]
        conv = cb_ref[...] + cw_ref[ML_CONV - 1:ML_CONV, :] * qk_in
        for j in range(ML_CONV - 1):
            conv = conv + cw_ref[j:j + 1, :] * pltpu.roll(ext, ML_CONV - 1 - j, 0)[8:8 + L]
        qk = conv * _sigmoid(conv)
        q.append(qk[:, :BRANCH_W])
        k.append(qk[:, BRANCH_W:] * ML_HEAD ** -0.5)
        gates.append(x[g][:, qk_w + 2 * BRANCH_W:] + gb_ref[...])
        b_all.append(_dot_hi(tril_f, _log_sigmoid(gates[g])))
        b_t.append(b_all[g].T)
        g_t.append(gates[g].T)

    units = [(g, h) for g in seqs for h in range(ML_HEADS)]
    sl = [slice(h * ML_HEAD, (h + 1) * ML_HEAD) for _, h in units]
    q_u = [q[g][:, sl[i]] for i, (g, h) in enumerate(units)]
    k_u = [k[g][:, sl[i]] for i, (g, h) in enumerate(units)]
    v_u = [x[g][:, qk_w + h * ML_HEAD:qk_w + (h + 1) * ML_HEAD] for g, h in units]
    b_col = [b_all[g][:, ML_HEADS + h:ML_HEADS + h + 1] for g, h in units]
    b_row = [b_t[g][ML_HEADS + h:ML_HEADS + h + 1, :] for g, h in units]
    i_col = [gates[g][:, h:h + 1] for g, h in units]
    i_row = [g_t[g][h:h + 1, :] for g, h in units]
    m_st = [m_ref[g, h:h + 1, 0:1] for g, h in units]
    n_st = [n_ref[g, h:h + 1, :] for g, h in units]
    c_st = [c_ref[g * ML_HEADS + h] for g, h in units]
    idx = range(len(units))

    log_d = [jnp.where(lower, b_col[i] - b_row[i] + i_row[i], -jnp.inf) for i in idx]
    m_j = [jnp.maximum(b_col[i] + m_st[i], jnp.max(log_d[i], axis=-1, keepdims=True)) for i in idx]
    s = [_dot_nt(q_u[i], k_u[i]) * jnp.exp(log_d[i] - m_j[i]) for i in idx]
    inter = [jnp.exp(b_col[i] + m_st[i] - m_j[i]) for i in idx]
    num = [_dot(s[i], v_u[i]) + inter[i] * _dot_nt(q_u[i], c_st[i]) for i in idx]
    den = [jnp.sum(s[i], axis=-1, keepdims=True)
           + inter[i] * jnp.sum(q_u[i] * n_st[i], axis=-1, keepdims=True) for i in idx]
    hh = [num[i] / jnp.maximum(jnp.abs(den[i]), jnp.exp(-m_j[i])) for i in idx]

    g_tot = [b_col[i][L - 1:L, :] for i in idx]
    w_log = [g_tot[i] - b_col[i] + i_col[i] for i in idx]
    m_new = [jnp.maximum(g_tot[i] + m_st[i], jnp.max(w_log[i], axis=0, keepdims=True)) for i in idx]
    wgt = [jnp.exp(w_log[i] - m_new[i]) for i in idx]
    dec = [jnp.exp(g_tot[i] + m_st[i] - m_new[i]) for i in idx]
    c_new = [dec[i] * c_st[i] + _dot_tn(wgt[i] * v_u[i], k_u[i]) for i in idx]
    n_new = [dec[i] * n_st[i] + jnp.sum(wgt[i] * k_u[i], axis=0, keepdims=True) for i in idx]
    hn = [hh[i] * lax.rsqrt(jnp.mean(hh[i] * hh[i], axis=-1, keepdims=True) + NORM_EPS) * ng_ref[:, sl[i]]
          for i in idx]

    for i, (g, h) in enumerate(units):
        c_ref[g * ML_HEADS + h] = c_new[i]
        n_ref[g, h:h + 1, :] = n_new[i]
        m_ref[g, h:h + 1, :] = jnp.broadcast_to(m_new[i], (1, ML_HEAD))
        o_gate = x[g][:, qk_w + BRANCH_W + h * ML_HEAD:qk_w + BRANCH_W + (h + 1) * ML_HEAD]
        o_ref[g, :, sl[i]] = _sigmoid(o_gate) * hn[i]


def _mlstm(p, conv_w, conv_b, gate_bias, norm_gain):
    b, t, c = p.shape
    group = math.gcd(b, ML_SEQ_GROUP)
    return pl.pallas_call(
        _mlstm_kernel,
        out_shape=jax.ShapeDtypeStruct((b, t, BRANCH_W), F32),
        grid=(b // group, t // ML_CHUNK),
        in_specs=[
            pl.BlockSpec((group, ML_CHUNK, c), lambda i, j: (i, j, 0)),
            pl.BlockSpec(conv_w.shape, lambda i, j: (0, 0)),
            pl.BlockSpec((1, 2 * BRANCH_W), lambda i, j: (0, 0)),
            pl.BlockSpec((1, 128), lambda i, j: (0, 0)),
            pl.BlockSpec((1, BRANCH_W), lambda i, j: (0, 0)),
        ],
        out_specs=pl.BlockSpec((group, ML_CHUNK, BRANCH_W), lambda i, j: (i, j, 0)),
        scratch_shapes=[
            pltpu.VMEM((group, 8, 2 * BRANCH_W), F32),
            pltpu.VMEM((group * ML_HEADS, ML_HEAD, ML_HEAD), F32),
            pltpu.VMEM((group, 8, ML_HEAD), F32),
            pltpu.VMEM((group, 8, ML_HEAD), F32),
        ],
        compiler_params=_params("parallel", "arbitrary"),
        name="mlstm",
    )(p, conv_w, conv_b.reshape(1, -1), gate_bias, norm_gain.reshape(1, -1))


def _sum_sublane_groups(x):
    n = x.shape[0] // 32
    g = x.reshape(4 * n, 8, x.shape[1])
    parts = [g[i * n:(i + 1) * n] for i in range(4)]
    return (jnp.sum(parts[0], axis=0) + jnp.sum(parts[1], axis=0)) + (jnp.sum(parts[2], axis=0) + jnp.sum(parts[3], axis=0))


def _dsa_kernel(pq_ref, pkv_ref, pki_ref, kvn_ref, wukt_ref, wuv_ref, o_ref, ckv_ref, ckvt_ref, kidx_ref, sc_ref,
                *, top_k):
    qb = pl.program_id(1)
    tq = DSA_QBLOCK
    kc = DSA_KEY_CHUNK
    tk = pkv_ref.shape[1]
    n_chunks_max = ckvt_ref.shape[0]
    hq = DSA_HEADS * tq

    @pl.when(qb == 0)
    def _():
        ckv = _rms(pkv_ref[0], kvn_ref[...])
        pad_rows = n_chunks_max * kc - tk
        ckv_ref[0:tk, :] = ckv.astype(BF16)
        ckv_ref[tk:, :] = jnp.zeros((pad_rows, DSA_LATENT), BF16)
        kidx_ref[0:tk, :] = pki_ref[0][:, :IDX_HEAD].astype(BF16)
        kidx_ref[tk:, :] = jnp.zeros((pad_rows, IDX_HEAD), BF16)
        ckv_t = jnp.concatenate([ckv.T, jnp.zeros((DSA_LATENT, pad_rows), F32)], axis=1)
        extra = jnp.where(lax.broadcasted_iota(jnp.int32, (DSA_SUM_ROWS, kc), 0) == 0, 1.0, 0.0)
        for c in range(n_chunks_max):
            ckvt_ref[c] = jnp.concatenate([ckv_t[:, c * kc:(c + 1) * kc], extra], axis=0).astype(BF16)

    n_chunks = jnp.minimum(lax.shift_right_logical(qb * tq + tq + N_META + kc - 1, int(math.log2(kc))),
                           n_chunks_max)
    q_pos = qb * tq + lax.broadcasted_iota(jnp.int32, (1, tq), 1)
    q_chunk = jnp.where(q_pos < N_META, 0,
                        1 + lax.shift_right_arithmetic(q_pos - N_META, int(math.log2(STREAM_CHUNK))))
    n_allowed = jnp.minimum(N_META + STREAM_CHUNK * q_chunk, tk)

    def rows_of(c):
        return pl.ds(pl.multiple_of(c * kc, kc), kc)

    pq = pq_ref[0]
    q = pq[:, :BRANCH_W]
    qi_t = jnp.concatenate([pq[:, DSA_QIDX_OFF + j * 128:DSA_QIDX_OFF + (j + 1) * 128].T
                            for j in range(IDX_HEADS * IDX_HEAD // 128)], axis=0)
    qi_t = jnp.concatenate([qi_t[h * IDX_HEAD:(h + 1) * IDX_HEAD] for h in range(IDX_HEADS)], axis=1)
    tail_t = pq[:, DSA_TAIL_OFF:].T
    w_t = tail_t[IDX_HEAD:IDX_HEAD + IDX_HEADS] * (IDX_HEADS * IDX_HEAD) ** -0.5
    w_row = jnp.concatenate([w_t[h:h + 1] for h in range(IDX_HEADS)], axis=1)
    qi_t = qi_t.astype(BF16)

    def score_chunk(c, bounds):
        lo, hi = bounds
        s = jnp.maximum(jnp.dot(kidx_ref[rows_of(c), :], qi_t, preferred_element_type=F32), 0.0) * w_row
        score = s[:, 0:tq]
        for h in range(1, IDX_HEADS):
            score = score + s[:, h * tq:(h + 1) * tq]
        k_pos = c * kc + lax.broadcasted_iota(jnp.int32, (kc, 1), 0)
        allowed = k_pos < n_allowed
        sc_ref[rows_of(c), :] = jnp.where(allowed, score, -jnp.inf)
        lo = jnp.minimum(lo, jnp.min(jnp.where(allowed, score, jnp.inf), axis=0, keepdims=True))
        hi = jnp.maximum(hi, jnp.max(jnp.where(allowed, score, -jnp.inf), axis=0, keepdims=True))
        return lo, hi

    lo, row_max = lax.fori_loop(0, n_chunks, score_chunk,
                                (jnp.full((1, tq), jnp.inf, F32), jnp.full((1, tq), -jnp.inf, F32)))
    hi = row_max + jnp.maximum(jnp.abs(row_max) * 2.0 ** -20, 1e-30)

    def count(pred, level):
        def body(c, acc):
            return acc + _sum_sublane_groups(jnp.where(pred(sc_ref[rows_of(c), :], level), 1.0, 0.0))
        return jnp.sum(lax.fori_loop(0, n_chunks, body, jnp.zeros((8, tq), F32)), axis=0, keepdims=True)

    def bisect(_, bracket):
        lo, hi, n_lo = bracket
        mid = 0.5 * lo + 0.5 * hi
        n_mid = count(jnp.greater_equal, mid)
        ge = n_mid >= top_k
        return jnp.where(ge, mid, lo), jnp.where(ge, hi, mid), jnp.where(ge, n_mid, n_lo)

    def snap(lo):
        def body(c, val):
            s = sc_ref[rows_of(c), :]
            return jnp.minimum(val, jnp.min(jnp.where(s >= lo, s, jnp.inf), axis=0, keepdims=True))
        val = lax.fori_loop(0, n_chunks, body, jnp.full((1, tq), jnp.inf, F32))
        return val, count(jnp.greater, val)

    bracket = (lo, hi, n_allowed.astype(F32))
    bracket = lax.fori_loop(0, DSA_BISECT_STEPS, bisect, bracket)

    def keeps_too_many(state):
        return jnp.logical_and(jnp.max(state[0][2]) > top_k, state[1] < DSA_BISECT_EXTRA_ROUNDS)

    def halve_more(state):
        return lax.fori_loop(0, DSA_BISECT_REFINE, bisect, state[0]), state[1] + 1

    bracket, _ = lax.while_loop(keeps_too_many, halve_more, (bracket, jnp.int32(0)))

    some_row_keeps_too_many = (jnp.max(bracket[2]) > top_k).astype(jnp.int32)
    thr, n_above = lax.fori_loop(0, some_row_keeps_too_many, lambda _, carry: snap(bracket[0]),
                                 (bracket[0], jnp.zeros((1, tq), F32)))

    def unsettled(state):
        return jnp.logical_and(jnp.max(state[2]) >= top_k, state[3] < DSA_BISECT_MAX_ROUNDS)

    def refine(state):
        bracket = lax.fori_loop(0, DSA_BISECT_REFINE, bisect, state[0])
        thr, n_above = snap(bracket[0])
        return bracket, thr, n_above, state[3] + 1

    bracket, thr, n_above, _ = lax.while_loop(unsettled, refine, (bracket, thr, n_above, jnp.int32(0)))

    @pl.when(jnp.max(bracket[2]) > top_k)
    def _():
        need = top_k - n_above
        earlier = (lax.broadcasted_iota(jnp.int32, (kc, kc), 1)
                   < lax.broadcasted_iota(jnp.int32, (kc, kc), 0)).astype(BF16)

        def body(c, seen):
            s = sc_ref[rows_of(c), :]
            tie = jnp.where(s == thr, 1.0, 0.0)
            rank = seen + jnp.dot(earlier, tie.astype(BF16), preferred_element_type=F32)
            sc_ref[rows_of(c), :] = jnp.where((tie > 0.0) & (rank >= need), -jnp.inf, s)
            return seen + jnp.sum(tie, axis=0, keepdims=True)

        lax.fori_loop(0, n_chunks, body, jnp.zeros((1, tq), F32))

    q_t = jnp.concatenate([_dot_nt(wukt_ref[h], q[:, h * DSA_HEAD:(h + 1) * DSA_HEAD])
                           for h in range(DSA_HEADS)], axis=1)
    q_t = (q_t * (DSA_HEAD ** -0.5 * math.log2(math.e))).astype(BF16)

    def finite_or_zero(m):
        return jnp.where(m == -jnp.inf, 0.0, m)

    def attend(chunks, m, acc):
        n = range(len(chunks))
        logits = [jnp.dot(ckv_ref[rows_of(c), :], q_t, preferred_element_type=F32)
                  + jnp.concatenate([jnp.where(sc_ref[rows_of(c), :] >= thr, 0.0, -jnp.inf)] * DSA_HEADS, axis=1)
                  for c in chunks]
        m_new = [jnp.maximum(m[s], jnp.max(logits[s], axis=0, keepdims=True)) for s in n]
        m_safe = [finite_or_zero(m_new[s]) for s in n]
        e = [jnp.exp2(logits[s] - m_safe[s]).astype(BF16) for s in n]
        acc = [acc[s] * jnp.exp2(m[s] - m_safe[s])
               + jnp.dot(ckvt_ref[chunks[s]], e[s], preferred_element_type=F32) for s in n]
        return m_new, acc

    def attend_pair(i, carry):
        m, acc = attend([2 * i, 2 * i + 1], carry[:2], carry[2:])
        return m[0], m[1], acc[0], acc[1]

    def attend_last(_, carry):
        m, acc = attend([n_chunks - 1], carry[:1], carry[2:3])
        return m[0], carry[1], acc[0], carry[3]

    m_init = jnp.full((1, hq), -jnp.inf, F32)
    acc_init = jnp.zeros((DSA_LATENT + DSA_SUM_ROWS, hq), F32)
    streams = lax.fori_loop(0, lax.shift_right_logical(n_chunks, 1), attend_pair,
                            (m_init, m_init, acc_init, acc_init))
    m0, m1, acc0, acc1 = lax.fori_loop(0, n_chunks & 1, attend_last, streams)
    m_safe = finite_or_zero(jnp.maximum(m0, m1))
    acc = acc0 * jnp.exp2(m0 - m_safe) + acc1 * jnp.exp2(m1 - m_safe)
    out_t = acc[:DSA_LATENT] / acc[DSA_LATENT:DSA_LATENT + 1]
    for h in range(DSA_HEADS):
        o_ref[0, :, h * DSA_HEAD:(h + 1) * DSA_HEAD] = _dot_tn(out_t[:, h * tq:(h + 1) * tq], wuv_ref[h])


def _dsa(p, kv_norm, w_uk_t, w_uv, top_k):
    b, t, c = p.shape
    n_chunks = -(-t // DSA_KEY_CHUNK)
    return pl.pallas_call(
        functools.partial(_dsa_kernel, top_k=top_k),
        out_shape=jax.ShapeDtypeStruct((b, t, BRANCH_W), F32),
        grid=(b, t // DSA_QBLOCK),
        in_specs=[
            pl.BlockSpec((1, DSA_QBLOCK, c), lambda i, j: (i, j, 0)),
            pl.BlockSpec((1, t, DSA_LATENT), lambda i, j: (i, 0, DSA_CKV_OFF // DSA_LATENT)),
            pl.BlockSpec((1, t, 128), lambda i, j: (i, 0, DSA_TAIL_OFF // 128)),
            pl.BlockSpec((1, DSA_LATENT), lambda i, j: (0, 0)),
            pl.BlockSpec(w_uk_t.shape, lambda i, j: (0, 0, 0)),
            pl.BlockSpec(w_uv.shape, lambda i, j: (0, 0, 0)),
        ],
        out_specs=pl.BlockSpec((1, DSA_QBLOCK, BRANCH_W), lambda i, j: (i, j, 0)),
        scratch_shapes=[
            pltpu.VMEM((n_chunks * DSA_KEY_CHUNK, DSA_LATENT), BF16),
            pltpu.VMEM((n_chunks, DSA_LATENT + DSA_SUM_ROWS, DSA_KEY_CHUNK), BF16),
            pltpu.VMEM((n_chunks * DSA_KEY_CHUNK, IDX_HEAD), BF16),
            pltpu.VMEM((n_chunks * DSA_KEY_CHUNK, DSA_QBLOCK), F32),
        ],
        compiler_params=_params("parallel", "arbitrary"),
        name="dsa",
    )(p, p, p, kv_norm.reshape(1, -1), w_uk_t, w_uv)


def _merge_kernel(h_ref, yr_ref, ym_ref, yd_ref, wg_ref, wb_ref, wo_ref, gpre_ref, gpost_ref, o_ref):
    h = h_ref[...]
    xn = _rms(h, gpre_ref[...]).astype(BF16)
    mixed = None
    for i, y_ref in enumerate((yr_ref, ym_ref, yd_ref)):
        gate = jnp.dot(xn, wg_ref[:, i * D_MODEL:(i + 1) * D_MODEL], preferred_element_type=F32)
        proj = jnp.dot(y_ref[...].astype(BF16), wb_ref[i], preferred_element_type=F32)
        term = _sigmoid(gate) * proj
        mixed = term if mixed is None else mixed + term
    out = jnp.dot(mixed.astype(BF16), wo_ref[...], preferred_element_type=F32)
    o_ref[...] = h + _rms(out, gpost_ref[...])


def _merge(h, y_rw, y_ml, y_dsa, w_gate, w_branch, w_out, gain_pre, gain_post):
    m, d = h.shape
    rows = lambda n: pl.BlockSpec((ROW_TILE, n), lambda i: (i, 0))
    return pl.pallas_call(
        _merge_kernel,
        out_shape=jax.ShapeDtypeStruct((m, d), F32),
        grid=(m // ROW_TILE,),
        in_specs=[
            rows(d), rows(BRANCH_W), rows(BRANCH_W), rows(BRANCH_W),
            pl.BlockSpec(w_gate.shape, lambda i: (0, 0)),
            pl.BlockSpec(w_branch.shape, lambda i: (0, 0, 0)),
            pl.BlockSpec(w_out.shape, lambda i: (0, 0)),
            pl.BlockSpec((1, d), lambda i: (0, 0)),
            pl.BlockSpec((1, d), lambda i: (0, 0)),
        ],
        out_specs=rows(d),
        compiler_params=_params("parallel"),
        name="merge",
    )(h, y_rw, y_ml, y_dsa, w_gate, w_branch, w_out, gain_pre.reshape(1, d), gain_post.reshape(1, d))


def _split_w_in(w):
    w = w.astype(BF16)
    rw_end = RW_COLS
    ml_end = rw_end + 4 * BRANCH_W + 2 * ML_HEADS
    dsa_end = ml_end + DSA_TAIL_OFF + IDX_HEAD + IDX_HEADS
    w_ml = jnp.pad(w[:, rw_end:ml_end], ((0, 0), (0, ML_COLS_PAD - (ml_end - rw_end))))
    w_dsa = jnp.pad(w[:, ml_end:dsa_end], ((0, 0), (0, DSA_COLS_PAD - (dsa_end - ml_end))))
    return w[:, :rw_end], w_ml, w_dsa, w[:, dsa_end:]


def kernel(x, meta_tokens, norm_gain, ffn_w_in, ffn_w_out, w_in, rw_mu, rw_w0, rw_w_up, rw_a0, rw_a_up, rw_g_up, rw_k_k, rw_k_a, rw_r_k, rw_gn_gain, rw_gn_bias, ml_conv_w, ml_conv_b, ml_i_bias, ml_f_bias, ml_norm_gain, dsa_kv_norm, dsa_w_uk, dsa_w_uv, w_branch, w_out):
    bsz, seq, d = x.shape
    depth = norm_gain.shape[0]
    top_k = min(TOPK_MAX, seq // 4)
    t_len = seq + N_META
    t_pad = -(-t_len // SEQ_PAD_MULTIPLE) * SEQ_PAD_MULTIPLE
    h = jnp.concatenate([
        jnp.broadcast_to(meta_tokens.astype(x.dtype)[None], (bsz, N_META, d)), x,
        jnp.zeros((bsz, t_pad - t_len, d), x.dtype)], axis=1).reshape(bsz * t_pad, d)

    for l in range(depth):
        g = norm_gain[l]
        h = _ffn(h, g[0], g[1], ffn_w_in[l, 0].astype(BF16), ffn_w_out[l, 0].astype(BF16))

        w_rw, w_ml, w_dsa, w_gate = _split_w_in(w_in[l])
        p_rw, p_ml, p_dsa = (p.reshape(bsz, t_pad, -1) for p in _proj(h, g[2], (w_rw, w_ml, w_dsa)))

        y_rw = _rwkv(p_rw, rw_mu[l], rw_w0[l], rw_w_up[l], rw_a0[l], rw_a_up[l], rw_g_up[l], rw_k_k[l],
                     rw_k_a[l], rw_r_k[l], rw_gn_gain[l], rw_gn_bias[l])
        gate_bias = jnp.pad(jnp.concatenate([ml_i_bias[l], ml_f_bias[l]]), (0, 128 - 2 * ML_HEADS))
        y_ml = _mlstm(p_ml, ml_conv_w[l], ml_conv_b[l], gate_bias.reshape(1, 128), ml_norm_gain[l])
        y_dsa = _dsa(p_dsa, dsa_kv_norm[l], jnp.swapaxes(dsa_w_uk[l], 1, 2).astype(BF16), dsa_w_uv[l].astype(BF16), top_k)

        flat = lambda y: y.reshape(bsz * t_pad, BRANCH_W)
        h = _merge(h, flat(y_rw), flat(y_ml), flat(y_dsa), w_gate, w_branch[l].astype(BF16),
                   w_out[l].astype(BF16), g[2], g[3])
        ffn2 = (g[4], g[5], ffn_w_in[l, 1].astype(BF16), ffn_w_out[l, 1].astype(BF16))
        if l + 1 < depth:
            h = _ffn(h, *ffn2)
    return _ffn_tail(h.reshape(bsz, t_pad, d), seq, *ffn2)
```
